```python
import jax, jax.numpy as jnp
from jax import lax
import numpy as np

D_MODEL = 1024
BATCH = 8
SEQ = 4096
DEPTH = 1

HEAD_DIM = 64
RET_HEADS = 8
SWA_Q_HEADS = 8
SWA_KV_HEADS = 2
SWA_GROUP = SWA_Q_HEADS // SWA_KV_HEADS
RET_WIDTH = RET_HEADS * HEAD_DIM
SWA_WIDTH = SWA_Q_HEADS * HEAD_DIM
KV_WIDTH = SWA_KV_HEADS * HEAD_DIM
MIX_WIDTH = RET_WIDTH + SWA_WIDTH
IN_WIDTH = 4 * RET_WIDTH + SWA_WIDTH + 2 * KV_WIDTH
CHUNK = 128
WINDOW = 128
RET_THETA = 10000.0
SWA_THETA = 500000.0
SWA_ROT_DIM = HEAD_DIM // 4
N_EXPERTS = 64
TOP_K = 8
N_GROUPS = 8
TOPK_GROUPS = 4
EXPERT_DIM = 256
SHARED_DIM = 256
ROUTED_SCALE = 2.5
PLE_DIM = 256
LN_EPS = 1e-5
GN_EPS = 1e-6
NEG_INF = -1e30
DEEPNORM_ALPHA = (2.0 * DEPTH) ** 0.25
DEEPNORM_BETA = (8.0 * DEPTH) ** -0.25

kernel_name = "hymba_retnet_swa_sink_moe_deepnorm"


def rope(x, rot_dim, theta):
    s = x.shape[1]
    half = rot_dim // 2
    inv_freq = theta ** (-jnp.arange(half, dtype=jnp.float32) / half)
    ang = jnp.arange(s, dtype=jnp.float32)[:, None] * inv_freq[None, :]
    cos = jnp.cos(ang)[None, :, None, :].astype(x.dtype)
    sin = jnp.sin(ang)[None, :, None, :].astype(x.dtype)
    x1, x2, rest = x[..., :half], x[..., half:rot_dim], x[..., rot_dim:]
    return jnp.concatenate([x1 * cos - x2 * sin, x2 * cos + x1 * sin, rest], axis=-1)


def layer_norm(x, g, b):
    xf = x.astype(jnp.float32)
    mu = jnp.mean(xf, axis=-1, keepdims=True)
    var = jnp.mean(jnp.square(xf - mu), axis=-1, keepdims=True)
    y = (xf - mu) * lax.rsqrt(var + LN_EPS)
    return (y * g.astype(jnp.float32) + b.astype(jnp.float32)).astype(x.dtype)


def head_group_norm(y, gain):
    yf = y.astype(jnp.float32)
    mu = jnp.mean(yf, axis=-1, keepdims=True)
    var = jnp.mean(jnp.square(yf - mu), axis=-1, keepdims=True)
    yn = ((yf - mu) * lax.rsqrt(var + GN_EPS)).astype(y.dtype)
    b, s, h, d = y.shape
    return yn.reshape(b, s, h * d) * gain


def retention_chunkwise(q, k, v):
    b, s, h, d = q.shape
    n = s // CHUNK
    k = k * (d ** -0.5)
    q, k, v = [t.reshape(b, n, CHUNK, h, d) for t in (q, k, v)]
    log_g = jnp.log1p(-(2.0 ** (-5.0 - jnp.arange(h, dtype=jnp.float32))))
    idx = jnp.arange(CHUNK, dtype=jnp.float32)
    diff = idx[:, None] - idx[None, :]
    intra_decay = jnp.where(diff[None] >= 0,
                            jnp.exp(jnp.maximum(diff, 0.0)[None] * log_g[:, None, None]), 0.0)
    scores = jnp.einsum('bnqhd,bnkhd->bnhqk', q, k) * intra_decay.astype(q.dtype)
    intra = jnp.einsum('bnhqk,bnkhd->bnqhd', scores, v)
    k_w = jnp.exp((CHUNK - 1 - idx)[:, None] * log_g[None, :]).astype(k.dtype)
    q_w = jnp.exp((idx + 1)[:, None] * log_g[None, :]).astype(q.dtype)
    chunk_decay = jnp.exp(CHUNK * log_g)
    kv = jnp.einsum('bnkhd,bnkhe->nbhde', k * k_w[:, :, None], v)

    def step(state, kv_n):
        return state * chunk_decay[None, :, None, None].astype(state.dtype) + kv_n, state

    _, states = lax.scan(step, jnp.zeros_like(kv[0]), kv)
    inter = jnp.einsum('bnqhd,nbhde->bnqhe', q * q_w[:, :, None], states)
    return (intra + inter).reshape(b, s, h, d)


def swa_with_sinks(q, k, v, sinks):
    b, s, hq, d = q.shape
    n = s // WINDOW
    c = WINDOW
    q = q.reshape(b, n, c, SWA_KV_HEADS, SWA_GROUP, d)

    def band(t):
        t = t.reshape(b, n, c, SWA_KV_HEADS, d)
        prev = jnp.concatenate([jnp.zeros_like(t[:, :1]), t[:, :-1]], axis=1)
        return jnp.concatenate([prev, t], axis=2)

    kb, vb = band(k), band(v)
    scores = jnp.einsum('bnqhgd,bnkhd->bhgnqk', q, kb).astype(jnp.float32) * (d ** -0.5)
    qi = jnp.arange(c)[:, None]
    kj = jnp.arange(2 * c)[None, :]
    rel = c + qi - kj
    blk = jnp.arange(n)[:, None, None]
    valid = (rel >= 0) & (rel < WINDOW) & (blk * c + kj - c >= 0)
    scores = jnp.where(valid, scores, NEG_INF)
    sink = jnp.broadcast_to(sinks.astype(jnp.float32).reshape(1, SWA_KV_HEADS, SWA_GROUP, 1, 1, 1),
                            scores.shape[:-1] + (1,))
    probs = jax.nn.softmax(jnp.concatenate([scores, sink], axis=-1), axis=-1)[..., :-1]
    out = jnp.einsum('bhgnqk,bnkhd->bnqhgd', probs.astype(v.dtype), vb)
    return out.reshape(b, s, hq * d)


def hybrid_mixer(x, w_in, ret_gn_gain, attn_scale, sinks, w_out):
    b, s, _ = x.shape
    proj = x @ w_in
    splits = [RET_WIDTH, 2 * RET_WIDTH, 3 * RET_WIDTH, 4 * RET_WIDTH,
              4 * RET_WIDTH + SWA_WIDTH, 4 * RET_WIDTH + SWA_WIDTH + KV_WIDTH]
    rq, rk, rv, rg, aq, ak, av = jnp.split(proj, splits, axis=-1)
    rq = rope(rq.reshape(b, s, RET_HEADS, HEAD_DIM), HEAD_DIM, RET_THETA)
    rk = rope(rk.reshape(b, s, RET_HEADS, HEAD_DIM), HEAD_DIM, RET_THETA)
    rv = rv.reshape(b, s, RET_HEADS, HEAD_DIM)
    ret = retention_chunkwise(rq, rk, rv)
    ret = jax.nn.silu(rg) * head_group_norm(ret, ret_gn_gain)
    aq = rope(aq.reshape(b, s, SWA_Q_HEADS, HEAD_DIM), SWA_ROT_DIM, SWA_THETA)
    ak = rope(ak.reshape(b, s, SWA_KV_HEADS, HEAD_DIM), SWA_ROT_DIM, SWA_THETA)
    av = av.reshape(b, s, SWA_KV_HEADS, HEAD_DIM)
    att = swa_with_sinks(aq, ak, av, sinks) * attn_scale
    return jnp.concatenate([ret, att], axis=-1) @ w_out


def moe_ffn(x, w_router, router_bias, w1, w3, w2, ws1, ws3, ws2):
    b, s, d = x.shape
    t = x.reshape(b * s, d)
    scores = jax.nn.sigmoid((t @ w_router).astype(jnp.float32))
    biased = scores + router_bias.astype(jnp.float32)
    grp = biased.reshape(-1, N_GROUPS, N_EXPERTS // N_GROUPS)
    grp_score = jnp.sum(lax.top_k(grp, 2)[0], axis=-1)
    _, top_grp = lax.top_k(grp_score, TOPK_GROUPS)
    grp_mask = jnp.sum(jax.nn.one_hot(top_grp, N_GROUPS, dtype=jnp.float32), axis=1) > 0
    expert_mask = jnp.repeat(grp_mask, N_EXPERTS // N_GROUPS, axis=1)
    _, idx = lax.top_k(jnp.where(expert_mask, biased, NEG_INF), TOP_K)
    w = jnp.take_along_axis(scores, idx, axis=1)
    w = w / jnp.sum(w, axis=-1, keepdims=True) * ROUTED_SCALE
    combine = jnp.sum(jax.nn.one_hot(idx, N_EXPERTS, dtype=jnp.float32) * w[..., None],
                      axis=1).astype(x.dtype)

    def expert_step(acc, params):
        e1, e3, e2, c = params
        h = jax.nn.silu(t @ e1) * (t @ e3)
        return acc + c[:, None] * (h @ e2), None

    routed, _ = lax.scan(expert_step, jnp.zeros_like(t), (w1, w3, w2, combine.T))
    shared = (jax.nn.silu(t @ ws1) * (t @ ws3)) @ ws2
    return (routed + shared).reshape(b, s, d)


def setup_inputs(seed: int = 0) -> dict:
    key = jax.random.key(seed)
    ks = jax.random.split(key, 24)
    L, D, E, F, SF = DEPTH, D_MODEL, N_EXPERTS, EXPERT_DIM, SHARED_DIM
    nrm = lambda k, shape, scale: jax.random.normal(k, shape, jnp.float32) * scale
    return {
        "x": nrm(ks[0], (BATCH, SEQ, D), 1.0),
        "p": nrm(ks[1], (L, BATCH, SEQ, PLE_DIM), 1.0),
        "w_in": nrm(ks[2], (L, D, IN_WIDTH), D ** -0.5),
        "ret_gn_gain": 1.0 + nrm(ks[3], (L, RET_WIDTH), 0.02),
        "attn_scale": 1.0 + nrm(ks[4], (L, SWA_WIDTH), 0.02),
        "sinks": nrm(ks[5], (L, SWA_Q_HEADS), 0.5),
        "w_out": nrm(ks[6], (L, MIX_WIDTH, D), MIX_WIDTH ** -0.5 * DEEPNORM_BETA),
        "ln1_g": 1.0 + nrm(ks[7], (L, D), 0.02),
        "ln1_b": nrm(ks[8], (L, D), 0.02),
        "w_router": nrm(ks[9], (L, D, E), D ** -0.5),
        "router_bias": nrm(ks[10], (L, E), 0.01),
        "w1": nrm(ks[11], (L, E, D, F), D ** -0.5),
        "w3": nrm(ks[12], (L, E, D, F), D ** -0.5),
        "w2": nrm(ks[13], (L, E, F, D), F ** -0.5 * DEEPNORM_BETA),
        "ws1": nrm(ks[14], (L, D, SF), D ** -0.5),
        "ws3": nrm(ks[15], (L, D, SF), D ** -0.5),
        "ws2": nrm(ks[16], (L, SF, D), SF ** -0.5 * DEEPNORM_BETA),
        "w_ple_gate": nrm(ks[17], (L, D, D), D ** -0.5),
        "b_ple_gate": nrm(ks[18], (L, D), 0.02),
        "w_ple_proj": nrm(ks[19], (L, PLE_DIM, D), PLE_DIM ** -0.5 * DEEPNORM_BETA),
        "ln2_g": 1.0 + nrm(ks[20], (L, D), 0.02),
        "ln2_b": nrm(ks[21], (L, D), 0.02),
    }


def reference(x, p, w_in, ret_gn_gain, attn_scale, sinks, w_out, ln1_g, ln1_b,
              w_router, router_bias, w1, w3, w2, ws1, ws3, ws2,
              w_ple_gate, b_ple_gate, w_ple_proj, ln2_g, ln2_b):
    h = x
    for i in range(DEPTH):
        mix = hybrid_mixer(h, w_in[i], ret_gn_gain[i], attn_scale[i], sinks[i], w_out[i])
        h = layer_norm(DEEPNORM_ALPHA * h + mix, ln1_g[i], ln1_b[i])
        ffn = moe_ffn(h, w_router[i], router_bias[i], w1[i], w3[i], w2[i], ws1[i], ws3[i], ws2[i])
        ple = jax.nn.sigmoid(h @ w_ple_gate[i] + b_ple_gate[i]) * (p[i] @ w_ple_proj[i])
        h = layer_norm(DEEPNORM_ALPHA * h + ffn + ple, ln2_g[i], ln2_b[i])
    return h
```

```python
import functools

import jax
import jax.numpy as jnp
from jax import lax
from jax.experimental import pallas as pl
from jax.experimental.pallas import tpu as pltpu

F32 = jnp.float32
BF16 = jnp.bfloat16

D_MODEL = 1024
HEAD_DIM = 64
RET_HEADS = 8
SWA_Q_HEADS = 8
SWA_KV_HEADS = 2
SWA_GROUP = SWA_Q_HEADS // SWA_KV_HEADS
RET_WIDTH = RET_HEADS * HEAD_DIM
SWA_WIDTH = SWA_Q_HEADS * HEAD_DIM
KV_WIDTH = SWA_KV_HEADS * HEAD_DIM
MIX_WIDTH = RET_WIDTH + SWA_WIDTH
IN_WIDTH = 4 * RET_WIDTH + SWA_WIDTH + 2 * KV_WIDTH
CHUNK = 128
WINDOW = 128
RET_THETA = 10000.0
SWA_THETA = 500000.0
SWA_ROT_DIM = HEAD_DIM // 4
N_EXPERTS = 64
TOP_K = 8
N_GROUPS = 8
GROUP_SIZE = N_EXPERTS // N_GROUPS
TOPK_GROUPS = 4
EXPERT_DIM = 256
SHARED_DIM = 256
ROUTED_SCALE = 2.5
PLE_DIM = 256
LN_EPS = 1e-5
GN_EPS = 1e-6
NEG_INF = -1e30
DEPTH = 1
DEEPNORM_ALPHA = (2.0 * DEPTH) ** 0.25

OFF_RQ = 0
OFF_RK = RET_WIDTH
OFF_RV = 2 * RET_WIDTH
OFF_RG = 3 * RET_WIDTH
OFF_AQ = 4 * RET_WIDTH
OFF_AK = OFF_AQ + SWA_WIDTH
OFF_AV = OFF_AK + KV_WIDTH

V7X_LANES = 128
V7X_VMEM_LIMIT_BYTES = 56 * 1024 * 1024

MIX_CHUNKS = 4
MIX_TILE = MIX_CHUNKS * CHUNK
ROUTE_SUB = 256
ROUTE_TILE = 512
FFN_TILE = 1024

NT_DIMS = (((1,), (1,)), ((), ()))
TN_DIMS = (((0,), (0,)), ((), ()))


def _layer_norm(y, g, b):
    mu = jnp.mean(y, axis=-1, keepdims=True)
    d = y - mu
    var = jnp.mean(d * d, axis=-1, keepdims=True)
    return d * lax.rsqrt(var + LN_EPS) * g + b


def _silu(v):
    return v * jax.nn.sigmoid(v)


def _rope_inplace(buf, col0, n_groups, cos, sin, near, shift, scale):
    for g in range(n_groups):
        cols = slice(col0 + g * V7X_LANES, col0 + (g + 1) * V7X_LANES)
        xg = buf[:, cols]
        partner = jnp.where(near, pltpu.roll(xg, V7X_LANES - shift, 1), pltpu.roll(xg, shift, 1))
        y = xg * cos + partner * sin
        if scale != 1.0:
            y = y * scale
        buf[:, cols] = y


def _mixer_kernel(x_ref, win_ref, wout_ref, cr_ref, sr_ref, ca_ref, sa_ref,
                  dec_ref, kw_ref, qw_ref, cd_ref, gn_ref, asc_ref, sink_ref,
                  g1_ref, b1_ref, o_ref, proj_s, mix_s, state_s, kprev_s, vprev_s):
    step = pl.program_id(1)

    @pl.when(step == 0)
    def _():
        state_s[...] = jnp.zeros_like(state_s)
        kprev_s[...] = jnp.zeros_like(kprev_s)
        vprev_s[...] = jnp.zeros_like(vprev_s)

    proj_s[...] = jnp.dot(x_ref[...].astype(BF16), win_ref[...], preferred_element_type=F32)

    lane = lax.broadcasted_iota(jnp.int32, (MIX_TILE, V7X_LANES), 1) % HEAD_DIM
    scale = HEAD_DIM ** -0.5
    cr, sr = cr_ref[...], sr_ref[...]
    _rope_inplace(proj_s, OFF_RQ, RET_WIDTH // V7X_LANES, cr, sr, lane < HEAD_DIM // 2, HEAD_DIM // 2, 1.0)
    _rope_inplace(proj_s, OFF_RK, RET_WIDTH // V7X_LANES, cr, sr, lane < HEAD_DIM // 2, HEAD_DIM // 2, scale)
    ca, sa = ca_ref[...], sa_ref[...]
    _rope_inplace(proj_s, OFF_AQ, SWA_WIDTH // V7X_LANES, ca, sa, lane < SWA_ROT_DIM // 2, SWA_ROT_DIM // 2, 1.0)
    _rope_inplace(proj_s, OFF_AK, KV_WIDTH // V7X_LANES, ca, sa, lane < SWA_ROT_DIM // 2, SWA_ROT_DIM // 2, 1.0)

    qi = lax.broadcasted_iota(jnp.int32, (CHUNK, 2 * CHUNK), 0)
    kj = lax.broadcasted_iota(jnp.int32, (CHUNK, 2 * CHUNK), 1)
    rel = CHUNK + qi - kj
    in_window = (rel >= 0) & (rel < WINDOW)

    def chunk_body(c, carry):
        rows = pl.ds(pl.multiple_of(c * CHUNK, CHUNK), CHUNK)

        ret = []
        for h in range(RET_HEADS):
            q = proj_s[rows, OFF_RQ + h * HEAD_DIM:OFF_RQ + (h + 1) * HEAD_DIM]
            k = proj_s[rows, OFF_RK + h * HEAD_DIM:OFF_RK + (h + 1) * HEAD_DIM]
            v = proj_s[rows, OFF_RV + h * HEAD_DIM:OFF_RV + (h + 1) * HEAD_DIM].astype(BF16)
            s = lax.dot_general(q.astype(BF16), k.astype(BF16), NT_DIMS, preferred_element_type=F32)
            s = s * dec_ref[h]
            intra = jnp.dot(s.astype(BF16), v, preferred_element_type=F32)
            st = state_s[h]
            inter = jnp.dot((q * qw_ref[h]).astype(BF16), st.astype(BF16), preferred_element_type=F32)
            kv = lax.dot_general((k * kw_ref[h]).astype(BF16), v, TN_DIMS, preferred_element_type=F32)
            state_s[h] = st * cd_ref[h] + kv
            y = intra + inter
            mu = jnp.mean(y, axis=-1, keepdims=True)
            d = y - mu
            var = jnp.mean(d * d, axis=-1, keepdims=True)
            ret.append(d * lax.rsqrt(var + GN_EPS))
        gate = proj_s[rows, OFF_RG:OFF_RG + RET_WIDTH]
        mix_s[rows, 0:RET_WIDTH] = _silu(gate) * (jnp.concatenate(ret, axis=-1) * gn_ref[...])

        first = jnp.logical_and(step == 0, c == 0)
        valid = in_window & (kj >= jnp.where(first, CHUNK, 0))
        att = []
        for g in range(SWA_KV_HEADS):
            kc = proj_s[rows, OFF_AK + g * HEAD_DIM:OFF_AK + (g + 1) * HEAD_DIM]
            vc = proj_s[rows, OFF_AV + g * HEAD_DIM:OFF_AV + (g + 1) * HEAD_DIM]
            kb = jnp.concatenate([kprev_s[g], kc], axis=0).astype(BF16)
            vb = jnp.concatenate([vprev_s[g], vc], axis=0).astype(BF16)
            for j in range(SWA_GROUP):
                h = g * SWA_GROUP + j
                q = proj_s[rows, OFF_AQ + h * HEAD_DIM:OFF_AQ + (h + 1) * HEAD_DIM]
                s = lax.dot_general(q.astype(BF16), kb, NT_DIMS, preferred_element_type=F32) * scale
                s = jnp.where(valid, s, NEG_INF)
                sink = sink_ref[h]
                m = jnp.maximum(jnp.max(s, axis=-1, keepdims=True), sink)
                p = jnp.exp(s - m)
                den = jnp.sum(p, axis=-1, keepdims=True) + jnp.exp(sink - m)
                att.append(jnp.dot(p.astype(BF16), vb, preferred_element_type=F32) / den)
            kprev_s[g] = kc
            vprev_s[g] = vc
        mix_s[rows, RET_WIDTH:MIX_WIDTH] = jnp.concatenate(att, axis=-1) * asc_ref[...]
        return carry

    lax.fori_loop(0, MIX_CHUNKS, chunk_body, 0)

    y = jnp.dot(mix_s[...].astype(BF16), wout_ref[...], preferred_element_type=F32)
    y = y + DEEPNORM_ALPHA * x_ref[...]
    o_ref[...] = _layer_norm(y, g1_ref[...], b1_ref[...])


def _rope_tables(seq, rot_dim, theta):
    half = rot_dim // 2
    inv_freq = theta ** (-jnp.arange(half, dtype=F32) / half)
    ang = jnp.arange(seq, dtype=F32)[:, None] * inv_freq[None, :]
    cos, sin = jnp.cos(ang), jnp.sin(ang)
    rest = HEAD_DIM - rot_dim
    cos_h = jnp.concatenate([cos, cos, jnp.ones((seq, rest), F32)], axis=1)
    sin_h = jnp.concatenate([-sin, sin, jnp.zeros((seq, rest), F32)], axis=1)
    reps = V7X_LANES // HEAD_DIM
    return jnp.tile(cos_h, (1, reps)), jnp.tile(sin_h, (1, reps))


def _retention_tables():
    h = RET_HEADS
    log_g = jnp.log1p(-(2.0 ** (-5.0 - jnp.arange(h, dtype=F32))))
    idx = jnp.arange(CHUNK, dtype=F32)
    diff = idx[:, None] - idx[None, :]
    intra = jnp.where(diff[None] >= 0, jnp.exp(jnp.maximum(diff, 0.0)[None] * log_g[:, None, None]), 0.0)
    k_w = jnp.exp((CHUNK - 1 - idx)[:, None] * log_g[None, :])
    q_w = jnp.exp((idx + 1)[:, None] * log_g[None, :])
    chunk_decay = jnp.exp(CHUNK * log_g)
    expand = lambda t: jnp.broadcast_to(t.T[:, :, None], (h, CHUNK, HEAD_DIM))
    return intra, expand(k_w), expand(q_w), chunk_decay


def _mixer(x, w_in, w_out, gn_gain, attn_scale, sinks, ln_g, ln_b):
    b, s, d = x.shape
    cr, sr = _rope_tables(s, HEAD_DIM, RET_THETA)
    ca, sa = _rope_tables(s, SWA_ROT_DIM, SWA_THETA)
    dec, kw, qw, cd = _retention_tables()
    const = lambda shape: pl.BlockSpec(shape, lambda bi, i: (0,) * len(shape))
    tab = pl.BlockSpec((MIX_TILE, V7X_LANES), lambda bi, i: (i, 0))
    smem = pl.BlockSpec(memory_space=pltpu.SMEM)
    return pl.pallas_call(
        _mixer_kernel,
        name="mixer",
        grid=(b, s // MIX_TILE),
        in_specs=[
            pl.BlockSpec((None, MIX_TILE, d), lambda bi, i: (bi, i, 0)),
            const((d, IN_WIDTH)), const((MIX_WIDTH, d)),
            tab, tab, tab, tab,
            const((RET_HEADS, CHUNK, CHUNK)), const((RET_HEADS, CHUNK, HEAD_DIM)),
            const((RET_HEADS, CHUNK, HEAD_DIM)), smem,
            const((1, RET_WIDTH)), const((1, SWA_WIDTH)), smem,
            const((1, d)), const((1, d)),
        ],
        out_specs=pl.BlockSpec((None, MIX_TILE, d), lambda bi, i: (bi, i, 0)),
        out_shape=jax.ShapeDtypeStruct((b, s, d), F32),
        scratch_shapes=[
            pltpu.VMEM((MIX_TILE, IN_WIDTH), F32),
            pltpu.VMEM((MIX_TILE, MIX_WIDTH), F32),
            pltpu.VMEM((RET_HEADS, HEAD_DIM, HEAD_DIM), F32),
            pltpu.VMEM((SWA_KV_HEADS, CHUNK, HEAD_DIM), F32),
            pltpu.VMEM((SWA_KV_HEADS, CHUNK, HEAD_DIM), F32),
        ],
        compiler_params=pltpu.CompilerParams(
            dimension_semantics=("parallel", "arbitrary"),
            vmem_limit_bytes=V7X_VMEM_LIMIT_BYTES),
    )(x, w_in.astype(BF16), w_out.astype(BF16), cr, sr, ca, sa, dec, kw, qw, cd,
      gn_gain.reshape(1, -1), attn_scale.reshape(1, -1), sinks, ln_g.reshape(1, -1), ln_b.reshape(1, -1))


def _router_kernel(x_ref, wr_ref, bias_ref, tri_ref, wt_ref, rk_ref, cnt_ref, wtok_ref):
    tt = x_ref.shape[0]
    x = x_ref[...]
    xh = x.astype(BF16)
    xl = (x - xh.astype(F32)).astype(BF16)
    w = wr_ref[...]
    wh = w.astype(BF16)
    wl = (w - wh.astype(F32)).astype(BF16)
    dg = functools.partial(lax.dot_general, dimension_numbers=NT_DIMS, preferred_element_type=F32)
    logits = dg(wh, xh) + (dg(wh, xl) + dg(wl, xh))
    scores = jax.nn.sigmoid(logits)
    biased = scores + bias_ref[...]

    shape3 = (N_GROUPS, GROUP_SIZE, tt)
    grp = lax.broadcasted_iota(jnp.int32, shape3, 0)
    mem = lax.broadcasted_iota(jnp.int32, shape3, 1)
    b3 = biased.reshape(shape3)

    m1 = jnp.max(b3, axis=1, keepdims=True)
    i1 = jnp.min(jnp.where(b3 == m1, mem, GROUP_SIZE), axis=1, keepdims=True)
    m2 = jnp.max(jnp.where(mem == i1, -jnp.inf, b3), axis=1, keepdims=True)
    gs = jnp.broadcast_to(m1 + m2, shape3)

    gmask = jnp.zeros(shape3, F32)
    for _ in range(TOPK_GROUPS):
        m = jnp.max(gs, axis=0, keepdims=True)
        pick = grp == jnp.min(jnp.where(gs == m, grp, N_GROUPS), axis=0, keepdims=True)
        gmask = jnp.where(pick, 1.0, gmask)
        gs = jnp.where(pick, -jnp.inf, gs)

    eid = grp * GROUP_SIZE + mem
    cand = jnp.where(gmask > 0.0, b3, NEG_INF)
    sel = jnp.zeros(shape3, F32)
    for _ in range(TOP_K):
        m = jnp.max(jnp.max(cand, axis=0, keepdims=True), axis=1, keepdims=True)
        first = jnp.where(cand == m, eid, N_EXPERTS)
        first = jnp.min(jnp.min(first, axis=0, keepdims=True), axis=1, keepdims=True)
        pick = eid == first
        sel = jnp.where(pick, 1.0, sel)
        cand = jnp.where(pick, -jnp.inf, cand)

    sel = sel.reshape(N_EXPERTS, tt)
    wsel = scores * sel
    comb = wsel / jnp.sum(wsel, axis=0, keepdims=True) * ROUTED_SCALE
    wt_ref[...] = comb
    wtok_ref[...] = comb.T

    lane = lax.broadcasted_iota(jnp.int32, (N_EXPERTS, V7X_LANES), 1)
    cnt = jnp.zeros((N_EXPERTS, V7X_LANES), F32)
    for sb in range(tt // ROUTE_SUB):
        cols = slice(sb * ROUTE_SUB, (sb + 1) * ROUTE_SUB)
        sel_sb = sel[:, cols]
        rank = jnp.dot(sel_sb.astype(BF16), tri_ref[...], preferred_element_type=F32)
        rk_ref[:, cols] = jnp.where(sel_sb > 0.0, rank, -1.0)
        cnt = jnp.where(lane == sb, jnp.sum(sel_sb, axis=1, keepdims=True), cnt)
    cnt_ref[...] = cnt


def _router(x1, w_router, router_bias):
    t, d = x1.shape
    n_steps = t // ROUTE_TILE
    tri = (jnp.arange(ROUTE_SUB)[:, None] < jnp.arange(ROUTE_SUB)[None, :]).astype(BF16)
    return pl.pallas_call(
        _router_kernel,
        name="router",
        grid=(n_steps,),
        in_specs=[
            pl.BlockSpec((ROUTE_TILE, d), lambda i: (i, 0)),
            pl.BlockSpec((N_EXPERTS, d), lambda i: (0, 0)),
            pl.BlockSpec((N_EXPERTS, 1), lambda i: (0, 0)),
            pl.BlockSpec((ROUTE_SUB, ROUTE_SUB), lambda i: (0, 0)),
        ],
        out_specs=[
            pl.BlockSpec((N_EXPERTS, ROUTE_TILE), lambda i: (0, i)),
            pl.BlockSpec((N_EXPERTS, ROUTE_TILE), lambda i: (0, i)),
            pl.BlockSpec((None, N_EXPERTS, V7X_LANES), lambda i: (i, 0, 0)),
            pl.BlockSpec((ROUTE_TILE, N_EXPERTS), lambda i: (i, 0)),
        ],
        out_shape=[
            jax.ShapeDtypeStruct((N_EXPERTS, t), F32),
            jax.ShapeDtypeStruct((N_EXPERTS, t), F32),
            jax.ShapeDtypeStruct((n_steps, N_EXPERTS, V7X_LANES), F32),
            jax.ShapeDtypeStruct((t, N_EXPERTS), F32),
        ],
        compiler_params=pltpu.CompilerParams(
            dimension_semantics=("parallel",),
            vmem_limit_bytes=V7X_VMEM_LIMIT_BYTES),
    )(x1, w_router.T, router_bias.reshape(-1, 1), tri)


def _ffn_kernel(x_ref, p_ref, wtok_ref, w1_ref, w3_ref, w2_ref, ws1_ref, ws3_ref, ws2_ref,
                wg_ref, bg_ref, wp_ref, g2_ref, b2_ref, o_ref, acc_s, xb_s):
    e = pl.program_id(1)

    @pl.when(e == 0)
    def _():
        x = x_ref[...]
        xb = x.astype(BF16)
        xb_s[...] = xb
        hs = _silu(jnp.dot(xb, ws1_ref[...], preferred_element_type=F32))
        hs = hs * jnp.dot(xb, ws3_ref[...], preferred_element_type=F32)
        shared = jnp.dot(hs.astype(BF16), ws2_ref[...], preferred_element_type=F32)
        gate = jax.nn.sigmoid(jnp.dot(xb, wg_ref[...], preferred_element_type=F32) + bg_ref[...])
        side = jnp.dot(p_ref[...].astype(BF16), wp_ref[...], preferred_element_type=F32)
        acc_s[...] = DEEPNORM_ALPHA * x + shared + gate * side

    xb = xb_s[...]
    h = _silu(jnp.dot(xb, w1_ref[...], preferred_element_type=F32))
    h = h * jnp.dot(xb, w3_ref[...], preferred_element_type=F32)
    y = jnp.dot(h.astype(BF16), w2_ref[...], preferred_element_type=F32)
    lane = lax.broadcasted_iota(jnp.int32, wtok_ref.shape, 1)
    c = jnp.sum(jnp.where(lane == e, wtok_ref[...], 0.0), axis=1, keepdims=True)
    acc_s[...] += c * y

    @pl.when(e == N_EXPERTS - 1)
    def _():
        o_ref[...] = _layer_norm(acc_s[...], g2_ref[...], b2_ref[...])


def _ffn(x1, p, wtok, w1, w3, w2, ws1, ws3, ws2, wg, bg, wp, ln_g, ln_b):
    t, d = x1.shape
    const = lambda shape: pl.BlockSpec(shape, lambda i, e: (0,) * len(shape))
    return pl.pallas_call(
        _ffn_kernel,
        name="ffn",
        grid=(t // FFN_TILE, N_EXPERTS),
        in_specs=[
            pl.BlockSpec((FFN_TILE, d), lambda i, e: (i, 0)),
            pl.BlockSpec((FFN_TILE, PLE_DIM), lambda i, e: (i, 0)),
            pl.BlockSpec((FFN_TILE, N_EXPERTS), lambda i, e: (i, 0)),
            pl.BlockSpec((None, d, EXPERT_DIM), lambda i, e: (e, 0, 0)),
            pl.BlockSpec((None, d, EXPERT_DIM), lambda i, e: (e, 0, 0)),
            pl.BlockSpec((None, EXPERT_DIM, d), lambda i, e: (e, 0, 0)),
            const((d, SHARED_DIM)), const((d, SHARED_DIM)), const((SHARED_DIM, d)),
            const((d, d)), const((1, d)), const((PLE_DIM, d)), const((1, d)), const((1, d)),
        ],
        out_specs=pl.BlockSpec((FFN_TILE, d), lambda i, e: (i, 0)),
        out_shape=jax.ShapeDtypeStruct((t, d), F32),
        scratch_shapes=[pltpu.VMEM((FFN_TILE, d), F32), pltpu.VMEM((FFN_TILE, d), BF16)],
        compiler_params=pltpu.CompilerParams(
            dimension_semantics=("parallel", "arbitrary"),
            vmem_limit_bytes=V7X_VMEM_LIMIT_BYTES),
    )(x1, p, wtok, w1.astype(BF16), w3.astype(BF16), w2.astype(BF16),
      ws1.astype(BF16), ws3.astype(BF16), ws2.astype(BF16),
      wg.astype(BF16), bg.reshape(1, -1), wp.astype(BF16), ln_g.reshape(1, -1), ln_b.reshape(1, -1))


def kernel(x, p, w_in, ret_gn_gain, attn_scale, sinks, w_out, ln1_g, ln1_b, w_router, router_bias,
           w1, w3, w2, ws1, ws3, ws2, w_ple_gate, b_ple_gate, w_ple_proj, ln2_g, ln2_b):
    b, s, d = x.shape
    h = x
    for i in range(DEPTH):
        h = _mixer(h, w_in[i], w_out[i], ret_gn_gain[i], attn_scale[i], sinks[i], ln1_g[i], ln1_b[i])
        h2 = h.reshape(b * s, d)
        _, _, _, wtok = _router(h2, w_router[i], router_bias[i])
        h2 = _ffn(h2, p[i].reshape(b * s, -1), wtok, w1[i], w3[i], w2[i], ws1[i], ws3[i], ws2[i],
                  w_ple_gate[i], b_ple_gate[i], w_ple_proj[i], ln2_g[i], ln2_b[i])
        h = h2.reshape(b, s, d)
    return h
```

```python
import functools

import jax
import jax.numpy as jnp
from jax import lax
from jax.experimental import pallas as pl
from jax.experimental.pallas import tpu as pltpu

F32 = jnp.float32
BF16 = jnp.bfloat16

D_MODEL = 1024
HEAD_DIM = 64
RET_HEADS = 8
SWA_Q_HEADS = 8
SWA_KV_HEADS = 2
SWA_GROUP = SWA_Q_HEADS // SWA_KV_HEADS
RET_WIDTH = RET_HEADS * HEAD_DIM
SWA_WIDTH = SWA_Q_HEADS * HEAD_DIM
KV_WIDTH = SWA_KV_HEADS * HEAD_DIM
MIX_WIDTH = RET_WIDTH + SWA_WIDTH
IN_WIDTH = 4 * RET_WIDTH + SWA_WIDTH + 2 * KV_WIDTH
CHUNK = 128
WINDOW = 128
RET_THETA = 10000.0
SWA_THETA = 500000.0
SWA_ROT_DIM = HEAD_DIM // 4
N_EXPERTS = 64
TOP_K = 8
N_GROUPS = 8
GROUP_SIZE = N_EXPERTS // N_GROUPS
TOPK_GROUPS = 4
EXPERT_DIM = 256
SHARED_DIM = 256
ROUTED_SCALE = 2.5
PLE_DIM = 256
LN_EPS = 1e-5
GN_EPS = 1e-6
NEG_INF = -1e30
DEPTH = 1
DEEPNORM_ALPHA = (2.0 * DEPTH) ** 0.25

OFF_RQ = 0
OFF_RK = RET_WIDTH
OFF_RV = 2 * RET_WIDTH
OFF_RG = 3 * RET_WIDTH
OFF_AQ = 4 * RET_WIDTH
OFF_AK = OFF_AQ + SWA_WIDTH
OFF_AV = OFF_AK + KV_WIDTH

V7X_LANES = 128
V7X_VMEM_LIMIT_BYTES = 56 * 1024 * 1024

MIX_CHUNKS = 4
MIX_TILE = MIX_CHUNKS * CHUNK
ROUTE_SUB = 256
ROUTE_TILE = 512
GRANULE = 16
ROW_CHUNK = 256
SUB_GRANULES = ROUTE_SUB * TOP_K // GRANULE + N_EXPERTS
SUB_ROWS = SUB_GRANULES * GRANULE
SUB_TOP_BIT = 1 << (SUB_GRANULES.bit_length() - 1)
EXPERT_TILE = 512
TILE_GRANULES = EXPERT_TILE // GRANULE
NG_BITS = 5
FAR = 1.0e6

NT_DIMS = (((1,), (1,)), ((), ()))
TN_DIMS = (((0,), (0,)), ((), ()))


def _layer_norm(y, g, b):
    mu = jnp.mean(y, axis=-1, keepdims=True)
    d = y - mu
    var = jnp.mean(d * d, axis=-1, keepdims=True)
    return d * lax.rsqrt(var + LN_EPS) * g + b


def _silu(v):
    return v * jax.nn.sigmoid(v)


def _rope_inplace(buf, col0, n_groups, cos, sin, near, shift, scale):
    for g in range(n_groups):
        cols = slice(col0 + g * V7X_LANES, col0 + (g + 1) * V7X_LANES)
        xg = buf[:, cols]
        partner = jnp.where(near, pltpu.roll(xg, V7X_LANES - shift, 1), pltpu.roll(xg, shift, 1))
        y = xg * cos + partner * sin
        if scale != 1.0:
            y = y * scale
        buf[:, cols] = y


def _mixer_kernel(x_ref, win_ref, wout_ref, cr_ref, sr_ref, ca_ref, sa_ref,
                  dec_ref, kw_ref, qw_ref, cd_ref, gn_ref, asc_ref, sink_ref,
                  g1_ref, b1_ref, o_ref, proj_s, mix_s, state_s, kprev_s, vprev_s):
    step = pl.program_id(1)

    @pl.when(step == 0)
    def _():
        state_s[...] = jnp.zeros_like(state_s)
        kprev_s[...] = jnp.zeros_like(kprev_s)
        vprev_s[...] = jnp.zeros_like(vprev_s)

    proj_s[...] = jnp.dot(x_ref[...].astype(BF16), win_ref[...], preferred_element_type=F32)

    lane = lax.broadcasted_iota(jnp.int32, (MIX_TILE, V7X_LANES), 1) % HEAD_DIM
    scale = HEAD_DIM ** -0.5
    cr, sr = cr_ref[...], sr_ref[...]
    _rope_inplace(proj_s, OFF_RQ, RET_WIDTH // V7X_LANES, cr, sr, lane < HEAD_DIM // 2, HEAD_DIM // 2, 1.0)
    _rope_inplace(proj_s, OFF_RK, RET_WIDTH // V7X_LANES, cr, sr, lane < HEAD_DIM // 2, HEAD_DIM // 2, scale)
    ca, sa = ca_ref[...], sa_ref[...]
    _rope_inplace(proj_s, OFF_AQ, SWA_WIDTH // V7X_LANES, ca, sa, lane < SWA_ROT_DIM // 2, SWA_ROT_DIM // 2, 1.0)
    _rope_inplace(proj_s, OFF_AK, KV_WIDTH // V7X_LANES, ca, sa, lane < SWA_ROT_DIM // 2, SWA_ROT_DIM // 2, 1.0)

    qi = lax.broadcasted_iota(jnp.int32, (CHUNK, 2 * CHUNK), 0)
    kj = lax.broadcasted_iota(jnp.int32, (CHUNK, 2 * CHUNK), 1)
    rel = CHUNK + qi - kj
    in_window = (rel >= 0) & (rel < WINDOW)

    def chunk_body(c, carry):
        rows = pl.ds(pl.multiple_of(c * CHUNK, CHUNK), CHUNK)

        ret = []
        for h in range(RET_HEADS):
            q = proj_s[rows, OFF_RQ + h * HEAD_DIM:OFF_RQ + (h + 1) * HEAD_DIM]
            k = proj_s[rows, OFF_RK + h * HEAD_DIM:OFF_RK + (h + 1) * HEAD_DIM]
            v = proj_s[rows, OFF_RV + h * HEAD_DIM:OFF_RV + (h + 1) * HEAD_DIM].astype(BF16)
            s = lax.dot_general(q.astype(BF16), k.astype(BF16), NT_DIMS, preferred_element_type=F32)
            s = s * dec_ref[h]
            intra = jnp.dot(s.astype(BF16), v, preferred_element_type=F32)
            st = state_s[h]
            inter = jnp.dot((q * qw_ref[h]).astype(BF16), st.astype(BF16), preferred_element_type=F32)
            kv = lax.dot_general((k * kw_ref[h]).astype(BF16), v, TN_DIMS, preferred_element_type=F32)
            state_s[h] = st * cd_ref[h] + kv
            y = intra + inter
            mu = jnp.mean(y, axis=-1, keepdims=True)
            d = y - mu
            var = jnp.mean(d * d, axis=-1, keepdims=True)
            ret.append(d * lax.rsqrt(var + GN_EPS))
        gate = proj_s[rows, OFF_RG:OFF_RG + RET_WIDTH]
        mix_s[rows, 0:RET_WIDTH] = _silu(gate) * (jnp.concatenate(ret, axis=-1) * gn_ref[...])

        first = jnp.logical_and(step == 0, c == 0)
        valid = in_window & (kj >= jnp.where(first, CHUNK, 0))
        att = []
        for g in range(SWA_KV_HEADS):
            kc = proj_s[rows, OFF_AK + g * HEAD_DIM:OFF_AK + (g + 1) * HEAD_DIM]
            vc = proj_s[rows, OFF_AV + g * HEAD_DIM:OFF_AV + (g + 1) * HEAD_DIM]
            kb = jnp.concatenate([kprev_s[g], kc], axis=0).astype(BF16)
            vb = jnp.concatenate([vprev_s[g], vc], axis=0).astype(BF16)
            for j in range(SWA_GROUP):
                h = g * SWA_GROUP + j
                q = proj_s[rows, OFF_AQ + h * HEAD_DIM:OFF_AQ + (h + 1) * HEAD_DIM]
                s = lax.dot_general(q.astype(BF16), kb, NT_DIMS, preferred_element_type=F32) * scale
                s = jnp.where(valid, s, NEG_INF)
                sink = sink_ref[h]
                m = jnp.maximum(jnp.max(s, axis=-1, keepdims=True), sink)
                p = jnp.exp(s - m)
                den = jnp.sum(p, axis=-1, keepdims=True) + jnp.exp(sink - m)
                att.append(jnp.dot(p.astype(BF16), vb, preferred_element_type=F32) / den)
            kprev_s[g] = kc
            vprev_s[g] = vc
        mix_s[rows, RET_WIDTH:MIX_WIDTH] = jnp.concatenate(att, axis=-1) * asc_ref[...]
        return carry

    lax.fori_loop(0, MIX_CHUNKS, chunk_body, 0)

    y = jnp.dot(mix_s[...].astype(BF16), wout_ref[...], preferred_element_type=F32)
    y = y + DEEPNORM_ALPHA * x_ref[...]
    o_ref[...] = _layer_norm(y, g1_ref[...], b1_ref[...])


def _rope_tables(seq, rot_dim, theta):
    half = rot_dim // 2
    inv_freq = theta ** (-jnp.arange(half, dtype=F32) / half)
    ang = jnp.arange(seq, dtype=F32)[:, None] * inv_freq[None, :]
    cos, sin = jnp.cos(ang), jnp.sin(ang)
    rest = HEAD_DIM - rot_dim
    cos_h = jnp.concatenate([cos, cos, jnp.ones((seq, rest), F32)], axis=1)
    sin_h = jnp.concatenate([-sin, sin, jnp.zeros((seq, rest), F32)], axis=1)
    reps = V7X_LANES // HEAD_DIM
    return jnp.tile(cos_h, (1, reps)), jnp.tile(sin_h, (1, reps))


def _retention_tables():
    h = RET_HEADS
    log_g = jnp.log1p(-(2.0 ** (-5.0 - jnp.arange(h, dtype=F32))))
    idx = jnp.arange(CHUNK, dtype=F32)
    diff = idx[:, None] - idx[None, :]
    intra = jnp.where(diff[None] >= 0, jnp.exp(jnp.maximum(diff, 0.0)[None] * log_g[:, None, None]), 0.0)
    k_w = jnp.exp((CHUNK - 1 - idx)[:, None] * log_g[None, :])
    q_w = jnp.exp((idx + 1)[:, None] * log_g[None, :])
    chunk_decay = jnp.exp(CHUNK * log_g)
    expand = lambda t: jnp.broadcast_to(t.T[:, :, None], (h, CHUNK, HEAD_DIM))
    return intra, expand(k_w), expand(q_w), chunk_decay


def _mixer(x, w_in, w_out, gn_gain, attn_scale, sinks, ln_g, ln_b):
    b, s, d = x.shape
    cr, sr = _rope_tables(s, HEAD_DIM, RET_THETA)
    ca, sa = _rope_tables(s, SWA_ROT_DIM, SWA_THETA)
    dec, kw, qw, cd = _retention_tables()
    const = lambda shape: pl.BlockSpec(shape, lambda bi, i: (0,) * len(shape))
    tab = pl.BlockSpec((MIX_TILE, V7X_LANES), lambda bi, i: (i, 0))
    smem = pl.BlockSpec(memory_space=pltpu.SMEM)
    return pl.pallas_call(
        _mixer_kernel,
        name="mixer",
        grid=(b, s // MIX_TILE),
        in_specs=[
            pl.BlockSpec((None, MIX_TILE, d), lambda bi, i: (bi, i, 0)),
            const((d, IN_WIDTH)), const((MIX_WIDTH, d)),
            tab, tab, tab, tab,
            const((RET_HEADS, CHUNK, CHUNK)), const((RET_HEADS, CHUNK, HEAD_DIM)),
            const((RET_HEADS, CHUNK, HEAD_DIM)), smem,
            const((1, RET_WIDTH)), const((1, SWA_WIDTH)), smem,
            const((1, d)), const((1, d)),
        ],
        out_specs=pl.BlockSpec((None, MIX_TILE, d), lambda bi, i: (bi, i, 0)),
        out_shape=jax.ShapeDtypeStruct((b, s, d), F32),
        scratch_shapes=[
            pltpu.VMEM((MIX_TILE, IN_WIDTH), F32),
            pltpu.VMEM((MIX_TILE, MIX_WIDTH), F32),
            pltpu.VMEM((RET_HEADS, HEAD_DIM, HEAD_DIM), F32),
            pltpu.VMEM((SWA_KV_HEADS, CHUNK, HEAD_DIM), F32),
            pltpu.VMEM((SWA_KV_HEADS, CHUNK, HEAD_DIM), F32),
        ],
        compiler_params=pltpu.CompilerParams(
            dimension_semantics=("parallel", "arbitrary"),
            vmem_limit_bytes=V7X_VMEM_LIMIT_BYTES),
    )(x, w_in.astype(BF16), w_out.astype(BF16), cr, sr, ca, sa, dec, kw, qw, cd,
      gn_gain.reshape(1, -1), attn_scale.reshape(1, -1), sinks, ln_g.reshape(1, -1), ln_b.reshape(1, -1))


def _router_kernel(x_ref, wr_ref, bias_ref, tri_ref, wt_ref, rk_ref, cnt_ref):
    tt = x_ref.shape[0]
    x = x_ref[...]
    xh = x.astype(BF16)
    xl = (x - xh.astype(F32)).astype(BF16)
    w = wr_ref[...]
    wh = w.astype(BF16)
    wl = (w - wh.astype(F32)).astype(BF16)
    dg = functools.partial(lax.dot_general, dimension_numbers=NT_DIMS, preferred_element_type=F32)
    logits = dg(wh, xh) + (dg(wh, xl) + dg(wl, xh))
    scores = jax.nn.sigmoid(logits)
    biased = scores + bias_ref[...]

    shape3 = (N_GROUPS, GROUP_SIZE, tt)
    grp = lax.broadcasted_iota(jnp.int32, shape3, 0)
    mem = lax.broadcasted_iota(jnp.int32, shape3, 1)
    b3 = biased.reshape(shape3)

    m1 = jnp.max(b3, axis=1, keepdims=True)
    i1 = jnp.min(jnp.where(b3 == m1, mem, GROUP_SIZE), axis=1, keepdims=True)
    m2 = jnp.max(jnp.where(mem == i1, -jnp.inf, b3), axis=1, keepdims=True)
    gs = jnp.broadcast_to(m1 + m2, shape3)

    gmask = jnp.zeros(shape3, F32)
    for _ in range(TOPK_GROUPS):
        m = jnp.max(gs, axis=0, keepdims=True)
        pick = grp == jnp.min(jnp.where(gs == m, grp, N_GROUPS), axis=0, keepdims=True)
        gmask = jnp.where(pick, 1.0, gmask)
        gs = jnp.where(pick, -jnp.inf, gs)

    eid = grp * GROUP_SIZE + mem
    cand = jnp.where(gmask > 0.0, b3, NEG_INF)
    sel = jnp.zeros(shape3, F32)
    for _ in range(TOP_K):
        m = jnp.max(jnp.max(cand, axis=0, keepdims=True), axis=1, keepdims=True)
        first = jnp.where(cand == m, eid, N_EXPERTS)
        first = jnp.min(jnp.min(first, axis=0, keepdims=True), axis=1, keepdims=True)
        pick = eid == first
        sel = jnp.where(pick, 1.0, sel)
        cand = jnp.where(pick, -jnp.inf, cand)

    sel = sel.reshape(N_EXPERTS, tt)
    wsel = scores * sel
    comb = wsel / jnp.sum(wsel, axis=0, keepdims=True) * ROUTED_SCALE
    wt_ref[...] = comb

    lane = lax.broadcasted_iota(jnp.int32, (N_EXPERTS, V7X_LANES), 1)
    cnt = jnp.zeros((N_EXPERTS, V7X_LANES), F32)
    for sb in range(tt // ROUTE_SUB):
        cols = slice(sb * ROUTE_SUB, (sb + 1) * ROUTE_SUB)
        sel_sb = sel[:, cols]
        rank = jnp.dot(sel_sb.astype(BF16), tri_ref[...], preferred_element_type=F32)
        rk_ref[:, cols] = jnp.where(sel_sb > 0.0, rank, -1.0)
        cnt = jnp.where(lane == sb, jnp.sum(sel_sb, axis=1, keepdims=True), cnt)
    cnt_ref[...] = cnt


def _router(x1, w_router, router_bias):
    t, d = x1.shape
    n_steps = t // ROUTE_TILE
    tri = (jnp.arange(ROUTE_SUB)[:, None] < jnp.arange(ROUTE_SUB)[None, :]).astype(BF16)
    return pl.pallas_call(
        _router_kernel,
        name="router",
        grid=(n_steps,),
        in_specs=[
            pl.BlockSpec((ROUTE_TILE, d), lambda i: (i, 0)),
            pl.BlockSpec((N_EXPERTS, d), lambda i: (0, 0)),
            pl.BlockSpec((N_EXPERTS, 1), lambda i: (0, 0)),
            pl.BlockSpec((ROUTE_SUB, ROUTE_SUB), lambda i: (0, 0)),
        ],
        out_specs=[
            pl.BlockSpec((N_EXPERTS, ROUTE_TILE), lambda i: (0, i)),
            pl.BlockSpec((N_EXPERTS, ROUTE_TILE), lambda i: (0, i)),
            pl.BlockSpec((None, N_EXPERTS, V7X_LANES), lambda i: (i, 0, 0)),
        ],
        out_shape=[
            jax.ShapeDtypeStruct((N_EXPERTS, t), F32),
            jax.ShapeDtypeStruct((N_EXPERTS, t), F32),
            jax.ShapeDtypeStruct((n_steps, N_EXPERTS, V7X_LANES), F32),
        ],
        compiler_params=pltpu.CompilerParams(
            dimension_semantics=("parallel",),
            vmem_limit_bytes=V7X_VMEM_LIMIT_BYTES),
    )(x1, w_router.T, router_bias.reshape(-1, 1), tri)


def _run_copy(src, dst, src_g, dst_g, n_g, sem):
    rows = n_g * GRANULE
    return pltpu.make_async_copy(
        src.at[pl.ds(pl.multiple_of(src_g * GRANULE, GRANULE), rows)],
        dst.at[pl.ds(pl.multiple_of(dst_g * GRANULE, GRANULE), rows)], sem)


def _for_each_piece(n_g, max_bit, fn):
    b = max_bit
    while b >= 1:
        pl.when((n_g & b) != 0)(functools.partial(fn, n_g & ~(2 * b - 1), b))
        b //= 2


def _wait_granules(src, dst, n_g, max_bit, sem):
    _for_each_piece(n_g, max_bit, lambda off, size: _run_copy(src, dst, 0, 0, size, sem).wait())


def _sorted_positions(rk_ref, goff_ref):
    rk = rk_ref[...]
    pos = jnp.where(rk >= 0.0, rk + GRANULE * goff_ref[...], FAR)
    slots, owners = [], []
    for _ in range(TOP_K):
        m = jnp.min(pos, axis=0, keepdims=True)
        own = pos == m
        slots.append(m)
        owners.append(own)
        pos = jnp.where(own, FAR, pos)
    return slots, owners


def _dispatch_kernel(run_ref, dst_ref, tot_ref, tail_ref, x_ref, rk_ref, goff_ref, xs_hbm,
                     s_buf, zero_buf, sems, tail_sem):
    j = pl.program_id(0)
    last = pl.num_programs(0) - 1
    slot = j % 2
    stage = s_buf.at[slot]

    @pl.when(j >= 2)
    def _():
        _wait_granules(stage, xs_hbm, tot_ref[j - 2], SUB_TOP_BIT, sems.at[slot])

    slots, _ = _sorted_positions(rk_ref, goff_ref)
    xb = x_ref[...].astype(BF16)
    row_id = lax.broadcasted_iota(jnp.int32, (ROW_CHUNK, ROUTE_SUB), 0).astype(F32)

    def chunk(c, carry):
        r0 = pl.multiple_of(c * ROW_CHUNK, ROW_CHUNK)
        rid = row_id + r0.astype(F32)
        onehot = jnp.zeros((ROW_CHUNK, ROUTE_SUB), F32)
        for m in slots:
            onehot = jnp.where(rid == m, 1.0, onehot)
        rows = jnp.dot(onehot.astype(BF16), xb, preferred_element_type=F32)
        stage[pl.ds(r0, ROW_CHUNK), :] = rows.astype(BF16)
        return carry

    n_chunks = (tot_ref[j] * GRANULE + ROW_CHUNK - 1) // ROW_CHUNK
    lax.fori_loop(0, n_chunks, chunk, 0)

    def run(e, carry):
        word = run_ref[j * N_EXPERTS + e]
        n_g = word & (2 ** NG_BITS - 1)
        src_g = word >> NG_BITS
        dst_g = dst_ref[j * N_EXPERTS + e]
        _for_each_piece(n_g, 2 ** (NG_BITS - 1), lambda off, size: _run_copy(
            stage, xs_hbm, src_g + off, dst_g + off, size, sems.at[slot]).start())
        return carry

    lax.fori_loop(0, N_EXPERTS, run, 0)

    @pl.when(j == last)
    def _():
        zero_buf[...] = jnp.zeros_like(zero_buf)

        def tail(e, total):
            n_g = tail_ref[N_EXPERTS + e]
            _for_each_piece(n_g, TILE_GRANULES // 2, lambda off, size: _run_copy(
                zero_buf, xs_hbm, 0, tail_ref[e] + off, size, tail_sem).start())
            return total + n_g

        total = lax.fori_loop(0, N_EXPERTS, tail, 0)
        lax.fori_loop(0, total // TILE_GRANULES, lambda i, c: (
            _run_copy(zero_buf, xs_hbm, 0, 0, TILE_GRANULES, tail_sem).wait(), c)[1], 0)
        _wait_granules(zero_buf, xs_hbm, total % TILE_GRANULES, TILE_GRANULES // 2, tail_sem)

        @pl.when(j >= 1)
        def _():
            other = s_buf.at[1 - slot]
            _wait_granules(other, xs_hbm, tot_ref[j - 1], SUB_TOP_BIT, sems.at[1 - slot])

        _wait_granules(stage, xs_hbm, tot_ref[j], SUB_TOP_BIT, sems.at[slot])


def _dispatch(x1, rk, goff, run_tab, dst_tab, tot_tab, tail_tab, n_rows):
    t, d = x1.shape
    n_sub = t // ROUTE_SUB
    return pl.pallas_call(
        _dispatch_kernel,
        name="dispatch",
        grid_spec=pltpu.PrefetchScalarGridSpec(
            num_scalar_prefetch=4,
            grid=(n_sub,),
            in_specs=[
                pl.BlockSpec((ROUTE_SUB, d), lambda j, *_: (j, 0)),
                pl.BlockSpec((N_EXPERTS, ROUTE_SUB), lambda j, *_: (0, j)),
                pl.BlockSpec((None, N_EXPERTS, 1), lambda j, *_: (j, 0, 0)),
            ],
            out_specs=pl.BlockSpec(memory_space=pl.ANY),
            scratch_shapes=[
                pltpu.VMEM((2, SUB_ROWS, d), BF16),
                pltpu.VMEM((EXPERT_TILE, d), BF16),
                pltpu.SemaphoreType.DMA((2,)),
                pltpu.SemaphoreType.DMA,
            ],
        ),
        out_shape=jax.ShapeDtypeStruct((n_rows, d), BF16),
        compiler_params=pltpu.CompilerParams(
            dimension_semantics=("arbitrary",),
            vmem_limit_bytes=V7X_VMEM_LIMIT_BYTES),
    )(run_tab, dst_tab, tot_tab, tail_tab, x1, rk, goff)


def _expert_kernel(te_ref, used_ref, xs_ref, w1_ref, w3_ref, w2_ref, ys_ref, w1_s, w3_s, w2_s):
    t = pl.program_id(0)

    @pl.when(t < used_ref[0])
    def _():
        @pl.when(jnp.logical_or(t == 0, te_ref[t] != te_ref[jnp.maximum(t - 1, 0)]))
        def _():
            w1_s[...] = w1_ref[...].astype(BF16)
            w3_s[...] = w3_ref[...].astype(BF16)
            w2_s[...] = w2_ref[...].astype(BF16)

        x = xs_ref[...]
        h = _silu(jnp.dot(x, w1_s[...], preferred_element_type=F32))
        h = h * jnp.dot(x, w3_s[...], preferred_element_type=F32)
        ys_ref[...] = jnp.dot(h.astype(BF16), w2_s[...], preferred_element_type=F32).astype(BF16)


def _experts(xs, w1, w3, w2, tile_expert, n_used):
    n_rows, d = xs.shape
    n_tiles = n_rows // EXPERT_TILE
    row_map = lambda t, te, used: (jnp.minimum(t, used[0] - 1), 0)
    w_map = lambda t, te, used: (te[t], 0, 0)
    return pl.pallas_call(
        _expert_kernel,
        name="experts",
        grid_spec=pltpu.PrefetchScalarGridSpec(
            num_scalar_prefetch=2,
            grid=(n_tiles,),
            in_specs=[
                pl.BlockSpec((EXPERT_TILE, d), row_map),
                pl.BlockSpec((None, d, EXPERT_DIM), w_map),
                pl.BlockSpec((None, d, EXPERT_DIM), w_map),
                pl.BlockSpec((None, EXPERT_DIM, d), w_map),
            ],
            out_specs=pl.BlockSpec((EXPERT_TILE, d), row_map),
            scratch_shapes=[
                pltpu.VMEM((d, EXPERT_DIM), BF16),
                pltpu.VMEM((d, EXPERT_DIM), BF16),
                pltpu.VMEM((EXPERT_DIM, d), BF16),
            ],
        ),
        out_shape=jax.ShapeDtypeStruct((n_rows, d), BF16),
        compiler_params=pltpu.CompilerParams(
            dimension_semantics=("arbitrary",),
            vmem_limit_bytes=V7X_VMEM_LIMIT_BYTES),
    )(tile_expert, n_used, xs, w1, w3, w2)


def _combine_kernel(run_ref, dst_ref, tot_ref, x_ref, p_ref, rk_ref, wt_ref, goff_ref, ys_hbm,
                    ws1_ref, ws3_ref, ws2_ref, wg_ref, bg_ref, wp_ref, g2_ref, b2_ref,
                    o_ref, s_buf, acc_s, sems):
    j = pl.program_id(0)
    last = pl.num_programs(0) - 1
    slot = j % 2

    def gather(jj, sl):
        def run(e, carry):
            word = run_ref[jj * N_EXPERTS + e]
            n_g = word & (2 ** NG_BITS - 1)
            dst_g = word >> NG_BITS
            src_g = dst_ref[jj * N_EXPERTS + e]
            _for_each_piece(n_g, 2 ** (NG_BITS - 1), lambda off, size: _run_copy(
                ys_hbm, s_buf.at[sl], src_g + off, dst_g + off, size, sems.at[sl]).start())
            return carry

        lax.fori_loop(0, N_EXPERTS, run, 0)

    @pl.when(j == 0)
    def _():
        gather(j, slot)

    @pl.when(j < last)
    def _():
        gather(j + 1, 1 - slot)

    x = x_ref[...]
    xb = x.astype(BF16)
    hs = _silu(jnp.dot(xb, ws1_ref[...], preferred_element_type=F32))
    hs = hs * jnp.dot(xb, ws3_ref[...], preferred_element_type=F32)
    shared = jnp.dot(hs.astype(BF16), ws2_ref[...], preferred_element_type=F32)
    gate = jax.nn.sigmoid(jnp.dot(xb, wg_ref[...], preferred_element_type=F32) + bg_ref[...])
    side = jnp.dot(p_ref[...].astype(BF16), wp_ref[...], preferred_element_type=F32)
    acc_s[...] = DEEPNORM_ALPHA * x + shared + gate * side

    slots, owners = _sorted_positions(rk_ref, goff_ref)
    wt = wt_ref[...]
    weights = [jnp.sum(jnp.where(own, wt, 0.0), axis=0, keepdims=True) for own in owners]

    stage = s_buf.at[slot]
    tot = tot_ref[j]
    _wait_granules(ys_hbm, stage, tot, SUB_TOP_BIT, sems.at[slot])

    n_chunks = (tot * GRANULE + ROW_CHUNK - 1) // ROW_CHUNK
    pad_g = n_chunks * (ROW_CHUNK // GRANULE) - tot

    def zero(g, carry):
        r0 = pl.multiple_of((tot + g) * GRANULE, GRANULE)
        stage[pl.ds(r0, GRANULE), :] = jnp.zeros((GRANULE, stage.shape[1]), BF16)
        return carry

    lax.fori_loop(0, pad_g, zero, 0)

    row_id = lax.broadcasted_iota(jnp.int32, (ROW_CHUNK, ROUTE_SUB), 0).astype(F32)

    def chunk(c, carry):
        r0 = pl.multiple_of(c * ROW_CHUNK, ROW_CHUNK)
        rid = row_id + r0.astype(F32)
        scat = jnp.zeros((ROW_CHUNK, ROUTE_SUB), F32)
        for m, w in zip(slots, weights):
            scat = jnp.where(rid == m, w, scat)
        acc_s[...] += lax.dot_general(scat.astype(BF16), stage[pl.ds(r0, ROW_CHUNK), :], TN_DIMS,
                                      preferred_element_type=F32)
        return carry

    lax.fori_loop(0, n_chunks, chunk, 0)
    o_ref[...] = _layer_norm(acc_s[...], g2_ref[...], b2_ref[...])


def _combine(x1, p, rk, wt, goff, ys, run_tab, dst_tab, tot_tab, ws1, ws3, ws2, wg, bg, wp, ln_g, ln_b):
    t, d = x1.shape
    n_sub = t // ROUTE_SUB
    const = lambda shape: pl.BlockSpec(shape, lambda j, *_: (0,) * len(shape))
    return pl.pallas_call(
        _combine_kernel,
        name="combine",
        grid_spec=pltpu.PrefetchScalarGridSpec(
            num_scalar_prefetch=3,
            grid=(n_sub,),
            in_specs=[
                pl.BlockSpec((ROUTE_SUB, d), lambda j, *_: (j, 0)),
                pl.BlockSpec((ROUTE_SUB, PLE_DIM), lambda j, *_: (j, 0)),
                pl.BlockSpec((N_EXPERTS, ROUTE_SUB), lambda j, *_: (0, j)),
                pl.BlockSpec((N_EXPERTS, ROUTE_SUB), lambda j, *_: (0, j)),
                pl.BlockSpec((None, N_EXPERTS, 1), lambda j, *_: (j, 0, 0)),
                pl.BlockSpec(memory_space=pl.ANY),
                const((d, SHARED_DIM)), const((d, SHARED_DIM)), const((SHARED_DIM, d)),
                const((d, d)), const((1, d)), const((PLE_DIM, d)), const((1, d)), const((1, d)),
            ],
            out_specs=pl.BlockSpec((ROUTE_SUB, d), lambda j, *_: (j, 0)),
            scratch_shapes=[
                pltpu.VMEM((2, SUB_ROWS, d), BF16),
                pltpu.VMEM((ROUTE_SUB, d), F32),
                pltpu.SemaphoreType.DMA((2,)),
            ],
        ),
        out_shape=jax.ShapeDtypeStruct((t, d), F32),
        compiler_params=pltpu.CompilerParams(
            dimension_semantics=("arbitrary",),
            vmem_limit_bytes=V7X_VMEM_LIMIT_BYTES),
    )(run_tab, dst_tab, tot_tab, x1, p, rk, wt, goff, ys,
      ws1.astype(BF16), ws3.astype(BF16), ws2.astype(BF16),
      wg.astype(BF16), bg.reshape(1, -1), wp.astype(BF16), ln_g.reshape(1, -1), ln_b.reshape(1, -1))


def _routing_tables(cnt, n_tiles):
    ng = (cnt + GRANULE - 1) // GRANULE
    goff = jnp.cumsum(ng, axis=1) - ng
    tot = jnp.sum(ng, axis=1)
    per_expert = jnp.sum(ng, axis=0)
    padded = (per_expert + TILE_GRANULES - 1) // TILE_GRANULES * TILE_GRANULES
    seg_end = jnp.cumsum(padded)
    base = seg_end - padded
    dst = base[None, :] + jnp.cumsum(ng, axis=0) - ng
    run = goff * (2 ** NG_BITS) + ng
    tail = jnp.concatenate([base + per_expert, padded - per_expert])
    n_used = seg_end[-1:] // TILE_GRANULES
    tile_end = seg_end // TILE_GRANULES
    tile_expert = jnp.sum(tile_end[None, :] <= jnp.arange(n_tiles)[:, None], axis=1)
    tile_expert = jnp.minimum(tile_expert, N_EXPERTS - 1)
    i32 = lambda a: a.astype(jnp.int32)
    return (goff.astype(F32)[:, :, None], i32(run.reshape(-1)), i32(dst.reshape(-1)), i32(tot), i32(tail),
            i32(tile_expert), i32(n_used))


def _moe_block(x1, p, w_router, router_bias, w1, w3, w2, ws1, ws3, ws2, wg, bg, wp, ln_g, ln_b):
    t, d = x1.shape
    n_sub = t // ROUTE_SUB
    wt, rk, cnt = _router(x1, w_router, router_bias)
    per_step = ROUTE_TILE // ROUTE_SUB
    cnt = jnp.swapaxes(cnt[:, :, :per_step], 1, 2).reshape(n_sub, N_EXPERTS).astype(jnp.int32)
    max_granules = t * TOP_K // GRANULE + n_sub * N_EXPERTS + N_EXPERTS * (TILE_GRANULES - 1)
    n_tiles = (max_granules + TILE_GRANULES - 1) // TILE_GRANULES
    goff, run_tab, dst_tab, tot_tab, tail_tab, tile_expert, n_used = _routing_tables(cnt, n_tiles)
    xs = _dispatch(x1, rk, goff, run_tab, dst_tab, tot_tab, tail_tab, n_tiles * EXPERT_TILE)
    ys = _experts(xs, w1, w3, w2, tile_expert, n_used)
    return _combine(x1, p, rk, wt, goff, ys, run_tab, dst_tab, tot_tab, ws1, ws3, ws2, wg, bg, wp, ln_g, ln_b)


def kernel(x, p, w_in, ret_gn_gain, attn_scale, sinks, w_out, ln1_g, ln1_b, w_router, router_bias,
           w1, w3, w2, ws1, ws3, ws2, w_ple_gate, b_ple_gate, w_ple_proj, ln2_g, ln2_b):
    b, s, d = x.shape
    h = x
    for i in range(DEPTH):
        h = _mixer(h, w_in[i], w_out[i], ret_gn_gain[i], attn_scale[i], sinks[i], ln1_g[i], ln1_b[i])
        h2 = _moe_block(h.reshape(b * s, d), p[i].reshape(b * s, -1), w_router[i], router_bias[i],
                        w1[i], w3[i], w2[i], ws1[i], ws3[i], ws2[i],
                        w_ple_gate[i], b_ple_gate[i], w_ple_proj[i], ln2_g[i], ln2_b[i])
        h = h2.reshape(b, s, d)
    return h
```

```python
import functools

import jax
import jax.numpy as jnp
from jax import lax
from jax.experimental import pallas as pl
from jax.experimental.pallas import tpu as pltpu

F32 = jnp.float32
BF16 = jnp.bfloat16

D_MODEL = 1024
HEAD_DIM = 64
RET_HEADS = 8
SWA_Q_HEADS = 8
SWA_KV_HEADS = 2
SWA_GROUP = SWA_Q_HEADS // SWA_KV_HEADS
RET_WIDTH = RET_HEADS * HEAD_DIM
SWA_WIDTH = SWA_Q_HEADS * HEAD_DIM
KV_WIDTH = SWA_KV_HEADS * HEAD_DIM
MIX_WIDTH = RET_WIDTH + SWA_WIDTH
IN_WIDTH = 4 * RET_WIDTH + SWA_WIDTH + 2 * KV_WIDTH
CHUNK = 128
WINDOW = 128
RET_THETA = 10000.0
SWA_THETA = 500000.0
SWA_ROT_DIM = HEAD_DIM // 4
N_EXPERTS = 64
TOP_K = 8
N_GROUPS = 8
GROUP_SIZE = N_EXPERTS // N_GROUPS
TOPK_GROUPS = 4
EXPERT_DIM = 256
SHARED_DIM = 256
ROUTED_SCALE = 2.5
PLE_DIM = 256
LN_EPS = 1e-5
GN_EPS = 1e-6
NEG_INF = -1e30
DEPTH = 1
DEEPNORM_ALPHA = (2.0 * DEPTH) ** 0.25

OFF_RQ = 0
OFF_RK = RET_WIDTH
OFF_RV = 2 * RET_WIDTH
OFF_RG = 3 * RET_WIDTH
OFF_AQ = 4 * RET_WIDTH
OFF_AK = OFF_AQ + SWA_WIDTH
OFF_AV = OFF_AK + KV_WIDTH

V7X_LANES = 128
V7X_VMEM_LIMIT_BYTES = 56 * 1024 * 1024

MIX_CHUNKS = 4
MIX_TILE = MIX_CHUNKS * CHUNK
ROUTE_SUB = 256
ROUTE_TILE = 512
GRANULE = 16
ROW_CHUNK = 256
CHUNK_GRANULES = ROW_CHUNK // GRANULE
ONEHOT_PIECE = 64
SUB_GRANULES = ROUTE_SUB * TOP_K // GRANULE + N_EXPERTS
SUB_ROWS = SUB_GRANULES * GRANULE
SUB_TOP_BIT = 1 << (SUB_GRANULES.bit_length() - 1)
EXPERT_TILE = 512
TILE_GRANULES = EXPERT_TILE // GRANULE
BLOCK_TILES = 4
EXPERT_BLOCK = BLOCK_TILES * EXPERT_TILE
BLOCK_GRANULES = EXPERT_BLOCK // GRANULE
NG_BITS = 5
RUN_LOOP_PIECE = 4
TRIP_CHUNKS = 2
TRIP_GRANULES = TRIP_CHUNKS * CHUNK_GRANULES
FAR = 1.0e6

NT_DIMS = (((1,), (1,)), ((), ()))
TN_DIMS = (((0,), (0,)), ((), ()))


def _layer_norm(y, g, b):
    mu = jnp.mean(y, axis=-1, keepdims=True)
    d = y - mu
    var = jnp.mean(d * d, axis=-1, keepdims=True)
    return d * lax.rsqrt(var + LN_EPS) * g + b


def _silu(v):
    return v * jax.nn.sigmoid(v)


def _rope_inplace(buf, col0, n_groups, cos, sin, near, shift, scale):
    for g in range(n_groups):
        cols = slice(col0 + g * V7X_LANES, col0 + (g + 1) * V7X_LANES)
        xg = buf[:, cols]
        partner = jnp.where(near, pltpu.roll(xg, V7X_LANES - shift, 1), pltpu.roll(xg, shift, 1))
        y = xg * cos + partner * sin
        if scale != 1.0:
            y = y * scale
        buf[:, cols] = y


def _mixer_kernel(x_ref, win_ref, wout_ref, cr_ref, sr_ref, ca_ref, sa_ref,
                  dec_ref, kw_ref, qw_ref, cd_ref, gn_ref, asc_ref, sink_ref,
                  g1_ref, b1_ref, o_ref, proj_s, mix_s, state_s, kprev_s, vprev_s):
    step = pl.program_id(1)

    @pl.when(step == 0)
    def _():
        state_s[...] = jnp.zeros_like(state_s)
        kprev_s[...] = jnp.zeros_like(kprev_s)
        vprev_s[...] = jnp.zeros_like(vprev_s)

    proj_s[...] = jnp.dot(x_ref[...].astype(BF16), win_ref[...], preferred_element_type=F32)

    lane = lax.broadcasted_iota(jnp.int32, (MIX_TILE, V7X_LANES), 1) % HEAD_DIM
    scale = HEAD_DIM ** -0.5
    cr, sr = cr_ref[...], sr_ref[...]
    _rope_inplace(proj_s, OFF_RQ, RET_WIDTH // V7X_LANES, cr, sr, lane < HEAD_DIM // 2, HEAD_DIM // 2, 1.0)
    _rope_inplace(proj_s, OFF_RK, RET_WIDTH // V7X_LANES, cr, sr, lane < HEAD_DIM // 2, HEAD_DIM // 2, scale)
    ca, sa = ca_ref[...], sa_ref[...]
    _rope_inplace(proj_s, OFF_AQ, SWA_WIDTH // V7X_LANES, ca, sa, lane < SWA_ROT_DIM // 2, SWA_ROT_DIM // 2, 1.0)
    _rope_inplace(proj_s, OFF_AK, KV_WIDTH // V7X_LANES, ca, sa, lane < SWA_ROT_DIM // 2, SWA_ROT_DIM // 2, 1.0)

    qi = lax.broadcasted_iota(jnp.int32, (CHUNK, 2 * CHUNK), 0)
    kj = lax.broadcasted_iota(jnp.int32, (CHUNK, 2 * CHUNK), 1)
    rel = CHUNK + qi - kj
    in_window = (rel >= 0) & (rel < WINDOW)

    def chunk_body(c, carry):
        rows = pl.ds(pl.multiple_of(c * CHUNK, CHUNK), CHUNK)

        ret = []
        for h in range(RET_HEADS):
            q = proj_s[rows, OFF_RQ + h * HEAD_DIM:OFF_RQ + (h + 1) * HEAD_DIM]
            k = proj_s[rows, OFF_RK + h * HEAD_DIM:OFF_RK + (h + 1) * HEAD_DIM]
            v = proj_s[rows, OFF_RV + h * HEAD_DIM:OFF_RV + (h + 1) * HEAD_DIM].astype(BF16)
            s = lax.dot_general(q.astype(BF16), k.astype(BF16), NT_DIMS, preferred_element_type=F32)
            s = s * dec_ref[h]
            intra = jnp.dot(s.astype(BF16), v, preferred_element_type=F32)
            st = state_s[h]
            inter = jnp.dot((q * qw_ref[h]).astype(BF16), st.astype(BF16), preferred_element_type=F32)
            kv = lax.dot_general((k * kw_ref[h]).astype(BF16), v, TN_DIMS, preferred_element_type=F32)
            state_s[h] = st * cd_ref[h] + kv
            y = intra + inter
            mu = jnp.mean(y, axis=-1, keepdims=True)
            d = y - mu
            var = jnp.mean(d * d, axis=-1, keepdims=True)
            ret.append(d * lax.rsqrt(var + GN_EPS))
        gate = proj_s[rows, OFF_RG:OFF_RG + RET_WIDTH]
        mix_s[rows, 0:RET_WIDTH] = _silu(gate) * (jnp.concatenate(ret, axis=-1) * gn_ref[...])

        first = jnp.logical_and(step == 0, c == 0)
        valid = in_window & (kj >= jnp.where(first, CHUNK, 0))
        att = []
        for g in range(SWA_KV_HEADS):
            kc = proj_s[rows, OFF_AK + g * HEAD_DIM:OFF_AK + (g + 1) * HEAD_DIM]
            vc = proj_s[rows, OFF_AV + g * HEAD_DIM:OFF_AV + (g + 1) * HEAD_DIM]
            kb = jnp.concatenate([kprev_s[g], kc], axis=0).astype(BF16)
            vb = jnp.concatenate([vprev_s[g], vc], axis=0).astype(BF16)
            for j in range(SWA_GROUP):
                h = g * SWA_GROUP + j
                q = proj_s[rows, OFF_AQ + h * HEAD_DIM:OFF_AQ + (h + 1) * HEAD_DIM]
                s = lax.dot_general(q.astype(BF16), kb, NT_DIMS, preferred_element_type=F32) * scale
                s = jnp.where(valid, s, NEG_INF)
                sink = sink_ref[h]
                m = jnp.maximum(jnp.max(s, axis=-1, keepdims=True), sink)
                p = jnp.exp(s - m)
                den = jnp.sum(p, axis=-1, keepdims=True) + jnp.exp(sink - m)
                att.append(jnp.dot(p.astype(BF16), vb, preferred_element_type=F32) / den)
            kprev_s[g] = kc
            vprev_s[g] = vc
        mix_s[rows, RET_WIDTH:MIX_WIDTH] = jnp.concatenate(att, axis=-1) * asc_ref[...]
        return carry

    lax.fori_loop(0, MIX_CHUNKS, chunk_body, 0)

    y = jnp.dot(mix_s[...].astype(BF16), wout_ref[...], preferred_element_type=F32)
    y = y + DEEPNORM_ALPHA * x_ref[...]
    o_ref[...] = _layer_norm(y, g1_ref[...], b1_ref[...])


def _rope_tables(seq, rot_dim, theta):
    half = rot_dim // 2
    inv_freq = theta ** (-jnp.arange(half, dtype=F32) / half)
    ang = jnp.arange(seq, dtype=F32)[:, None] * inv_freq[None, :]
    cos, sin = jnp.cos(ang), jnp.sin(ang)
    rest = HEAD_DIM - rot_dim
    cos_h = jnp.concatenate([cos, cos, jnp.ones((seq, rest), F32)], axis=1)
    sin_h = jnp.concatenate([-sin, sin, jnp.zeros((seq, rest), F32)], axis=1)
    reps = V7X_LANES // HEAD_DIM
    return jnp.tile(cos_h, (1, reps)), jnp.tile(sin_h, (1, reps))


def _retention_tables():
    h = RET_HEADS
    log_g = jnp.log1p(-(2.0 ** (-5.0 - jnp.arange(h, dtype=F32))))
    idx = jnp.arange(CHUNK, dtype=F32)
    diff = idx[:, None] - idx[None, :]
    intra = jnp.where(diff[None] >= 0, jnp.exp(jnp.maximum(diff, 0.0)[None] * log_g[:, None, None]), 0.0)
    k_w = jnp.exp((CHUNK - 1 - idx)[:, None] * log_g[None, :])
    q_w = jnp.exp((idx + 1)[:, None] * log_g[None, :])
    chunk_decay = jnp.exp(CHUNK * log_g)
    expand = lambda t: jnp.broadcast_to(t.T[:, :, None], (h, CHUNK, HEAD_DIM))
    return intra, expand(k_w), expand(q_w), chunk_decay


def _mixer(x, w_in, w_out, gn_gain, attn_scale, sinks, ln_g, ln_b):
    b, s, d = x.shape
    cr, sr = _rope_tables(s, HEAD_DIM, RET_THETA)
    ca, sa = _rope_tables(s, SWA_ROT_DIM, SWA_THETA)
    dec, kw, qw, cd = _retention_tables()
    const = lambda shape: pl.BlockSpec(shape, lambda bi, i: (0,) * len(shape))
    tab = pl.BlockSpec((MIX_TILE, V7X_LANES), lambda bi, i: (i, 0))
    smem = pl.BlockSpec(memory_space=pltpu.SMEM)
    return pl.pallas_call(
        _mixer_kernel,
        name="mixer",
        grid=(b, s // MIX_TILE),
        in_specs=[
            pl.BlockSpec((None, MIX_TILE, d), lambda bi, i: (bi, i, 0)),
            const((d, IN_WIDTH)), const((MIX_WIDTH, d)),
            tab, tab, tab, tab,
            const((RET_HEADS, CHUNK, CHUNK)), const((RET_HEADS, CHUNK, HEAD_DIM)),
            const((RET_HEADS, CHUNK, HEAD_DIM)), smem,
            const((1, RET_WIDTH)), const((1, SWA_WIDTH)), smem,
            const((1, d)), const((1, d)),
        ],
        out_specs=pl.BlockSpec((None, MIX_TILE, d), lambda bi, i: (bi, i, 0)),
        out_shape=jax.ShapeDtypeStruct((b, s, d), F32),
        scratch_shapes=[
            pltpu.VMEM((MIX_TILE, IN_WIDTH), F32),
            pltpu.VMEM((MIX_TILE, MIX_WIDTH), F32),
            pltpu.VMEM((RET_HEADS, HEAD_DIM, HEAD_DIM), F32),
            pltpu.VMEM((SWA_KV_HEADS, CHUNK, HEAD_DIM), F32),
            pltpu.VMEM((SWA_KV_HEADS, CHUNK, HEAD_DIM), F32),
        ],
        compiler_params=pltpu.CompilerParams(
            dimension_semantics=("parallel", "arbitrary"),
            vmem_limit_bytes=V7X_VMEM_LIMIT_BYTES),
    )(x, w_in.astype(BF16), w_out.astype(BF16), cr, sr, ca, sa, dec, kw, qw, cd,
      gn_gain.reshape(1, -1), attn_scale.reshape(1, -1), sinks, ln_g.reshape(1, -1), ln_b.reshape(1, -1))


def _router_kernel(x_ref, wr_ref, bias_ref, tri_ref, wt_ref, rk_ref, cnt_ref):
    tt = x_ref.shape[0]
    x = x_ref[...]
    xh = x.astype(BF16)
    xl = (x - xh.astype(F32)).astype(BF16)
    w = wr_ref[...]
    wh = w.astype(BF16)
    wl = (w - wh.astype(F32)).astype(BF16)
    dg = functools.partial(lax.dot_general, dimension_numbers=NT_DIMS, preferred_element_type=F32)
    logits = dg(wh, xh) + (dg(wh, xl) + dg(wl, xh))
    scores = jax.nn.sigmoid(logits)
    biased = scores + bias_ref[...]

    shape3 = (N_GROUPS, GROUP_SIZE, tt)
    grp = lax.broadcasted_iota(jnp.int32, shape3, 0)
    mem = lax.broadcasted_iota(jnp.int32, shape3, 1)
    b3 = biased.reshape(shape3)

    m1 = jnp.max(b3, axis=1, keepdims=True)
    i1 = jnp.min(jnp.where(b3 == m1, mem, GROUP_SIZE), axis=1, keepdims=True)
    m2 = jnp.max(jnp.where(mem == i1, -jnp.inf, b3), axis=1, keepdims=True)
    gs = jnp.broadcast_to(m1 + m2, shape3)

    gmask = jnp.zeros(shape3, F32)
    for _ in range(TOPK_GROUPS):
        m = jnp.max(gs, axis=0, keepdims=True)
        pick = grp == jnp.min(jnp.where(gs == m, grp, N_GROUPS), axis=0, keepdims=True)
        gmask = jnp.where(pick, 1.0, gmask)
        gs = jnp.where(pick, -jnp.inf, gs)

    eid = grp * GROUP_SIZE + mem
    cand = jnp.where(gmask > 0.0, b3, NEG_INF)
    sel = jnp.zeros(shape3, F32)
    for _ in range(TOP_K):
        m = jnp.max(jnp.max(cand, axis=0, keepdims=True), axis=1, keepdims=True)
        first = jnp.where(cand == m, eid, N_EXPERTS)
        first = jnp.min(jnp.min(first, axis=0, keepdims=True), axis=1, keepdims=True)
        pick = eid == first
        sel = jnp.where(pick, 1.0, sel)
        cand = jnp.where(pick, -jnp.inf, cand)

    sel = sel.reshape(N_EXPERTS, tt)
    wsel = scores * sel
    comb = wsel / jnp.sum(wsel, axis=0, keepdims=True) * ROUTED_SCALE
    wt_ref[...] = comb

    lane = lax.broadcasted_iota(jnp.int32, (N_EXPERTS, V7X_LANES), 1)
    cnt = jnp.zeros((N_EXPERTS, V7X_LANES), F32)
    for sb in range(tt // ROUTE_SUB):
        cols = slice(sb * ROUTE_SUB, (sb + 1) * ROUTE_SUB)
        sel_sb = sel[:, cols]
        rank = jnp.dot(sel_sb.astype(BF16), tri_ref[...], preferred_element_type=F32)
        rk_ref[:, cols] = jnp.where(sel_sb > 0.0, rank, -1.0)
        cnt = jnp.where(lane == sb, jnp.sum(sel_sb, axis=1, keepdims=True), cnt)
    cnt_ref[...] = cnt


def _router(x1, w_router, router_bias):
    t, d = x1.shape
    n_steps = t // ROUTE_TILE
    tri = (jnp.arange(ROUTE_SUB)[:, None] < jnp.arange(ROUTE_SUB)[None, :]).astype(BF16)
    return pl.pallas_call(
        _router_kernel,
        name="router",
        grid=(n_steps,),
        in_specs=[
            pl.BlockSpec((ROUTE_TILE, d), lambda i: (i, 0)),
            pl.BlockSpec((N_EXPERTS, d), lambda i: (0, 0)),
            pl.BlockSpec((N_EXPERTS, 1), lambda i: (0, 0)),
            pl.BlockSpec((ROUTE_SUB, ROUTE_SUB), lambda i: (0, 0)),
        ],
        out_specs=[
            pl.BlockSpec((N_EXPERTS, ROUTE_TILE), lambda i: (0, i)),
            pl.BlockSpec((N_EXPERTS, ROUTE_TILE), lambda i: (0, i)),
            pl.BlockSpec((None, N_EXPERTS, V7X_LANES), lambda i: (i, 0, 0)),
        ],
        out_shape=[
            jax.ShapeDtypeStruct((N_EXPERTS, t), F32),
            jax.ShapeDtypeStruct((N_EXPERTS, t), F32),
            jax.ShapeDtypeStruct((n_steps, N_EXPERTS, V7X_LANES), F32),
        ],
        compiler_params=pltpu.CompilerParams(
            dimension_semantics=("parallel",),
            vmem_limit_bytes=V7X_VMEM_LIMIT_BYTES),
    )(x1, w_router.T, router_bias.reshape(-1, 1), tri)


def _run_copy(src, dst, src_g, dst_g, n_g, sem):
    return pltpu.make_async_copy(src.at[pl.ds(src_g, n_g)], dst.at[pl.ds(dst_g, n_g)], sem)


def _for_each_piece(n_g, max_bit, fn):
    b = max_bit
    while b >= 1:
        pl.when((n_g & b) != 0)(functools.partial(fn, n_g & ~(2 * b - 1), b))
        b //= 2


def _for_each_run_piece(n_g, fn):
    def body(i, carry):
        fn(i * RUN_LOOP_PIECE, RUN_LOOP_PIECE)
        return carry

    lax.fori_loop(0, lax.shift_right_logical(n_g, RUN_LOOP_PIECE.bit_length() - 1), body, 0)
    _for_each_piece(n_g, RUN_LOOP_PIECE // 2, fn)


def _wait_granules(src, dst, n_g, max_bit, sem):
    _for_each_piece(n_g, max_bit, lambda off, size: _run_copy(src, dst, 0, 0, size, sem).wait())


def _sorted_positions(rk_ref, goff_ref):
    rk = rk_ref[...]
    pos = jnp.where(rk >= 0.0, rk + GRANULE * goff_ref[...], FAR)
    slots, owners = [], []
    for _ in range(TOP_K):
        m = jnp.min(pos, axis=0, keepdims=True)
        own = pos == m
        slots.append(m)
        owners.append(own)
        pos = jnp.where(own, FAR, pos)
    return slots, owners


def _n_trips(tot_g):
    pair_granules = 2 * TRIP_GRANULES
    assert SUB_GRANULES % pair_granules == 0 and pair_granules & (pair_granules - 1) == 0
    return 2 * lax.shift_right_logical(tot_g + (pair_granules - 1), pair_granules.bit_length() - 1)


def _chunk_onehot(c, slots, values):
    rid = lax.broadcasted_iota(jnp.int32, (ONEHOT_PIECE, ROUTE_SUB), 0).astype(F32).astype(BF16)
    vals = [jnp.asarray(v, F32).astype(BF16) for v in values]
    pieces = []
    for p in range(ROW_CHUNK // ONEHOT_PIECE):
        base = jnp.asarray(c * ROW_CHUNK + p * ONEHOT_PIECE, F32)
        acc = jnp.zeros((ONEHOT_PIECE, ROUTE_SUB), BF16)
        for m, v in zip(slots, vals):
            acc = jnp.where(rid == (m - base).astype(BF16), v, acc)
        pieces.append(acc)
    return jnp.concatenate(pieces, axis=0)


def _pipelined_trips(n_trips, p_buf, slots, values, consume):
    def build(i, half):
        p_buf[half] = jnp.concatenate(
            [_chunk_onehot(i * TRIP_CHUNKS + u, slots, values) for u in range(TRIP_CHUNKS)], axis=0)

    build(0, 0)

    def pair(k, carry):
        build(2 * k + 1, 1)
        consume(2 * k, p_buf[0])
        build(2 * k + 2, 0)
        consume(2 * k + 1, p_buf[1])
        return carry

    lax.fori_loop(0, lax.shift_right_logical(n_trips, 1), pair, 0)


def _serial_trips(n_trips, slots, values, consume):
    def trip(i, carry):
        consume(i, jnp.concatenate(
            [_chunk_onehot(i * TRIP_CHUNKS + u, slots, values) for u in range(TRIP_CHUNKS)], axis=0))
        return carry

    lax.fori_loop(0, n_trips, trip, 0)


def _dispatch_kernel(run_ref, dst_ref, tot_ref, tail_ref, x_ref, rk_ref, goff_ref, xs_hbm,
                     s_buf, p_buf, zero_buf, sems, tail_sem):
    j = pl.program_id(0)
    last = pl.num_programs(0) - 1
    slot = j % 2
    stage = s_buf.at[slot]

    @pl.when(j >= 2)
    def _():
        _wait_granules(stage, xs_hbm, tot_ref[j - 2], SUB_TOP_BIT, sems.at[slot])

    slots, _ = _sorted_positions(rk_ref, goff_ref)
    xb = x_ref[...].astype(BF16)

    def consume(i, onehot):
        rows = jnp.dot(onehot, xb, preferred_element_type=F32).astype(BF16)
        g0 = pl.multiple_of(i * TRIP_GRANULES, TRIP_GRANULES)
        stage[pl.ds(g0, TRIP_GRANULES)] = rows.reshape(TRIP_GRANULES, GRANULE, rows.shape[1])

    _pipelined_trips(_n_trips(tot_ref[j]), p_buf, slots, [1.0] * TOP_K, consume)

    def run(e, carry):
        word = run_ref[j * N_EXPERTS + e]
        n_g = word & (2 ** NG_BITS - 1)
        src_g = word >> NG_BITS
        dst_g = dst_ref[j * N_EXPERTS + e]
        _for_each_run_piece(n_g, lambda off, size: _run_copy(
            stage, xs_hbm, src_g + off, dst_g + off, size, sems.at[slot]).start())
        return carry

    lax.fori_loop(0, N_EXPERTS, run, 0)

    @pl.when(j == last)
    def _():
        zero_buf[...] = jnp.zeros_like(zero_buf)

        def tail(e, total):
            n_g = tail_ref[N_EXPERTS + e]
            _for_each_piece(n_g, TILE_GRANULES // 2, lambda off, size: _run_copy(
                zero_buf, xs_hbm, 0, tail_ref[e] + off, size, tail_sem).start())
            return total + n_g

        total = lax.fori_loop(0, N_EXPERTS, tail, 0)
        lax.fori_loop(0, total // TILE_GRANULES, lambda i, c: (
            _run_copy(zero_buf, xs_hbm, 0, 0, TILE_GRANULES, tail_sem).wait(), c)[1], 0)
        _wait_granules(zero_buf, xs_hbm, total % TILE_GRANULES, TILE_GRANULES // 2, tail_sem)

        @pl.when(j >= 1)
        def _():
            other = s_buf.at[1 - slot]
            _wait_granules(other, xs_hbm, tot_ref[j - 1], SUB_TOP_BIT, sems.at[1 - slot])

        _wait_granules(stage, xs_hbm, tot_ref[j], SUB_TOP_BIT, sems.at[slot])


def _dispatch(x1, rk, goff, run_tab, dst_tab, tot_tab, tail_tab, n_rows):
    t, d = x1.shape
    n_sub = t // ROUTE_SUB
    return pl.pallas_call(
        _dispatch_kernel,
        name="dispatch",
        grid_spec=pltpu.PrefetchScalarGridSpec(
            num_scalar_prefetch=4,
            grid=(n_sub,),
            in_specs=[
                pl.BlockSpec((ROUTE_SUB, d), lambda j, *_: (j, 0)),
                pl.BlockSpec((N_EXPERTS, ROUTE_SUB), lambda j, *_: (0, j)),
                pl.BlockSpec((None, N_EXPERTS, 1), lambda j, *_: (j, 0, 0)),
            ],
            out_specs=pl.BlockSpec(memory_space=pl.ANY),
            scratch_shapes=[
                pltpu.VMEM((2, SUB_GRANULES, GRANULE, d), BF16),
                pltpu.VMEM((2, TRIP_GRANULES * GRANULE, ROUTE_SUB), BF16),
                pltpu.VMEM((TILE_GRANULES, GRANULE, d), BF16),
                pltpu.SemaphoreType.DMA((2,)),
                pltpu.SemaphoreType.DMA,
            ],
        ),
        out_shape=jax.ShapeDtypeStruct((n_rows // GRANULE, GRANULE, d), BF16),
        compiler_params=pltpu.CompilerParams(
            dimension_semantics=("arbitrary",),
            vmem_limit_bytes=V7X_VMEM_LIMIT_BYTES),
    )(run_tab, dst_tab, tot_tab, tail_tab, x1, rk, goff)


def _expert_kernel(be_ref, bt_ref, used_ref, xs_ref, w1_ref, w3_ref, w2_ref, ys_ref, w1_s, w3_s, w2_s):
    t = pl.program_id(0)

    @pl.when(t < used_ref[0])
    def _():
        @pl.when(jnp.logical_or(t == 0, be_ref[t] != be_ref[jnp.maximum(t - 1, 0)]))
        def _():
            w1_s[...] = w1_ref[...].astype(BF16)
            w3_s[...] = w3_ref[...].astype(BF16)
            w2_s[...] = w2_ref[...].astype(BF16)

        for q in range(BLOCK_TILES):
            @pl.when(q < bt_ref[t])
            def _():
                rows = pl.ds(q * EXPERT_TILE, EXPERT_TILE)
                x = xs_ref[rows, :]
                h = _silu(jnp.dot(x, w1_s[...], preferred_element_type=F32))
                h = h * jnp.dot(x, w3_s[...], preferred_element_type=F32)
                ys_ref[rows, :] = jnp.dot(h.astype(BF16), w2_s[...], preferred_element_type=F32).astype(BF16)


def _experts(xs, w1, w3, w2, block_expert, block_tiles, n_used):
    n_rows, d = xs.shape
    n_blocks = n_rows // EXPERT_BLOCK
    row_map = lambda t, be, bt, used: (jnp.minimum(t, used[0] - 1), 0)
    w_map = lambda t, be, bt, used: (be[t], 0, 0)
    return pl.pallas_call(
        _expert_kernel,
        name="experts",
        grid_spec=pltpu.PrefetchScalarGridSpec(
            num_scalar_prefetch=3,
            grid=(n_blocks,),
            in_specs=[
                pl.BlockSpec((EXPERT_BLOCK, d), row_map),
                pl.BlockSpec((None, d, EXPERT_DIM), w_map),
                pl.BlockSpec((None, d, EXPERT_DIM), w_map),
                pl.BlockSpec((None, EXPERT_DIM, d), w_map),
            ],
            out_specs=pl.BlockSpec((EXPERT_BLOCK, d), row_map),
            scratch_shapes=[
                pltpu.VMEM((d, EXPERT_DIM), BF16),
                pltpu.VMEM((d, EXPERT_DIM), BF16),
                pltpu.VMEM((EXPERT_DIM, d), BF16),
            ],
        ),
        out_shape=jax.ShapeDtypeStruct((n_rows, d), BF16),
        compiler_params=pltpu.CompilerParams(
            dimension_semantics=("arbitrary",),
            vmem_limit_bytes=V7X_VMEM_LIMIT_BYTES),
    )(block_expert, block_tiles, n_used, xs, w1, w3, w2)


def _combine_kernel(run_ref, dst_ref, tot_ref, x_ref, p_ref, rk_ref, wt_ref, goff_ref, ys_hbm,
                    ws1_ref, ws3_ref, ws2_ref, wg_ref, bg_ref, wp_ref, g2_ref, b2_ref,
                    o_ref, s_buf, acc_s, sems):
    j = pl.program_id(0)
    last = pl.num_programs(0) - 1
    slot = j % 2

    def gather(jj, sl):
        def run(e, carry):
            word = run_ref[jj * N_EXPERTS + e]
            n_g = word & (2 ** NG_BITS - 1)
            dst_g = word >> NG_BITS
            src_g = dst_ref[jj * N_EXPERTS + e]
            _for_each_run_piece(n_g, lambda off, size: _run_copy(
                ys_hbm, s_buf.at[sl], src_g + off, dst_g + off, size, sems.at[sl]).start())
            return carry

        lax.fori_loop(0, N_EXPERTS, run, 0)

    @pl.when(j == 0)
    def _():
        gather(j, slot)

    @pl.when(j < last)
    def _():
        gather(j + 1, 1 - slot)

    x = x_ref[...]
    xb = x.astype(BF16)
    hs = _silu(jnp.dot(xb, ws1_ref[...], preferred_element_type=F32))
    hs = hs * jnp.dot(xb, ws3_ref[...], preferred_element_type=F32)
    shared = jnp.dot(hs.astype(BF16), ws2_ref[...], preferred_element_type=F32)
    gate = jax.nn.sigmoid(jnp.dot(xb, wg_ref[...], preferred_element_type=F32) + bg_ref[...])
    side = jnp.dot(p_ref[...].astype(BF16), wp_ref[...], preferred_element_type=F32)
    acc_s[...] = DEEPNORM_ALPHA * x + shared + gate * side

    slots, owners = _sorted_positions(rk_ref, goff_ref)
    wt = wt_ref[...]
    weights = [jnp.sum(jnp.where(own, wt, 0.0), axis=0, keepdims=True) for own in owners]

    stage = s_buf.at[slot]
    tot = tot_ref[j]
    _wait_granules(ys_hbm, stage, tot, SUB_TOP_BIT, sems.at[slot])

    n_trips = _n_trips(tot)
    pad_g = n_trips * TRIP_GRANULES - tot

    def zero(g, carry):
        stage[pl.ds(tot + g, 1)] = jnp.zeros((1,) + stage.shape[1:], BF16)
        return carry

    lax.fori_loop(0, pad_g, zero, 0)

    def consume(i, scat):
        g0 = pl.multiple_of(i * TRIP_GRANULES, TRIP_GRANULES)
        rows = stage[pl.ds(g0, TRIP_GRANULES)].reshape(TRIP_GRANULES * GRANULE, stage.shape[2])
        acc_s[...] += lax.dot_general(scat, rows, TN_DIMS, preferred_element_type=F32)

    _serial_trips(n_trips, slots, weights, consume)
    o_ref[...] = _layer_norm(acc_s[...], g2_ref[...], b2_ref[...])


def _combine(x1, p, rk, wt, goff, ys, run_tab, dst_tab, tot_tab, ws1, ws3, ws2, wg, bg, wp, ln_g, ln_b):
    t, d = x1.shape
    n_sub = t // ROUTE_SUB
    const = lambda shape: pl.BlockSpec(shape, lambda j, *_: (0,) * len(shape))
    return pl.pallas_call(
        _combine_kernel,
        name="combine",
        grid_spec=pltpu.PrefetchScalarGridSpec(
            num_scalar_prefetch=3,
            grid=(n_sub,),
            in_specs=[
                pl.BlockSpec((ROUTE_SUB, d), lambda j, *_: (j, 0)),
                pl.BlockSpec((ROUTE_SUB, PLE_DIM), lambda j, *_: (j, 0)),
                pl.BlockSpec((N_EXPERTS, ROUTE_SUB), lambda j, *_: (0, j)),
                pl.BlockSpec((N_EXPERTS, ROUTE_SUB), lambda j, *_: (0, j)),
                pl.BlockSpec((None, N_EXPERTS, 1), lambda j, *_: (j, 0, 0)),
                pl.BlockSpec(memory_space=pl.ANY),
                const((d, SHARED_DIM)), const((d, SHARED_DIM)), const((SHARED_DIM, d)),
                const((d, d)), const((1, d)), const((PLE_DIM, d)), const((1, d)), const((1, d)),
            ],
            out_specs=pl.BlockSpec((ROUTE_SUB, d), lambda j, *_: (j, 0)),
            scratch_shapes=[
                pltpu.VMEM((2, SUB_GRANULES, GRANULE, d), BF16),
                pltpu.VMEM((ROUTE_SUB, d), F32),
                pltpu.SemaphoreType.DMA((2,)),
            ],
        ),
        out_shape=jax.ShapeDtypeStruct((t, d), F32),
        compiler_params=pltpu.CompilerParams(
            dimension_semantics=("arbitrary",),
            vmem_limit_bytes=V7X_VMEM_LIMIT_BYTES),
    )(run_tab, dst_tab, tot_tab, x1, p, rk, wt, goff, ys,
      ws1.astype(BF16), ws3.astype(BF16), ws2.astype(BF16),
      wg.astype(BF16), bg.reshape(1, -1), wp.astype(BF16), ln_g.reshape(1, -1), ln_b.reshape(1, -1))


def _routing_tables(cnt, n_blocks):
    ng = (cnt + GRANULE - 1) // GRANULE
    goff = jnp.cumsum(ng, axis=1) - ng
    tot = jnp.sum(ng, axis=1)
    per_expert = jnp.sum(ng, axis=0)
    tiles = (per_expert + TILE_GRANULES - 1) // TILE_GRANULES
    padded = (per_expert + BLOCK_GRANULES - 1) // BLOCK_GRANULES * BLOCK_GRANULES
    seg_end = jnp.cumsum(padded)
    base = seg_end - padded
    dst = base[None, :] + jnp.cumsum(ng, axis=0) - ng
    run = goff * (2 ** NG_BITS) + ng
    tail = jnp.concatenate([base + per_expert, tiles * TILE_GRANULES - per_expert])
    n_used = seg_end[-1:] // BLOCK_GRANULES
    block_end = seg_end // BLOCK_GRANULES
    block_id = jnp.arange(n_blocks)
    block_expert = jnp.minimum(jnp.sum(block_end[None, :] <= block_id[:, None], axis=1), N_EXPERTS - 1)
    done_tiles = (block_id - (base // BLOCK_GRANULES)[block_expert]) * BLOCK_TILES
    block_tiles = jnp.clip(tiles[block_expert] - done_tiles, 0, BLOCK_TILES)
    i32 = lambda a: a.astype(jnp.int32)
    return (goff.astype(F32)[:, :, None], i32(run.reshape(-1)), i32(dst.reshape(-1)), i32(tot), i32(tail),
            i32(block_expert), i32(block_tiles), i32(n_used))


def _moe_block(x1, p, w_router, router_bias, w1, w3, w2, ws1, ws3, ws2, wg, bg, wp, ln_g, ln_b):
    t, d = x1.shape
    n_sub = t // ROUTE_SUB
    wt, rk, cnt = _router(x1, w_router, router_bias)
    per_step = ROUTE_TILE // ROUTE_SUB
    cnt = jnp.swapaxes(cnt[:, :, :per_step], 1, 2).reshape(n_sub, N_EXPERTS).astype(jnp.int32)
    max_granules = t * TOP_K // GRANULE + n_sub * N_EXPERTS + N_EXPERTS * (BLOCK_GRANULES - 1)
    n_blocks = (max_granules + BLOCK_GRANULES - 1) // BLOCK_GRANULES
    (goff, run_tab, dst_tab, tot_tab, tail_tab,
     block_expert, block_tiles, n_used) = _routing_tables(cnt, n_blocks)
    xs = _dispatch(x1, rk, goff, run_tab, dst_tab, tot_tab, tail_tab, n_blocks * EXPERT_BLOCK)
    ys = _experts(xs.reshape(-1, d), w1, w3, w2, block_expert, block_tiles, n_used).reshape(xs.shape)
    return _combine(x1, p, rk, wt, goff, ys, run_tab, dst_tab, tot_tab, ws1, ws3, ws2, wg, bg, wp, ln_g, ln_b)


def kernel(x, p, w_in, ret_gn_gain, attn_scale, sinks, w_out, ln1_g, ln1_b, w_router, router_bias,
           w1, w3, w2, ws1, ws3, ws2, w_ple_gate, b_ple_gate, w_ple_proj, ln2_g, ln2_b):
    b, s, d = x.shape
    h = x
    for i in range(DEPTH):
        h = _mixer(h, w_in[i], w_out[i], ret_gn_gain[i], attn_scale[i], sinks[i], ln1_g[i], ln1_b[i])
        h2 = _moe_block(h.reshape(b * s, d), p[i].reshape(b * s, -1), w_router[i], router_bias[i],
                        w1[i], w3[i], w2[i], ws1[i], ws3[i], ws2[i],
                        w_ple_gate[i], b_ple_gate[i], w_ple_proj[i], ln2_g[i], ln2_b[i])
        h = h2.reshape(b, s, d)
    return h
```

```python
import functools

import jax
import jax.numpy as jnp
from jax import lax
from jax.experimental import pallas as pl
from jax.experimental.pallas import tpu as pltpu

F32 = jnp.float32
BF16 = jnp.bfloat16

D_MODEL = 1024
HEAD_DIM = 64
RET_HEADS = 8
SWA_Q_HEADS = 8
SWA_KV_HEADS = 2
SWA_GROUP = SWA_Q_HEADS // SWA_KV_HEADS
RET_WIDTH = RET_HEADS * HEAD_DIM
SWA_WIDTH = SWA_Q_HEADS * HEAD_DIM
KV_WIDTH = SWA_KV_HEADS * HEAD_DIM
MIX_WIDTH = RET_WIDTH + SWA_WIDTH
IN_WIDTH = 4 * RET_WIDTH + SWA_WIDTH + 2 * KV_WIDTH
CHUNK = 128
WINDOW = 128
RET_THETA = 10000.0
SWA_THETA = 500000.0
SWA_ROT_DIM = HEAD_DIM // 4
N_EXPERTS = 64
TOP_K = 8
N_GROUPS = 8
GROUP_SIZE = N_EXPERTS // N_GROUPS
TOPK_GROUPS = 4
EXPERT_DIM = 256
SHARED_DIM = 256
ROUTED_SCALE = 2.5
PLE_DIM = 256
LN_EPS = 1e-5
GN_EPS = 1e-6
NEG_INF = -1e30
DEPTH = 1
DEEPNORM_ALPHA = (2.0 * DEPTH) ** 0.25

OFF_RQ = 0
OFF_RK = RET_WIDTH
OFF_RV = 2 * RET_WIDTH
OFF_RG = 3 * RET_WIDTH
OFF_AQ = 4 * RET_WIDTH
OFF_AK = OFF_AQ + SWA_WIDTH
OFF_AV = OFF_AK + KV_WIDTH

V7X_LANES = 128
V7X_VMEM_LIMIT_BYTES = 56 * 1024 * 1024

MIX_CHUNKS = 4
MIX_TILE = MIX_CHUNKS * CHUNK
ROUTE_SUB = 256
ROUTE_TILE = 512
GRANULE = 16
ROW_CHUNK = 256
CHUNK_GRANULES = ROW_CHUNK // GRANULE
ONEHOT_PIECE = 64
SUB_GRANULES = ROUTE_SUB * TOP_K // GRANULE + N_EXPERTS
SUB_ROWS = SUB_GRANULES * GRANULE
SUB_TOP_BIT = 1 << (SUB_GRANULES.bit_length() - 1)
EXPERT_TILE = 512
TILE_GRANULES = EXPERT_TILE // GRANULE
BLOCK_TILES = 4
EXPERT_BLOCK = BLOCK_TILES * EXPERT_TILE
BLOCK_GRANULES = EXPERT_BLOCK // GRANULE
NG_BITS = 5
RUN_LOOP_PIECE = 4
TRIP_CHUNKS = 2
TRIP_GRANULES = TRIP_CHUNKS * CHUNK_GRANULES
FAR = 1.0e6

NT_DIMS = (((1,), (1,)), ((), ()))
TN_DIMS = (((0,), (0,)), ((), ()))


def _layer_norm(y, g, b):
    mu = jnp.mean(y, axis=-1, keepdims=True)
    d = y - mu
    var = jnp.mean(d * d, axis=-1, keepdims=True)
    return d * lax.rsqrt(var + LN_EPS) * g + b


def _silu(v):
    return v * jax.nn.sigmoid(v)


def _rope_inplace(buf, col0, n_groups, cos, sin, near, shift, scale):
    for g in range(n_groups):
        cols = slice(col0 + g * V7X_LANES, col0 + (g + 1) * V7X_LANES)
        xg = buf[:, cols]
        partner = jnp.where(near, pltpu.roll(xg, V7X_LANES - shift, 1), pltpu.roll(xg, shift, 1))
        y = xg * cos + partner * sin
        if scale != 1.0:
            y = y * scale
        buf[:, cols] = y


def _head_mean(y, head_avg):
    hi = y.astype(BF16)
    lo = (y - hi.astype(F32)).astype(BF16)
    return (jnp.dot(hi, head_avg, preferred_element_type=F32)
            + jnp.dot(lo, head_avg, preferred_element_type=F32))


def _mixer_kernel(x_ref, win_ref, wout_ref, cr_ref, sr_ref, ca_ref, sa_ref,
                  dec_ref, kw_ref, qw_ref, cd_ref, gn_ref, asc_ref, sink_ref,
                  g1_ref, b1_ref, o_ref, proj_s, mix_s, state_s, kprev_s, vprev_s):
    step = pl.program_id(1)

    @pl.when(step == 0)
    def _():
        state_s[...] = jnp.zeros_like(state_s)
        kprev_s[...] = jnp.zeros_like(kprev_s)
        vprev_s[...] = jnp.zeros_like(vprev_s)

    proj_s[...] = jnp.dot(x_ref[...].astype(BF16), win_ref[...], preferred_element_type=F32)

    lane = lax.broadcasted_iota(jnp.int32, (MIX_TILE, V7X_LANES), 1) % HEAD_DIM
    scale = HEAD_DIM ** -0.5
    cr, sr = cr_ref[...], sr_ref[...]
    _rope_inplace(proj_s, OFF_RQ, RET_WIDTH // V7X_LANES, cr, sr, lane < HEAD_DIM // 2, HEAD_DIM // 2, 1.0)
    _rope_inplace(proj_s, OFF_RK, RET_WIDTH // V7X_LANES, cr, sr, lane < HEAD_DIM // 2, HEAD_DIM // 2, scale)
    ca, sa = ca_ref[...], sa_ref[...]
    _rope_inplace(proj_s, OFF_AQ, SWA_WIDTH // V7X_LANES, ca, sa, lane < SWA_ROT_DIM // 2, SWA_ROT_DIM // 2, 1.0)
    _rope_inplace(proj_s, OFF_AK, KV_WIDTH // V7X_LANES, ca, sa, lane < SWA_ROT_DIM // 2, SWA_ROT_DIM // 2, 1.0)

    qi = lax.broadcasted_iota(jnp.int32, (CHUNK, 2 * CHUNK), 0)
    kj = lax.broadcasted_iota(jnp.int32, (CHUNK, 2 * CHUNK), 1)
    rel = CHUNK + qi - kj
    in_window = (rel >= 0) & (rel < WINDOW)

    sub = lax.broadcasted_iota(jnp.int32, (CHUNK, V7X_LANES), 0)
    lan = lax.broadcasted_iota(jnp.int32, (CHUNK, V7X_LANES), 1)
    low = lan < HEAD_DIM
    same_head = (sub // HEAD_DIM) == (lan // HEAD_DIM)
    head_avg = jnp.where(same_head, 1.0 / HEAD_DIM, 0.0).astype(BF16)
    row_sum = jnp.ones((2 * CHUNK, V7X_LANES), BF16)

    def chunk_body(c, carry):
        rows = pl.ds(pl.multiple_of(c * CHUNK, CHUNK), CHUNK)

        first = jnp.logical_and(step == 0, c == 0)
        valid = in_window & (kj >= jnp.where(first, CHUNK, 0))
        lanes = lambda off, p: slice(off + p * V7X_LANES, off + (p + 1) * V7X_LANES)

        q2s, v2s, s_ret, kvs = [], [], [], []
        for p in range(RET_HEADS // 2):
            q2 = proj_s[rows, lanes(OFF_RQ, p)]
            k2 = proj_s[rows, lanes(OFF_RK, p)]
            v2 = proj_s[rows, lanes(OFF_RV, p)]
            q_ab = jnp.concatenate([jnp.where(low, q2, 0.0), jnp.where(low, 0.0, q2)], axis=0).astype(BF16)
            s_ret.append(lax.dot_general(q_ab, k2.astype(BF16), NT_DIMS, preferred_element_type=F32))
            kvs.append(lax.dot_general((k2 * kw_ref[p]).astype(BF16), v2.astype(BF16), TN_DIMS,
                                       preferred_element_type=F32))
            q2s.append(q2)
            v2s.append(v2)

        kc = proj_s[rows, OFF_AK:OFF_AK + KV_WIDTH]
        vc = proj_s[rows, OFF_AV:OFF_AV + KV_WIDTH]
        kband = jnp.concatenate([kprev_s[...], kc], axis=0)
        vband = jnp.concatenate([vprev_s[...], vc], axis=0)
        kprev_s[...] = kc
        vprev_s[...] = vc
        kband_r = pltpu.roll(kband, HEAD_DIM, 1)
        vband_r = pltpu.roll(vband, HEAD_DIM, 1)
        low2 = jnp.concatenate([low, low], axis=0)
        s_att, v_cats = [], []
        for g in range(SWA_KV_HEADS):
            k_even, k_odd = (kband, kband_r) if g == 0 else (kband_r, kband)
            v_low, v_high = (vband, vband_r) if g == 0 else (vband_r, vband)
            v_cats.append(jnp.concatenate([jnp.where(low2, v_low, 0.0), jnp.where(low2, 0.0, v_high)],
                                          axis=0).astype(BF16))
            q_off = OFF_AQ + g * SWA_GROUP * HEAD_DIM
            qt0 = proj_s[rows, q_off:q_off + V7X_LANES]
            qt1 = proj_s[rows, q_off + V7X_LANES:q_off + 2 * V7X_LANES]
            q_even = jnp.concatenate([jnp.where(low, qt0, 0.0), jnp.where(low, qt1, 0.0)], axis=0)
            q_odd = jnp.concatenate([jnp.where(low, 0.0, qt0), jnp.where(low, 0.0, qt1)], axis=0)
            s_even = lax.dot_general(q_even.astype(BF16), k_even.astype(BF16), NT_DIMS,
                                     preferred_element_type=F32)
            s_odd = lax.dot_general(q_odd.astype(BF16), k_odd.astype(BF16), NT_DIMS,
                                    preferred_element_type=F32)
            s_att.append((s_even[:CHUNK], s_odd[:CHUNK]))
            s_att.append((s_even[CHUNK:], s_odd[CHUNK:]))

        ys = []
        for p in range(RET_HEADS // 2):
            s_a = (s_ret[p][:CHUNK] * dec_ref[2 * p]).astype(BF16)
            s_b = (s_ret[p][CHUNK:] * dec_ref[2 * p + 1]).astype(BF16)
            st = state_s[p]
            lhs = jnp.concatenate([s_a, s_b, (q2s[p] * qw_ref[p]).astype(BF16)], axis=1)
            rhs = jnp.concatenate([jnp.where(low, v2s[p], 0.0), jnp.where(low, 0.0, v2s[p]), st],
                                  axis=0).astype(BF16)
            ys.append(jnp.dot(lhs, rhs, preferred_element_type=F32))
            state_s[p] = st * cd_ref[p] + jnp.where(same_head, kvs[p], 0.0)

        def soft(s, h):
            s = jnp.where(valid, s * scale, NEG_INF)
            sink = sink_ref[h]
            m = jnp.maximum(jnp.max(s, axis=-1, keepdims=True), sink)
            e = jnp.exp(s - m).astype(BF16)
            den = jnp.dot(e, row_sum, preferred_element_type=F32) + jnp.exp(sink - m)
            return e, den

        es, dens = [], []
        for t, (sa, sb) in enumerate(s_att):
            e_a, den_a = soft(sa, 2 * t)
            e_b, den_b = soft(sb, 2 * t + 1)
            es.append(jnp.concatenate([e_a, e_b], axis=1))
            dens.append(jnp.where(low, den_a, den_b))

        mus = [_head_mean(y, head_avg) for y in ys]
        for t in range(SWA_Q_HEADS // 2):
            o = jnp.dot(es[t], v_cats[t // (SWA_GROUP // 2)], preferred_element_type=F32) / dens[t]
            mix_s[rows, lanes(RET_WIDTH, t)] = o * asc_ref[:, lanes(0, t)]

        ds = [y - mu for y, mu in zip(ys, mus)]
        variances = [_head_mean(d * d, head_avg) for d in ds]
        for p in range(RET_HEADS // 2):
            yn = ds[p] * lax.rsqrt(variances[p] + GN_EPS)
            gate = proj_s[rows, lanes(OFF_RG, p)]
            mix_s[rows, lanes(0, p)] = _silu(gate) * (yn * gn_ref[:, lanes(0, p)])
        return carry

    lax.fori_loop(0, MIX_CHUNKS, chunk_body, 0)

    y = jnp.dot(mix_s[...].astype(BF16), wout_ref[...], preferred_element_type=F32)
    y = y + DEEPNORM_ALPHA * x_ref[...]
    o_ref[...] = _layer_norm(y, g1_ref[...], b1_ref[...])


def _rope_tables(seq, rot_dim, theta):
    half = rot_dim // 2
    inv_freq = theta ** (-jnp.arange(half, dtype=F32) / half)
    ang = jnp.arange(seq, dtype=F32)[:, None] * inv_freq[None, :]
    cos, sin = jnp.cos(ang), jnp.sin(ang)
    rest = HEAD_DIM - rot_dim
    cos_h = jnp.concatenate([cos, cos, jnp.ones((seq, rest), F32)], axis=1)
    sin_h = jnp.concatenate([-sin, sin, jnp.zeros((seq, rest), F32)], axis=1)
    reps = V7X_LANES // HEAD_DIM
    return jnp.tile(cos_h, (1, reps)), jnp.tile(sin_h, (1, reps))


def _retention_tables():
    h = RET_HEADS
    log_g = jnp.log1p(-(2.0 ** (-5.0 - jnp.arange(h, dtype=F32))))
    idx = jnp.arange(CHUNK, dtype=F32)
    diff = idx[:, None] - idx[None, :]
    intra = jnp.where(diff[None] >= 0, jnp.exp(jnp.maximum(diff, 0.0)[None] * log_g[:, None, None]), 0.0)
    k_w = jnp.exp((CHUNK - 1 - idx)[:, None] * log_g[None, :])
    q_w = jnp.exp((idx + 1)[:, None] * log_g[None, :])
    chunk_decay = jnp.exp(CHUNK * log_g)
    pairs = lambda t: jnp.repeat(t.T.reshape(h // 2, 2, -1), HEAD_DIM, axis=1).transpose(0, 2, 1)
    return intra, pairs(k_w), pairs(q_w), pairs(chunk_decay[None, :])


def _mixer(x, w_in, w_out, gn_gain, attn_scale, sinks, ln_g, ln_b):
    b, s, d = x.shape
    cr, sr = _rope_tables(s, HEAD_DIM, RET_THETA)
    ca, sa = _rope_tables(s, SWA_ROT_DIM, SWA_THETA)
    dec, kw, qw, cd = _retention_tables()
    const = lambda shape: pl.BlockSpec(shape, lambda bi, i: (0,) * len(shape))
    tab = pl.BlockSpec((MIX_TILE, V7X_LANES), lambda bi, i: (i, 0))
    smem = pl.BlockSpec(memory_space=pltpu.SMEM)
    return pl.pallas_call(
        _mixer_kernel,
        name="mixer",
        grid=(b, s // MIX_TILE),
        in_specs=[
            pl.BlockSpec((None, MIX_TILE, d), lambda bi, i: (bi, i, 0)),
            const((d, IN_WIDTH)), const((MIX_WIDTH, d)),
            tab, tab, tab, tab,
            const((RET_HEADS, CHUNK, CHUNK)), const((RET_HEADS // 2, CHUNK, V7X_LANES)),
            const((RET_HEADS // 2, CHUNK, V7X_LANES)), const((RET_HEADS // 2, 1, V7X_LANES)),
            const((1, RET_WIDTH)), const((1, SWA_WIDTH)), smem,
            const((1, d)), const((1, d)),
        ],
        out_specs=pl.BlockSpec((None, MIX_TILE, d), lambda bi, i: (bi, i, 0)),
        out_shape=jax.ShapeDtypeStruct((b, s, d), F32),
        scratch_shapes=[
            pltpu.VMEM((MIX_TILE, IN_WIDTH), F32),
            pltpu.VMEM((MIX_TILE, MIX_WIDTH), F32),
            pltpu.VMEM((RET_HEADS // 2, V7X_LANES, V7X_LANES), F32),
            pltpu.VMEM((CHUNK, KV_WIDTH), F32),
            pltpu.VMEM((CHUNK, KV_WIDTH), F32),
        ],
        compiler_params=pltpu.CompilerParams(
            dimension_semantics=("parallel", "arbitrary"),
            vmem_limit_bytes=V7X_VMEM_LIMIT_BYTES),
    )(x, w_in.astype(BF16), w_out.astype(BF16), cr, sr, ca, sa, dec, kw, qw, cd,
      gn_gain.reshape(1, -1), attn_scale.reshape(1, -1), sinks, ln_g.reshape(1, -1), ln_b.reshape(1, -1))


def _router_kernel(x_ref, wr_ref, bias_ref, tri_ref, wt_ref, rk_ref, cnt_ref):
    tt = x_ref.shape[0]
    x = x_ref[...]
    xh = x.astype(BF16)
    xl = (x - xh.astype(F32)).astype(BF16)
    w = wr_ref[...]
    wh = w.astype(BF16)
    wl = (w - wh.astype(F32)).astype(BF16)
    dg = functools.partial(lax.dot_general, dimension_numbers=NT_DIMS, preferred_element_type=F32)
    logits = dg(wh, xh) + (dg(wh, xl) + dg(wl, xh))
    scores = jax.nn.sigmoid(logits)
    biased = scores + bias_ref[...]

    shape3 = (N_GROUPS, GROUP_SIZE, tt)
    grp = lax.broadcasted_iota(jnp.int32, shape3, 0)
    mem = lax.broadcasted_iota(jnp.int32, shape3, 1)
    b3 = biased.reshape(shape3)

    m1 = jnp.max(b3, axis=1, keepdims=True)
    i1 = jnp.min(jnp.where(b3 == m1, mem, GROUP_SIZE), axis=1, keepdims=True)
    m2 = jnp.max(jnp.where(mem == i1, -jnp.inf, b3), axis=1, keepdims=True)
    gs = jnp.broadcast_to(m1 + m2, shape3)

    gmask = jnp.zeros(shape3, F32)
    for _ in range(TOPK_GROUPS):
        m = jnp.max(gs, axis=0, keepdims=True)
        pick = grp == jnp.min(jnp.where(gs == m, grp, N_GROUPS), axis=0, keepdims=True)
        gmask = jnp.where(pick, 1.0, gmask)
        gs = jnp.where(pick, -jnp.inf, gs)

    eid = grp * GROUP_SIZE + mem
    cand = jnp.where(gmask > 0.0, b3, NEG_INF)
    sel = jnp.zeros(shape3, F32)
    for _ in range(TOP_K):
        m = jnp.max(jnp.max(cand, axis=0, keepdims=True), axis=1, keepdims=True)
        first = jnp.where(cand == m, eid, N_EXPERTS)
        first = jnp.min(jnp.min(first, axis=0, keepdims=True), axis=1, keepdims=True)
        pick = eid == first
        sel = jnp.where(pick, 1.0, sel)
        cand = jnp.where(pick, -jnp.inf, cand)

    sel = sel.reshape(N_EXPERTS, tt)
    wsel = scores * sel
    comb = wsel / jnp.sum(wsel, axis=0, keepdims=True) * ROUTED_SCALE
    wt_ref[...] = comb

    lane = lax.broadcasted_iota(jnp.int32, (N_EXPERTS, V7X_LANES), 1)
    cnt = jnp.zeros((N_EXPERTS, V7X_LANES), F32)
    for sb in range(tt // ROUTE_SUB):
        cols = slice(sb * ROUTE_SUB, (sb + 1) * ROUTE_SUB)
        sel_sb = sel[:, cols]
        rank = jnp.dot(sel_sb.astype(BF16), tri_ref[...], preferred_element_type=F32)
        rk_ref[:, cols] = jnp.where(sel_sb > 0.0, rank, -1.0)
        cnt = jnp.where(lane == sb, jnp.sum(sel_sb, axis=1, keepdims=True), cnt)
    cnt_ref[...] = cnt


def _router(x1, w_router, router_bias):
    t, d = x1.shape
    n_steps = t // ROUTE_TILE
    tri = (jnp.arange(ROUTE_SUB)[:, None] < jnp.arange(ROUTE_SUB)[None, :]).astype(BF16)
    return pl.pallas_call(
        _router_kernel,
        name="router",
        grid=(n_steps,),
        in_specs=[
            pl.BlockSpec((ROUTE_TILE, d), lambda i: (i, 0)),
            pl.BlockSpec((N_EXPERTS, d), lambda i: (0, 0)),
            pl.BlockSpec((N_EXPERTS, 1), lambda i: (0, 0)),
            pl.BlockSpec((ROUTE_SUB, ROUTE_SUB), lambda i: (0, 0)),
        ],
        out_specs=[
            pl.BlockSpec((N_EXPERTS, ROUTE_TILE), lambda i: (0, i)),
            pl.BlockSpec((N_EXPERTS, ROUTE_TILE), lambda i: (0, i)),
            pl.BlockSpec((None, N_EXPERTS, V7X_LANES), lambda i: (i, 0, 0)),
        ],
        out_shape=[
            jax.ShapeDtypeStruct((N_EXPERTS, t), F32),
            jax.ShapeDtypeStruct((N_EXPERTS, t), F32),
            jax.ShapeDtypeStruct((n_steps, N_EXPERTS, V7X_LANES), F32),
        ],
        compiler_params=pltpu.CompilerParams(
            dimension_semantics=("parallel",),
            vmem_limit_bytes=V7X_VMEM_LIMIT_BYTES),
    )(x1, w_router.T, router_bias.reshape(-1, 1), tri)


def _run_copy(src, dst, src_g, dst_g, n_g, sem):
    return pltpu.make_async_copy(src.at[pl.ds(src_g, n_g)], dst.at[pl.ds(dst_g, n_g)], sem)


def _for_each_piece(n_g, max_bit, fn):
    b = max_bit
    while b >= 1:
        pl.when((n_g & b) != 0)(functools.partial(fn, n_g & ~(2 * b - 1), b))
        b //= 2


def _for_each_run_piece(n_g, fn):
    def body(i, carry):
        fn(i * RUN_LOOP_PIECE, RUN_LOOP_PIECE)
        return carry

    lax.fori_loop(0, lax.shift_right_logical(n_g, RUN_LOOP_PIECE.bit_length() - 1), body, 0)
    _for_each_piece(n_g, RUN_LOOP_PIECE // 2, fn)


def _wait_granules(src, dst, n_g, max_bit, sem):
    _for_each_piece(n_g, max_bit, lambda off, size: _run_copy(src, dst, 0, 0, size, sem).wait())


def _sorted_positions(rk_ref, goff_ref):
    rk = rk_ref[...]
    pos = jnp.where(rk >= 0.0, rk + GRANULE * goff_ref[...], FAR)
    slots, owners = [], []
    for _ in range(TOP_K):
        m = jnp.min(pos, axis=0, keepdims=True)
        own = pos == m
        slots.append(m)
        owners.append(own)
        pos = jnp.where(own, FAR, pos)
    return slots, owners


def _n_trips(tot_g):
    pair_granules = 2 * TRIP_GRANULES
    assert SUB_GRANULES % pair_granules == 0 and pair_granules & (pair_granules - 1) == 0
    return 2 * lax.shift_right_logical(tot_g + (pair_granules - 1), pair_granules.bit_length() - 1)


def _chunk_onehot(c, slots, values):
    rid = lax.broadcasted_iota(jnp.int32, (ONEHOT_PIECE, ROUTE_SUB), 0).astype(F32).astype(BF16)
    vals = [jnp.asarray(v, F32).astype(BF16) for v in values]
    pieces = []
    for p in range(ROW_CHUNK // ONEHOT_PIECE):
        base = jnp.asarray(c * ROW_CHUNK + p * ONEHOT_PIECE, F32)
        acc = jnp.zeros((ONEHOT_PIECE, ROUTE_SUB), BF16)
        for m, v in zip(slots, vals):
            acc = jnp.where(rid == (m - base).astype(BF16), v, acc)
        pieces.append(acc)
    return jnp.concatenate(pieces, axis=0)


def _pipelined_trips(n_trips, p_buf, slots, values, consume):
    def build(i, half):
        p_buf[half] = jnp.concatenate(
            [_chunk_onehot(i * TRIP_CHUNKS + u, slots, values) for u in range(TRIP_CHUNKS)], axis=0)

    build(0, 0)

    def pair(k, carry):
        build(2 * k + 1, 1)
        consume(2 * k, p_buf[0])
        build(2 * k + 2, 0)
        consume(2 * k + 1, p_buf[1])
        return carry

    lax.fori_loop(0, lax.shift_right_logical(n_trips, 1), pair, 0)


def _serial_trips(n_trips, slots, values, consume):
    def trip(i, carry):
        consume(i, jnp.concatenate(
            [_chunk_onehot(i * TRIP_CHUNKS + u, slots, values) for u in range(TRIP_CHUNKS)], axis=0))
        return carry

    lax.fori_loop(0, n_trips, trip, 0)


def _dispatch_kernel(run_ref, dst_ref, tot_ref, tail_ref, x_ref, rk_ref, goff_ref, xs_hbm,
                     s_buf, p_buf, zero_buf, sems, tail_sem):
    j = pl.program_id(0)
    last = pl.num_programs(0) - 1
    slot = j % 2
    stage = s_buf.at[slot]

    @pl.when(j >= 2)
    def _():
        _wait_granules(stage, xs_hbm, tot_ref[j - 2], SUB_TOP_BIT, sems.at[slot])

    slots, _ = _sorted_positions(rk_ref, goff_ref)
    xb = x_ref[...].astype(BF16)

    def consume(i, onehot):
        rows = jnp.dot(onehot, xb, preferred_element_type=F32).astype(BF16)
        g0 = pl.multiple_of(i * TRIP_GRANULES, TRIP_GRANULES)
        stage[pl.ds(g0, TRIP_GRANULES)] = rows.reshape(TRIP_GRANULES, GRANULE, rows.shape[1])

    _pipelined_trips(_n_trips(tot_ref[j]), p_buf, slots, [1.0] * TOP_K, consume)

    def run(e, carry):
        word = run_ref[j * N_EXPERTS + e]
        n_g = word & (2 ** NG_BITS - 1)
        src_g = word >> NG_BITS
        dst_g = dst_ref[j * N_EXPERTS + e]
        _for_each_run_piece(n_g, lambda off, size: _run_copy(
            stage, xs_hbm, src_g + off, dst_g + off, size, sems.at[slot]).start())
        return carry

    lax.fori_loop(0, N_EXPERTS, run, 0)

    @pl.when(j == last)
    def _():
        zero_buf[...] = jnp.zeros_like(zero_buf)

        def tail(e, total):
            n_g = tail_ref[N_EXPERTS + e]
            _for_each_piece(n_g, TILE_GRANULES // 2, lambda off, size: _run_copy(
                zero_buf, xs_hbm, 0, tail_ref[e] + off, size, tail_sem).start())
            return total + n_g

        total = lax.fori_loop(0, N_EXPERTS, tail, 0)
        lax.fori_loop(0, total // TILE_GRANULES, lambda i, c: (
            _run_copy(zero_buf, xs_hbm, 0, 0, TILE_GRANULES, tail_sem).wait(), c)[1], 0)
        _wait_granules(zero_buf, xs_hbm, total % TILE_GRANULES, TILE_GRANULES // 2, tail_sem)

        @pl.when(j >= 1)
        def _():
            other = s_buf.at[1 - slot]
            _wait_granules(other, xs_hbm, tot_ref[j - 1], SUB_TOP_BIT, sems.at[1 - slot])

        _wait_granules(stage, xs_hbm, tot_ref[j], SUB_TOP_BIT, sems.at[slot])


def _dispatch(x1, rk, goff, run_tab, dst_tab, tot_tab, tail_tab, n_rows):
    t, d = x1.shape
    n_sub = t // ROUTE_SUB
    return pl.pallas_call(
        _dispatch_kernel,
        name="dispatch",
        grid_spec=pltpu.PrefetchScalarGridSpec(
            num_scalar_prefetch=4,
            grid=(n_sub,),
            in_specs=[
                pl.BlockSpec((ROUTE_SUB, d), lambda j, *_: (j, 0)),
                pl.BlockSpec((N_EXPERTS, ROUTE_SUB), lambda j, *_: (0, j)),
                pl.BlockSpec((None, N_EXPERTS, 1), lambda j, *_: (j, 0, 0)),
            ],
            out_specs=pl.BlockSpec(memory_space=pl.ANY),
            scratch_shapes=[
                pltpu.VMEM((2, SUB_GRANULES, GRANULE, d), BF16),
                pltpu.VMEM((2, TRIP_GRANULES * GRANULE, ROUTE_SUB), BF16),
                pltpu.VMEM((TILE_GRANULES, GRANULE, d), BF16),
                pltpu.SemaphoreType.DMA((2,)),
                pltpu.SemaphoreType.DMA,
            ],
        ),
        out_shape=jax.ShapeDtypeStruct((n_rows // GRANULE, GRANULE, d), BF16),
        compiler_params=pltpu.CompilerParams(
            dimension_semantics=("arbitrary",),
            vmem_limit_bytes=V7X_VMEM_LIMIT_BYTES),
    )(run_tab, dst_tab, tot_tab, tail_tab, x1, rk, goff)


def _expert_kernel(be_ref, bt_ref, used_ref, xs_ref, w1_ref, w3_ref, w2_ref, ys_ref, w1_s, w3_s, w2_s):
    t = pl.program_id(0)

    @pl.when(t < used_ref[0])
    def _():
        @pl.when(jnp.logical_or(t == 0, be_ref[t] != be_ref[jnp.maximum(t - 1, 0)]))
        def _():
            w1_s[...] = w1_ref[...].astype(BF16)
            w3_s[...] = w3_ref[...].astype(BF16)
            w2_s[...] = w2_ref[...].astype(BF16)

        for q in range(BLOCK_TILES):
            @pl.when(q < bt_ref[t])
            def _():
                rows = pl.ds(q * EXPERT_TILE, EXPERT_TILE)
                x = xs_ref[rows, :]
                h = _silu(jnp.dot(x, w1_s[...], preferred_element_type=F32))
                h = h * jnp.dot(x, w3_s[...], preferred_element_type=F32)
                ys_ref[rows, :] = jnp.dot(h.astype(BF16), w2_s[...], preferred_element_type=F32).astype(BF16)


def _experts(xs, w1, w3, w2, block_expert, block_tiles, n_used):
    n_rows, d = xs.shape
    n_blocks = n_rows // EXPERT_BLOCK
    row_map = lambda t, be, bt, used: (jnp.minimum(t, used[0] - 1), 0)
    w_map = lambda t, be, bt, used: (be[t], 0, 0)
    return pl.pallas_call(
        _expert_kernel,
        name="experts",
        grid_spec=pltpu.PrefetchScalarGridSpec(
            num_scalar_prefetch=3,
            grid=(n_blocks,),
            in_specs=[
                pl.BlockSpec((EXPERT_BLOCK, d), row_map),
                pl.BlockSpec((None, d, EXPERT_DIM), w_map),
                pl.BlockSpec((None, d, EXPERT_DIM), w_map),
                pl.BlockSpec((None, EXPERT_DIM, d), w_map),
            ],
            out_specs=pl.BlockSpec((EXPERT_BLOCK, d), row_map),
            scratch_shapes=[
                pltpu.VMEM((d, EXPERT_DIM), BF16),
                pltpu.VMEM((d, EXPERT_DIM), BF16),
                pltpu.VMEM((EXPERT_DIM, d), BF16),
            ],
        ),
        out_shape=jax.ShapeDtypeStruct((n_rows, d), BF16),
        compiler_params=pltpu.CompilerParams(
            dimension_semantics=("arbitrary",),
            vmem_limit_bytes=V7X_VMEM_LIMIT_BYTES),
    )(block_expert, block_tiles, n_used, xs, w1, w3, w2)


def _combine_kernel(run_ref, dst_ref, tot_ref, x_ref, p_ref, rk_ref, wt_ref, goff_ref, ys_hbm,
                    ws1_ref, ws3_ref, ws2_ref, wg_ref, bg_ref, wp_ref, g2_ref, b2_ref,
                    o_ref, s_buf, acc_s, sems):
    j = pl.program_id(0)
    last = pl.num_programs(0) - 1
    slot = j % 2

    def gather(jj, sl):
        def run(e, carry):
            word = run_ref[jj * N_EXPERTS + e]
            n_g = word & (2 ** NG_BITS - 1)
            dst_g = word >> NG_BITS
            src_g = dst_ref[jj * N_EXPERTS + e]
            _for_each_run_piece(n_g, lambda off, size: _run_copy(
                ys_hbm, s_buf.at[sl], src_g + off, dst_g + off, size, sems.at[sl]).start())
            return carry

        lax.fori_loop(0, N_EXPERTS, run, 0)

    @pl.when(j == 0)
    def _():
        gather(j, slot)

    @pl.when(j < last)
    def _():
        gather(j + 1, 1 - slot)

    x = x_ref[...]
    xb = x.astype(BF16)
    hs = _silu(jnp.dot(xb, ws1_ref[...], preferred_element_type=F32))
    hs = hs * jnp.dot(xb, ws3_ref[...], preferred_element_type=F32)
    shared = jnp.dot(hs.astype(BF16), ws2_ref[...], preferred_element_type=F32)
    gate = jax.nn.sigmoid(jnp.dot(xb, wg_ref[...], preferred_element_type=F32) + bg_ref[...])
    side = jnp.dot(p_ref[...].astype(BF16), wp_ref[...], preferred_element_type=F32)
    acc_s[...] = DEEPNORM_ALPHA * x + shared + gate * side

    slots, owners = _sorted_positions(rk_ref, goff_ref)
    wt = wt_ref[...]
    weights = [jnp.sum(jnp.where(own, wt, 0.0), axis=0, keepdims=True) for own in owners]

    stage = s_buf.at[slot]
    tot = tot_ref[j]
    _wait_granules(ys_hbm, stage, tot, SUB_TOP_BIT, sems.at[slot])

    n_trips = _n_trips(tot)
    pad_g = n_trips * TRIP_GRANULES - tot

    def zero(g, carry):
        stage[pl.ds(tot + g, 1)] = jnp.zeros((1,) + stage.shape[1:], BF16)
        return carry

    lax.fori_loop(0, pad_g, zero, 0)

    def consume(i, scat):
        g0 = pl.multiple_of(i * TRIP_GRANULES, TRIP_GRANULES)
        rows = stage[pl.ds(g0, TRIP_GRANULES)].reshape(TRIP_GRANULES * GRANULE, stage.shape[2])
        acc_s[...] += lax.dot_general(scat, rows, TN_DIMS, preferred_element_type=F32)

    _serial_trips(n_trips, slots, weights, consume)
    o_ref[...] = _layer_norm(acc_s[...], g2_ref[...], b2_ref[...])


def _combine(x1, p, rk, wt, goff, ys, run_tab, dst_tab, tot_tab, ws1, ws3, ws2, wg, bg, wp, ln_g, ln_b):
    t, d = x1.shape
    n_sub = t // ROUTE_SUB
    const = lambda shape: pl.BlockSpec(shape, lambda j, *_: (0,) * len(shape))
    return pl.pallas_call(
        _combine_kernel,
        name="combine",
        grid_spec=pltpu.PrefetchScalarGridSpec(
            num_scalar_prefetch=3,
            grid=(n_sub,),
            in_specs=[
                pl.BlockSpec((ROUTE_SUB, d), lambda j, *_: (j, 0)),
                pl.BlockSpec((ROUTE_SUB, PLE_DIM), lambda j, *_: (j, 0)),
                pl.BlockSpec((N_EXPERTS, ROUTE_SUB), lambda j, *_: (0, j)),
                pl.BlockSpec((N_EXPERTS, ROUTE_SUB), lambda j, *_: (0, j)),
                pl.BlockSpec((None, N_EXPERTS, 1), lambda j, *_: (j, 0, 0)),
                pl.BlockSpec(memory_space=pl.ANY),
                const((d, SHARED_DIM)), const((d, SHARED_DIM)), const((SHARED_DIM, d)),
                const((d, d)), const((1, d)), const((PLE_DIM, d)), const((1, d)), const((1, d)),
            ],
            out_specs=pl.BlockSpec((ROUTE_SUB, d), lambda j, *_: (j, 0)),
            scratch_shapes=[
                pltpu.VMEM((2, SUB_GRANULES, GRANULE, d), BF16),
                pltpu.VMEM((ROUTE_SUB, d), F32),
                pltpu.SemaphoreType.DMA((2,)),
            ],
        ),
        out_shape=jax.ShapeDtypeStruct((t, d), F32),
        compiler_params=pltpu.CompilerParams(
            dimension_semantics=("arbitrary",),
            vmem_limit_bytes=V7X_VMEM_LIMIT_BYTES),
    )(run_tab, dst_tab, tot_tab, x1, p, rk, wt, goff, ys,
      ws1.astype(BF16), ws3.astype(BF16), ws2.astype(BF16),
      wg.astype(BF16), bg.reshape(1, -1), wp.astype(BF16), ln_g.reshape(1, -1), ln_b.reshape(1, -1))


def _routing_tables(cnt, n_blocks):
    ng = (cnt + GRANULE - 1) // GRANULE
    goff = jnp.cumsum(ng, axis=1) - ng
    tot = jnp.sum(ng, axis=1)
    per_expert = jnp.sum(ng, axis=0)
    tiles = (per_expert + TILE_GRANULES - 1) // TILE_GRANULES
    padded = (per_expert + BLOCK_GRANULES - 1) // BLOCK_GRANULES * BLOCK_GRANULES
    seg_end = jnp.cumsum(padded)
    base = seg_end - padded
    dst = base[None, :] + jnp.cumsum(ng, axis=0) - ng
    run = goff * (2 ** NG_BITS) + ng
    tail = jnp.concatenate([base + per_expert, tiles * TILE_GRANULES - per_expert])
    n_used = seg_end[-1:] // BLOCK_GRANULES
    block_end = seg_end // BLOCK_GRANULES
    block_id = jnp.arange(n_blocks)
    block_expert = jnp.minimum(jnp.sum(block_end[None, :] <= block_id[:, None], axis=1), N_EXPERTS - 1)
    done_tiles = (block_id - (base // BLOCK_GRANULES)[block_expert]) * BLOCK_TILES
    block_tiles = jnp.clip(tiles[block_expert] - done_tiles, 0, BLOCK_TILES)
    i32 = lambda a: a.astype(jnp.int32)
    return (goff.astype(F32)[:, :, None], i32(run.reshape(-1)), i32(dst.reshape(-1)), i32(tot), i32(tail),
            i32(block_expert), i32(block_tiles), i32(n_used))


def _moe_block(x1, p, w_router, router_bias, w1, w3, w2, ws1, ws3, ws2, wg, bg, wp, ln_g, ln_b):
    t, d = x1.shape
    n_sub = t // ROUTE_SUB
    wt, rk, cnt = _router(x1, w_router, router_bias)
    per_step = ROUTE_TILE // ROUTE_SUB
    cnt = jnp.swapaxes(cnt[:, :, :per_step], 1, 2).reshape(n_sub, N_EXPERTS).astype(jnp.int32)
    max_granules = t * TOP_K // GRANULE + n_sub * N_EXPERTS + N_EXPERTS * (BLOCK_GRANULES - 1)
    n_blocks = (max_granules + BLOCK_GRANULES - 1) // BLOCK_GRANULES
    (goff, run_tab, dst_tab, tot_tab, tail_tab,
     block_expert, block_tiles, n_used) = _routing_tables(cnt, n_blocks)
    xs = _dispatch(x1, rk, goff, run_tab, dst_tab, tot_tab, tail_tab, n_blocks * EXPERT_BLOCK)
    ys = _experts(xs.reshape(-1, d), w1, w3, w2, block_expert, block_tiles, n_used).reshape(xs.shape)
    return _combine(x1, p, rk, wt, goff, ys, run_tab, dst_tab, tot_tab, ws1, ws3, ws2, wg, bg, wp, ln_g, ln_b)


def kernel(x, p, w_in, ret_gn_gain, attn_scale, sinks, w_out, ln1_g, ln1_b, w_router, router_bias,
           w1, w3, w2, ws1, ws3, ws2, w_ple_gate, b_ple_gate, w_ple_proj, ln2_g, ln2_b):
    b, s, d = x.shape
    h = x
    for i in range(DEPTH):
        h = _mixer(h, w_in[i], w_out[i], ret_gn_gain[i], attn_scale[i], sinks[i], ln1_g[i], ln1_b[i])
        h2 = _moe_block(h.reshape(b * s, d), p[i].reshape(b * s, -1), w_router[i], router_bias[i],
                        w1[i], w3[i], w2[i], ws1[i], ws3[i], ws2[i],
                        w_ple_gate[i], b_ple_gate[i], w_ple_proj[i], ln2_g[i], ln2_b[i])
        h = h2.reshape(b, s, d)
    return h
```

```python
import functools

import jax
import jax.numpy as jnp
from jax import lax
from jax.experimental import pallas as pl
from jax.experimental.pallas import tpu as pltpu

F32 = jnp.float32
BF16 = jnp.bfloat16

D_MODEL = 1024
HEAD_DIM = 64
RET_HEADS = 8
SWA_Q_HEADS = 8
SWA_KV_HEADS = 2
SWA_GROUP = SWA_Q_HEADS // SWA_KV_HEADS
RET_WIDTH = RET_HEADS * HEAD_DIM
SWA_WIDTH = SWA_Q_HEADS * HEAD_DIM
KV_WIDTH = SWA_KV_HEADS * HEAD_DIM
MIX_WIDTH = RET_WIDTH + SWA_WIDTH
IN_WIDTH = 4 * RET_WIDTH + SWA_WIDTH + 2 * KV_WIDTH
CHUNK = 128
WINDOW = 128
RET_THETA = 10000.0
SWA_THETA = 500000.0
SWA_ROT_DIM = HEAD_DIM // 4
N_EXPERTS = 64
TOP_K = 8
N_GROUPS = 8
GROUP_SIZE = N_EXPERTS // N_GROUPS
TOPK_GROUPS = 4
EXPERT_DIM = 256
SHARED_DIM = 256
ROUTED_SCALE = 2.5
PLE_DIM = 256
LN_EPS = 1e-5
GN_EPS = 1e-6
NEG_INF = -1e30
DEPTH = 1
DEEPNORM_ALPHA = (2.0 * DEPTH) ** 0.25

OFF_RQ = 0
OFF_RK = RET_WIDTH
OFF_RV = 2 * RET_WIDTH
OFF_RG = 3 * RET_WIDTH
OFF_AQ = 4 * RET_WIDTH
OFF_AK = OFF_AQ + SWA_WIDTH
OFF_AV = OFF_AK + KV_WIDTH

V7X_LANES = 128
V7X_VMEM_LIMIT_BYTES = 56 * 1024 * 1024

MIX_CHUNKS = 4
MIX_TILE = MIX_CHUNKS * CHUNK
ROUTE_SUB = 256
ROUTE_TILE = 512
SLOT_ROWS = 48
SLOT_GROUP = 16
EXPERT_TILE = 512
EXPERT_BLOCK = 2048

NT_DIMS = (((1,), (1,)), ((), ()))
TN_DIMS = (((0,), (0,)), ((), ()))


def _layer_norm(y, g, b):
    mu = jnp.mean(y, axis=-1, keepdims=True)
    d = y - mu
    var = jnp.mean(d * d, axis=-1, keepdims=True)
    return d * lax.rsqrt(var + LN_EPS) * g + b


def _silu(v):
    return v * jax.nn.sigmoid(v)


def _rope_inplace(buf, col0, n_groups, cos, sin, near, shift, scale):
    for g in range(n_groups):
        cols = slice(col0 + g * V7X_LANES, col0 + (g + 1) * V7X_LANES)
        xg = buf[:, cols]
        partner = jnp.where(near, pltpu.roll(xg, V7X_LANES - shift, 1), pltpu.roll(xg, shift, 1))
        y = xg * cos + partner * sin
        if scale != 1.0:
            y = y * scale
        buf[:, cols] = y


def _head_mean(y, head_avg):
    hi = y.astype(BF16)
    lo = (y - hi.astype(F32)).astype(BF16)
    return (jnp.dot(hi, head_avg, preferred_element_type=F32)
            + jnp.dot(lo, head_avg, preferred_element_type=F32))


def _mixer_kernel(x_ref, win_ref, wout_ref, cr_ref, sr_ref, ca_ref, sa_ref,
                  dec_ref, kw_ref, qw_ref, cd_ref, gn_ref, asc_ref, sink_ref,
                  g1_ref, b1_ref, o_ref, proj_s, mix_s, state_s, kprev_s, vprev_s):
    step = pl.program_id(1)

    @pl.when(step == 0)
    def _():
        state_s[...] = jnp.zeros_like(state_s)
        kprev_s[...] = jnp.zeros_like(kprev_s)
        vprev_s[...] = jnp.zeros_like(vprev_s)

    proj_s[...] = jnp.dot(x_ref[...].astype(BF16), win_ref[...], preferred_element_type=F32)

    lane = lax.broadcasted_iota(jnp.int32, (MIX_TILE, V7X_LANES), 1) % HEAD_DIM
    scale = HEAD_DIM ** -0.5
    cr, sr = cr_ref[...], sr_ref[...]
    _rope_inplace(proj_s, OFF_RQ, RET_WIDTH // V7X_LANES, cr, sr, lane < HEAD_DIM // 2, HEAD_DIM // 2, 1.0)
    _rope_inplace(proj_s, OFF_RK, RET_WIDTH // V7X_LANES, cr, sr, lane < HEAD_DIM // 2, HEAD_DIM // 2, scale)
    ca, sa = ca_ref[...], sa_ref[...]
    _rope_inplace(proj_s, OFF_AQ, SWA_WIDTH // V7X_LANES, ca, sa, lane < SWA_ROT_DIM // 2, SWA_ROT_DIM // 2, 1.0)
    _rope_inplace(proj_s, OFF_AK, KV_WIDTH // V7X_LANES, ca, sa, lane < SWA_ROT_DIM // 2, SWA_ROT_DIM // 2, 1.0)

    qi = lax.broadcasted_iota(jnp.int32, (CHUNK, 2 * CHUNK), 0)
    kj = lax.broadcasted_iota(jnp.int32, (CHUNK, 2 * CHUNK), 1)
    rel = CHUNK + qi - kj
    in_window = (rel >= 0) & (rel < WINDOW)

    sub = lax.broadcasted_iota(jnp.int32, (CHUNK, V7X_LANES), 0)
    lan = lax.broadcasted_iota(jnp.int32, (CHUNK, V7X_LANES), 1)
    low = lan < HEAD_DIM
    same_head = (sub // HEAD_DIM) == (lan // HEAD_DIM)
    head_avg = jnp.where(same_head, 1.0 / HEAD_DIM, 0.0).astype(BF16)
    row_sum = jnp.ones((2 * CHUNK, V7X_LANES), BF16)

    def chunk_body(c, carry):
        rows = pl.ds(pl.multiple_of(c * CHUNK, CHUNK), CHUNK)

        first = jnp.logical_and(step == 0, c == 0)
        valid = in_window & (kj >= jnp.where(first, CHUNK, 0))
        lanes = lambda off, p: slice(off + p * V7X_LANES, off + (p + 1) * V7X_LANES)

        q2s, v2s, s_ret, kvs = [], [], [], []
        for p in range(RET_HEADS // 2):
            q2 = proj_s[rows, lanes(OFF_RQ, p)]
            k2 = proj_s[rows, lanes(OFF_RK, p)]
            v2 = proj_s[rows, lanes(OFF_RV, p)]
            q_ab = jnp.concatenate([jnp.where(low, q2, 0.0), jnp.where(low, 0.0, q2)], axis=0).astype(BF16)
            s_ret.append(lax.dot_general(q_ab, k2.astype(BF16), NT_DIMS, preferred_element_type=F32))
            kvs.append(lax.dot_general((k2 * kw_ref[p]).astype(BF16), v2.astype(BF16), TN_DIMS,
                                       preferred_element_type=F32))
            q2s.append(q2)
            v2s.append(v2)

        kc = proj_s[rows, OFF_AK:OFF_AK + KV_WIDTH]
        vc = proj_s[rows, OFF_AV:OFF_AV + KV_WIDTH]
        kband = jnp.concatenate([kprev_s[...], kc], axis=0)
        vband = jnp.concatenate([vprev_s[...], vc], axis=0)
        kprev_s[...] = kc
        vprev_s[...] = vc
        kband_r = pltpu.roll(kband, HEAD_DIM, 1)
        vband_r = pltpu.roll(vband, HEAD_DIM, 1)
        low2 = jnp.concatenate([low, low], axis=0)
        s_att, v_cats = [], []
        for g in range(SWA_KV_HEADS):
            k_even, k_odd = (kband, kband_r) if g == 0 else (kband_r, kband)
            v_low, v_high = (vband, vband_r) if g == 0 else (vband_r, vband)
            v_cats.append(jnp.concatenate([jnp.where(low2, v_low, 0.0), jnp.where(low2, 0.0, v_high)],
                                          axis=0).astype(BF16))
            q_off = OFF_AQ + g * SWA_GROUP * HEAD_DIM
            qt0 = proj_s[rows, q_off:q_off + V7X_LANES]
            qt1 = proj_s[rows, q_off + V7X_LANES:q_off + 2 * V7X_LANES]
            q_even = jnp.concatenate([jnp.where(low, qt0, 0.0), jnp.where(low, qt1, 0.0)], axis=0)
            q_odd = jnp.concatenate([jnp.where(low, 0.0, qt0), jnp.where(low, 0.0, qt1)], axis=0)
            s_even = lax.dot_general(q_even.astype(BF16), k_even.astype(BF16), NT_DIMS,
                                     preferred_element_type=F32)
            s_odd = lax.dot_general(q_odd.astype(BF16), k_odd.astype(BF16), NT_DIMS,
                                    preferred_element_type=F32)
            s_att.append((s_even[:CHUNK], s_odd[:CHUNK]))
            s_att.append((s_even[CHUNK:], s_odd[CHUNK:]))

        ys = []
        for p in range(RET_HEADS // 2):
            s_a = (s_ret[p][:CHUNK] * dec_ref[2 * p]).astype(BF16)
            s_b = (s_ret[p][CHUNK:] * dec_ref[2 * p + 1]).astype(BF16)
            st = state_s[p]
            lhs = jnp.concatenate([s_a, s_b, (q2s[p] * qw_ref[p]).astype(BF16)], axis=1)
            rhs = jnp.concatenate([jnp.where(low, v2s[p], 0.0), jnp.where(low, 0.0, v2s[p]), st],
                                  axis=0).astype(BF16)
            ys.append(jnp.dot(lhs, rhs, preferred_element_type=F32))
            state_s[p] = st * cd_ref[p] + jnp.where(same_head, kvs[p], 0.0)

        def soft(s, h):
            s = jnp.where(valid, s * scale, NEG_INF)
            sink = sink_ref[h]
            m = jnp.maximum(jnp.max(s, axis=-1, keepdims=True), sink)
            e = jnp.exp(s - m).astype(BF16)
            den = jnp.dot(e, row_sum, preferred_element_type=F32) + jnp.exp(sink - m)
            return e, den

        es, dens = [], []
        for t, (sa, sb) in enumerate(s_att):
            e_a, den_a = soft(sa, 2 * t)
            e_b, den_b = soft(sb, 2 * t + 1)
            es.append(jnp.concatenate([e_a, e_b], axis=1))
            dens.append(jnp.where(low, den_a, den_b))

        mus = [_head_mean(y, head_avg) for y in ys]
        for t in range(SWA_Q_HEADS // 2):
            o = jnp.dot(es[t], v_cats[t // (SWA_GROUP // 2)], preferred_element_type=F32) / dens[t]
            mix_s[rows, lanes(RET_WIDTH, t)] = o * asc_ref[:, lanes(0, t)]

        ds = [y - mu for y, mu in zip(ys, mus)]
        variances = [_head_mean(d * d, head_avg) for d in ds]
        for p in range(RET_HEADS // 2):
            yn = ds[p] * lax.rsqrt(variances[p] + GN_EPS)
            gate = proj_s[rows, lanes(OFF_RG, p)]
            mix_s[rows, lanes(0, p)] = _silu(gate) * (yn * gn_ref[:, lanes(0, p)])
        return carry

    lax.fori_loop(0, MIX_CHUNKS, chunk_body, 0)

    y = jnp.dot(mix_s[...].astype(BF16), wout_ref[...], preferred_element_type=F32)
    y = y + DEEPNORM_ALPHA * x_ref[...]
    o_ref[...] = _layer_norm(y, g1_ref[...], b1_ref[...])


def _rope_tables(seq, rot_dim, theta):
    half = rot_dim // 2
    inv_freq = theta ** (-jnp.arange(half, dtype=F32) / half)
    ang = jnp.arange(seq, dtype=F32)[:, None] * inv_freq[None, :]
    cos, sin = jnp.cos(ang), jnp.sin(ang)
    rest = HEAD_DIM - rot_dim
    cos_h = jnp.concatenate([cos, cos, jnp.ones((seq, rest), F32)], axis=1)
    sin_h = jnp.concatenate([-sin, sin, jnp.zeros((seq, rest), F32)], axis=1)
    reps = V7X_LANES // HEAD_DIM
    return jnp.tile(cos_h, (1, reps)), jnp.tile(sin_h, (1, reps))


def _retention_tables():
    h = RET_HEADS
    log_g = jnp.log1p(-(2.0 ** (-5.0 - jnp.arange(h, dtype=F32))))
    idx = jnp.arange(CHUNK, dtype=F32)
    diff = idx[:, None] - idx[None, :]
    intra = jnp.where(diff[None] >= 0, jnp.exp(jnp.maximum(diff, 0.0)[None] * log_g[:, None, None]), 0.0)
    k_w = jnp.exp((CHUNK - 1 - idx)[:, None] * log_g[None, :])
    q_w = jnp.exp((idx + 1)[:, None] * log_g[None, :])
    chunk_decay = jnp.exp(CHUNK * log_g)
    pairs = lambda t: jnp.repeat(t.T.reshape(h // 2, 2, -1), HEAD_DIM, axis=1).transpose(0, 2, 1)
    return intra, pairs(k_w), pairs(q_w), pairs(chunk_decay[None, :])


def _mixer(x, w_in, w_out, gn_gain, attn_scale, sinks, ln_g, ln_b):
    b, s, d = x.shape
    cr, sr = _rope_tables(s, HEAD_DIM, RET_THETA)
    ca, sa = _rope_tables(s, SWA_ROT_DIM, SWA_THETA)
    dec, kw, qw, cd = _retention_tables()
    const = lambda shape: pl.BlockSpec(shape, lambda bi, i: (0,) * len(shape))
    tab = pl.BlockSpec((MIX_TILE, V7X_LANES), lambda bi, i: (i, 0))
    smem = pl.BlockSpec(memory_space=pltpu.SMEM)
    return pl.pallas_call(
        _mixer_kernel,
        name="mixer",
        grid=(b, s // MIX_TILE),
        in_specs=[
            pl.BlockSpec((None, MIX_TILE, d), lambda bi, i: (bi, i, 0)),
            const((d, IN_WIDTH)), const((MIX_WIDTH, d)),
            tab, tab, tab, tab,
            const((RET_HEADS, CHUNK, CHUNK)), const((RET_HEADS // 2, CHUNK, V7X_LANES)),
            const((RET_HEADS // 2, CHUNK, V7X_LANES)), const((RET_HEADS // 2, 1, V7X_LANES)),
            const((1, RET_WIDTH)), const((1, SWA_WIDTH)), smem,
            const((1, d)), const((1, d)),
        ],
        out_specs=pl.BlockSpec((None, MIX_TILE, d), lambda bi, i: (bi, i, 0)),
        out_shape=jax.ShapeDtypeStruct((b, s, d), F32),
        scratch_shapes=[
            pltpu.VMEM((MIX_TILE, IN_WIDTH), F32),
            pltpu.VMEM((MIX_TILE, MIX_WIDTH), F32),
            pltpu.VMEM((RET_HEADS // 2, V7X_LANES, V7X_LANES), F32),
            pltpu.VMEM((CHUNK, KV_WIDTH), F32),
            pltpu.VMEM((CHUNK, KV_WIDTH), F32),
        ],
        compiler_params=pltpu.CompilerParams(
            dimension_semantics=("parallel", "arbitrary"),
            vmem_limit_bytes=V7X_VMEM_LIMIT_BYTES),
    )(x, w_in.astype(BF16), w_out.astype(BF16), cr, sr, ca, sa, dec, kw, qw, cd,
      gn_gain.reshape(1, -1), attn_scale.reshape(1, -1), sinks, ln_g.reshape(1, -1), ln_b.reshape(1, -1))


def _router_kernel(x_ref, wr_ref, bias_ref, tri_ref, wt_ref, rk_ref, cnt_ref):
    tt = x_ref.shape[0]
    x = x_ref[...]
    xh = x.astype(BF16)
    xl = (x - xh.astype(F32)).astype(BF16)
    w = wr_ref[...]
    wh = w.astype(BF16)
    wl = (w - wh.astype(F32)).astype(BF16)
    dg = functools.partial(lax.dot_general, dimension_numbers=NT_DIMS, preferred_element_type=F32)
    logits = dg(wh, xh) + (dg(wh, xl) + dg(wl, xh))
    scores = jax.nn.sigmoid(logits)
    biased = scores + bias_ref[...]

    shape3 = (N_GROUPS, GROUP_SIZE, tt)
    grp = lax.broadcasted_iota(jnp.int32, shape3, 0)
    mem = lax.broadcasted_iota(jnp.int32, shape3, 1)
    b3 = biased.reshape(shape3)

    m1 = jnp.max(b3, axis=1, keepdims=True)
    i1 = jnp.min(jnp.where(b3 == m1, mem, GROUP_SIZE), axis=1, keepdims=True)
    m2 = jnp.max(jnp.where(mem == i1, -jnp.inf, b3), axis=1, keepdims=True)
    gs = jnp.broadcast_to(m1 + m2, shape3)

    gmask = jnp.zeros(shape3, F32)
    for _ in range(TOPK_GROUPS):
        m = jnp.max(gs, axis=0, keepdims=True)
        pick = grp == jnp.min(jnp.where(gs == m, grp, N_GROUPS), axis=0, keepdims=True)
        gmask = jnp.where(pick, 1.0, gmask)
        gs = jnp.where(pick, -jnp.inf, gs)

    eid = grp * GROUP_SIZE + mem
    cand = jnp.where(gmask > 0.0, b3, NEG_INF)
    sel = jnp.zeros(shape3, F32)
    for _ in range(TOP_K):
        m = jnp.max(jnp.max(cand, axis=0, keepdims=True), axis=1, keepdims=True)
        first = jnp.where(cand == m, eid, N_EXPERTS)
        first = jnp.min(jnp.min(first, axis=0, keepdims=True), axis=1, keepdims=True)
        pick = eid == first
        sel = jnp.where(pick, 1.0, sel)
        cand = jnp.where(pick, -jnp.inf, cand)

    sel = sel.reshape(N_EXPERTS, tt)
    wsel = scores * sel
    comb = wsel / jnp.sum(wsel, axis=0, keepdims=True) * ROUTED_SCALE
    wt_ref[...] = comb

    lane = lax.broadcasted_iota(jnp.int32, (N_EXPERTS, V7X_LANES), 1)
    cnt = jnp.zeros((N_EXPERTS, V7X_LANES), F32)
    for sb in range(tt // ROUTE_SUB):
        cols = slice(sb * ROUTE_SUB, (sb + 1) * ROUTE_SUB)
        sel_sb = sel[:, cols]
        rank = jnp.dot(sel_sb.astype(BF16), tri_ref[...], preferred_element_type=F32)
        rk_ref[:, cols] = jnp.where(sel_sb > 0.0, rank, -1.0)
        cnt = jnp.where(lane == sb, jnp.sum(sel_sb, axis=1, keepdims=True), cnt)
    cnt_ref[...] = cnt


def _router(x1, w_router, router_bias):
    t, d = x1.shape
    n_steps = t // ROUTE_TILE
    tri = (jnp.arange(ROUTE_SUB)[:, None] < jnp.arange(ROUTE_SUB)[None, :]).astype(BF16)
    return pl.pallas_call(
        _router_kernel,
        name="router",
        grid=(n_steps,),
        in_specs=[
            pl.BlockSpec((ROUTE_TILE, d), lambda i: (i, 0)),
            pl.BlockSpec((N_EXPERTS, d), lambda i: (0, 0)),
            pl.BlockSpec((N_EXPERTS, 1), lambda i: (0, 0)),
            pl.BlockSpec((ROUTE_SUB, ROUTE_SUB), lambda i: (0, 0)),
        ],
        out_specs=[
            pl.BlockSpec((N_EXPERTS, ROUTE_TILE), lambda i: (0, i)),
            pl.BlockSpec((N_EXPERTS, ROUTE_TILE), lambda i: (0, i)),
            pl.BlockSpec((None, N_EXPERTS, V7X_LANES), lambda i: (i, 0, 0)),
        ],
        out_shape=[
            jax.ShapeDtypeStruct((N_EXPERTS, t), F32),
            jax.ShapeDtypeStruct((N_EXPERTS, t), F32),
            jax.ShapeDtypeStruct((n_steps, N_EXPERTS, V7X_LANES), F32),
        ],
        compiler_params=pltpu.CompilerParams(
            dimension_semantics=("parallel",),
            vmem_limit_bytes=V7X_VMEM_LIMIT_BYTES),
    )(x1, w_router.T, router_bias.reshape(-1, 1), tri)


def _slot_onehot(rk_ref, group, values):
    rid = lax.broadcasted_iota(jnp.int32, (SLOT_ROWS, ROUTE_SUB), 0).astype(F32).astype(BF16)
    pieces = []
    for i in range(SLOT_GROUP):
        e = group * SLOT_GROUP + i
        rank = rk_ref[e:e + 1, :].astype(BF16)
        pieces.append(jnp.where(rid == rank, values(e), jnp.zeros((), BF16)))
    return jnp.concatenate(pieces, axis=0)


def _dispatch_kernel(x_ref, rk_ref, xs_ref):
    xb = x_ref[...].astype(BF16)
    one = jnp.ones((), BF16)
    for g in range(N_EXPERTS // SLOT_GROUP):
        onehot = _slot_onehot(rk_ref, g, lambda e: one)
        rows = jnp.dot(onehot, xb, preferred_element_type=F32).astype(BF16)
        xs_ref[g * SLOT_GROUP:(g + 1) * SLOT_GROUP] = rows.reshape(SLOT_GROUP, SLOT_ROWS, rows.shape[1])


def _dispatch(x1, rk):
    t, d = x1.shape
    n_sub = t // ROUTE_SUB
    return pl.pallas_call(
        _dispatch_kernel,
        name="dispatch",
        grid=(n_sub,),
        in_specs=[
            pl.BlockSpec((ROUTE_SUB, d), lambda j: (j, 0)),
            pl.BlockSpec((N_EXPERTS, ROUTE_SUB), lambda j: (0, j)),
        ],
        out_specs=pl.BlockSpec((N_EXPERTS, SLOT_ROWS, d), lambda j: (0, j, 0)),
        out_shape=jax.ShapeDtypeStruct((N_EXPERTS, n_sub * SLOT_ROWS, d), BF16),
        compiler_params=pltpu.CompilerParams(
            dimension_semantics=("parallel",),
            vmem_limit_bytes=V7X_VMEM_LIMIT_BYTES),
    )(x1, rk)


def _expert_ffn(x, w1, w3, w2):
    h = _silu(jnp.dot(x, w1, preferred_element_type=F32))
    h = h * jnp.dot(x, w3, preferred_element_type=F32)
    return jnp.dot(h.astype(BF16), w2, preferred_element_type=F32).astype(BF16)


def _expert_kernel(xs_ref, w1_ref, w3_ref, w2_ref, ys_ref, w1_s, w3_s, w2_s):
    @pl.when(pl.program_id(1) == 0)
    def _():
        w1_s[...] = w1_ref[...].astype(BF16)
        w3_s[...] = w3_ref[...].astype(BF16)
        w2_s[...] = w2_ref[...].astype(BF16)

    block = xs_ref.shape[0]
    tile = EXPERT_TILE if block % EXPERT_TILE == 0 else block
    for q in range(block // tile):
        rows = slice(q * tile, (q + 1) * tile)
        ys_ref[rows, :] = _expert_ffn(xs_ref[rows, :], w1_s[...], w3_s[...], w2_s[...])


def _experts(xs, w1, w3, w2):
    n_e, seg, d = xs.shape
    block = EXPERT_BLOCK if seg % EXPERT_BLOCK == 0 else seg
    row_spec = pl.BlockSpec((None, block, d), lambda e, t: (e, t, 0))
    return pl.pallas_call(
        _expert_kernel,
        name="experts",
        grid=(n_e, seg // block),
        in_specs=[
            row_spec,
            pl.BlockSpec((None, d, EXPERT_DIM), lambda e, t: (e, 0, 0)),
            pl.BlockSpec((None, d, EXPERT_DIM), lambda e, t: (e, 0, 0)),
            pl.BlockSpec((None, EXPERT_DIM, d), lambda e, t: (e, 0, 0)),
        ],
        out_specs=row_spec,
        out_shape=jax.ShapeDtypeStruct(xs.shape, BF16),
        scratch_shapes=[
            pltpu.VMEM((d, EXPERT_DIM), BF16),
            pltpu.VMEM((d, EXPERT_DIM), BF16),
            pltpu.VMEM((EXPERT_DIM, d), BF16),
        ],
        compiler_params=pltpu.CompilerParams(
            dimension_semantics=("parallel", "arbitrary"),
            vmem_limit_bytes=V7X_VMEM_LIMIT_BYTES),
    )(xs, w1, w3, w2)


def _combine_kernel(ovn_ref, ovl_ref, x_ref, p_ref, rk_ref, wt_ref, ys_ref,
                    ws1_ref, ws3_ref, ws2_ref, wg_ref, bg_ref, wp_ref, g2_ref, b2_ref,
                    w1_hbm, w3_hbm, w2_hbm, o_ref, acc_s, w1_s, w3_s, w2_s, sems):
    j = pl.program_id(0)

    x = x_ref[...]
    xb = x.astype(BF16)
    hs = _silu(jnp.dot(xb, ws1_ref[...], preferred_element_type=F32))
    hs = hs * jnp.dot(xb, ws3_ref[...], preferred_element_type=F32)
    shared = jnp.dot(hs.astype(BF16), ws2_ref[...], preferred_element_type=F32)
    gate = jax.nn.sigmoid(jnp.dot(xb, wg_ref[...], preferred_element_type=F32) + bg_ref[...])
    side = jnp.dot(p_ref[...].astype(BF16), wp_ref[...], preferred_element_type=F32)
    acc_s[...] = DEEPNORM_ALPHA * x + shared + gate * side

    for g in range(N_EXPERTS // SLOT_GROUP):
        scat = _slot_onehot(rk_ref, g, lambda e: wt_ref[e:e + 1, :].astype(BF16))
        rows = ys_ref[g * SLOT_GROUP:(g + 1) * SLOT_GROUP].reshape(SLOT_GROUP * SLOT_ROWS, ys_ref.shape[2])
        acc_s[...] += lax.dot_general(scat, rows, TN_DIMS, preferred_element_type=F32)

    sub = lax.broadcasted_iota(jnp.int32, (ROUTE_SUB, ROUTE_SUB), 0)
    lan = lax.broadcasted_iota(jnp.int32, (ROUTE_SUB, ROUTE_SUB), 1)

    def overflow(k, carry):
        e = ovl_ref[j * N_EXPERTS + k]
        copies = [pltpu.make_async_copy(src.at[e], dst, sems.at[i])
                  for i, (src, dst) in enumerate(((w1_hbm, w1_s), (w3_hbm, w3_s), (w2_hbm, w2_s)))]
        for cp in copies:
            cp.start()
        rank = rk_ref[pl.ds(e, 1), :]
        late = jnp.where(rank >= SLOT_ROWS, wt_ref[pl.ds(e, 1), :], 0.0)
        pick = jnp.where(sub == lan, jnp.broadcast_to(late, (ROUTE_SUB, ROUTE_SUB)), 0.0).astype(BF16)
        for cp in copies:
            cp.wait()
        y = _expert_ffn(x_ref[...].astype(BF16), w1_s[...].astype(BF16), w3_s[...].astype(BF16),
                        w2_s[...].astype(BF16))
        acc_s[...] += jnp.dot(pick, y, preferred_element_type=F32)
        return carry

    lax.fori_loop(0, ovn_ref[j], overflow, 0)
    o_ref[...] = _layer_norm(acc_s[...], g2_ref[...], b2_ref[...])


def _combine(x1, p, rk, wt, ys, ov_n, ov_list, w1, w3, w2, ws1, ws3, ws2, wg, bg, wp, ln_g, ln_b):
    t, d = x1.shape
    n_sub = t // ROUTE_SUB
    const = lambda shape: pl.BlockSpec(shape, lambda j, *_: (0,) * len(shape))
    hbm = pl.BlockSpec(memory_space=pl.ANY)
    return pl.pallas_call(
        _combine_kernel,
        name="combine",
        grid_spec=pltpu.PrefetchScalarGridSpec(
            num_scalar_prefetch=2,
            grid=(n_sub,),
            in_specs=[
                pl.BlockSpec((ROUTE_SUB, d), lambda j, *_: (j, 0)),
                pl.BlockSpec((ROUTE_SUB, PLE_DIM), lambda j, *_: (j, 0)),
                pl.BlockSpec((N_EXPERTS, ROUTE_SUB), lambda j, *_: (0, j)),
                pl.BlockSpec((N_EXPERTS, ROUTE_SUB), lambda j, *_: (0, j)),
                pl.BlockSpec((N_EXPERTS, SLOT_ROWS, d), lambda j, *_: (0, j, 0)),
                const((d, SHARED_DIM)), const((d, SHARED_DIM)), const((SHARED_DIM, d)),
                const((d, d)), const((1, d)), const((PLE_DIM, d)), const((1, d)), const((1, d)),
                hbm, hbm, hbm,
            ],
            out_specs=pl.BlockSpec((ROUTE_SUB, d), lambda j, *_: (j, 0)),
            scratch_shapes=[
                pltpu.VMEM((ROUTE_SUB, d), F32),
                pltpu.VMEM((d, EXPERT_DIM), F32),
                pltpu.VMEM((d, EXPERT_DIM), F32),
                pltpu.VMEM((EXPERT_DIM, d), F32),
                pltpu.SemaphoreType.DMA((3,)),
            ],
        ),
        out_shape=jax.ShapeDtypeStruct((t, d), F32),
        compiler_params=pltpu.CompilerParams(
            dimension_semantics=("arbitrary",),
            vmem_limit_bytes=V7X_VMEM_LIMIT_BYTES),
    )(ov_n, ov_list, x1, p, rk, wt, ys,
      ws1.astype(BF16), ws3.astype(BF16), ws2.astype(BF16),
      wg.astype(BF16), bg.reshape(1, -1), wp.astype(BF16), ln_g.reshape(1, -1), ln_b.reshape(1, -1),
      w1, w3, w2)


def _overflow_lists(cnt):
    over = cnt > SLOT_ROWS
    place = jnp.cumsum(over, axis=1) - 1
    slot = jnp.arange(N_EXPERTS)
    hit = over[:, :, None] & (place[:, :, None] == slot[None, None, :])
    ov_list = jnp.sum(jnp.where(hit, slot[None, :, None], 0), axis=1)
    return jnp.sum(over, axis=1).astype(jnp.int32), ov_list.reshape(-1).astype(jnp.int32)


def _moe_block(x1, p, w_router, router_bias, w1, w3, w2, ws1, ws3, ws2, wg, bg, wp, ln_g, ln_b):
    t, d = x1.shape
    n_sub = t // ROUTE_SUB
    wt, rk, cnt = _router(x1, w_router, router_bias)
    per_step = ROUTE_TILE // ROUTE_SUB
    cnt = jnp.swapaxes(cnt[:, :, :per_step], 1, 2).reshape(n_sub, N_EXPERTS).astype(jnp.int32)
    ov_n, ov_list = _overflow_lists(cnt)
    xs = _dispatch(x1, rk)
    ys = _experts(xs, w1, w3, w2)
    return _combine(x1, p, rk, wt, ys, ov_n, ov_list, w1, w3, w2, ws1, ws3, ws2, wg, bg, wp, ln_g, ln_b)


def kernel(x, p, w_in, ret_gn_gain, attn_scale, sinks, w_out, ln1_g, ln1_b, w_router, router_bias,
           w1, w3, w2, ws1, ws3, ws2, w_ple_gate, b_ple_gate, w_ple_proj, ln2_g, ln2_b):
    b, s, d = x.shape
    h = x
    for i in range(DEPTH):
        h = _mixer(h, w_in[i], w_out[i], ret_gn_gain[i], attn_scale[i], sinks[i], ln1_g[i], ln1_b[i])
        h2 = _moe_block(h.reshape(b * s, d), p[i].reshape(b * s, -1), w_router[i], router_bias[i],
                        w1[i], w3[i], w2[i], ws1[i], ws3[i], ws2[i],
                        w_ple_gate[i], b_ple_gate[i], w_ple_proj[i], ln2_g[i], ln2_b[i])
        h = h2.reshape(b, s, d)
    return h
```

```python
import functools

import jax
import jax.numpy as jnp
from jax import lax
from jax.experimental import pallas as pl
from jax.experimental.pallas import tpu as pltpu

F32 = jnp.float32
BF16 = jnp.bfloat16

D_MODEL = 1024
HEAD_DIM = 64
RET_HEADS = 8
SWA_Q_HEADS = 8
SWA_KV_HEADS = 2
SWA_GROUP = SWA_Q_HEADS // SWA_KV_HEADS
RET_WIDTH = RET_HEADS * HEAD_DIM
SWA_WIDTH = SWA_Q_HEADS * HEAD_DIM
KV_WIDTH = SWA_KV_HEADS * HEAD_DIM
MIX_WIDTH = RET_WIDTH + SWA_WIDTH
IN_WIDTH = 4 * RET_WIDTH + SWA_WIDTH + 2 * KV_WIDTH
CHUNK = 128
WINDOW = 128
RET_THETA = 10000.0
SWA_THETA = 500000.0
SWA_ROT_DIM = HEAD_DIM // 4
N_EXPERTS = 64
TOP_K = 8
N_GROUPS = 8
GROUP_SIZE = N_EXPERTS // N_GROUPS
TOPK_GROUPS = 4
EXPERT_DIM = 256
SHARED_DIM = 256
ROUTED_SCALE = 2.5
PLE_DIM = 256
LN_EPS = 1e-5
GN_EPS = 1e-6
NEG_INF = -1e30
DEPTH = 1
DEEPNORM_ALPHA = (2.0 * DEPTH) ** 0.25

OFF_RQ = 0
OFF_RK = RET_WIDTH
OFF_RV = 2 * RET_WIDTH
OFF_RG = 3 * RET_WIDTH
OFF_AQ = 4 * RET_WIDTH
OFF_AK = OFF_AQ + SWA_WIDTH
OFF_AV = OFF_AK + KV_WIDTH

V7X_LANES = 128
V7X_VMEM_LIMIT_BYTES = 56 * 1024 * 1024

MIX_CHUNKS = 4
MIX_TILE = MIX_CHUNKS * CHUNK
ROUTE_SUB = 256
ROUTE_TILE = 512
SLOT_ROWS = 48
SLOT_GROUP = 16
EXPERT_TILE = 512
EXPERT_BLOCK = 3072

NT_DIMS = (((1,), (1,)), ((), ()))
TN_DIMS = (((0,), (0,)), ((), ()))


def _layer_norm(y, g, b):
    mu = jnp.mean(y, axis=-1, keepdims=True)
    d = y - mu
    var = jnp.mean(d * d, axis=-1, keepdims=True)
    return d * lax.rsqrt(var + LN_EPS) * g + b


def _silu(v):
    return v * jax.nn.sigmoid(v)


def _rope_inplace(buf, col0, n_groups, cos, sin, near, shift, scale):
    for g in range(n_groups):
        cols = slice(col0 + g * V7X_LANES, col0 + (g + 1) * V7X_LANES)
        xg = buf[:, cols]
        partner = jnp.where(near, pltpu.roll(xg, V7X_LANES - shift, 1), pltpu.roll(xg, shift, 1))
        y = xg * cos + partner * sin
        if scale != 1.0:
            y = y * scale
        buf[:, cols] = y


def _head_mean(y, head_avg):
    hi = y.astype(BF16)
    lo = (y - hi.astype(F32)).astype(BF16)
    return (jnp.dot(hi, head_avg, preferred_element_type=F32)
            + jnp.dot(lo, head_avg, preferred_element_type=F32))


def _mixer_kernel(x_ref, win_ref, wout_ref, cr_ref, sr_ref, ca_ref, sa_ref,
                  dec_ref, kw_ref, qw_ref, cd_ref, gn_ref, asc_ref, sink_ref,
                  g1_ref, b1_ref, o_ref, proj_s, mix_s, state_s, kprev_s, vprev_s):
    step = pl.program_id(1)

    @pl.when(step == 0)
    def _():
        state_s[...] = jnp.zeros_like(state_s)
        kprev_s[...] = jnp.zeros_like(kprev_s)
        vprev_s[...] = jnp.zeros_like(vprev_s)

    proj_s[...] = jnp.dot(x_ref[...].astype(BF16), win_ref[...], preferred_element_type=F32)

    lane = lax.broadcasted_iota(jnp.int32, (MIX_TILE, V7X_LANES), 1) % HEAD_DIM
    scale = HEAD_DIM ** -0.5
    cr, sr = cr_ref[...], sr_ref[...]
    _rope_inplace(proj_s, OFF_RQ, RET_WIDTH // V7X_LANES, cr, sr, lane < HEAD_DIM // 2, HEAD_DIM // 2, 1.0)
    _rope_inplace(proj_s, OFF_RK, RET_WIDTH // V7X_LANES, cr, sr, lane < HEAD_DIM // 2, HEAD_DIM // 2, scale)
    ca, sa = ca_ref[...], sa_ref[...]
    _rope_inplace(proj_s, OFF_AQ, SWA_WIDTH // V7X_LANES, ca, sa, lane < SWA_ROT_DIM // 2, SWA_ROT_DIM // 2, 1.0)
    _rope_inplace(proj_s, OFF_AK, KV_WIDTH // V7X_LANES, ca, sa, lane < SWA_ROT_DIM // 2, SWA_ROT_DIM // 2, 1.0)

    qi = lax.broadcasted_iota(jnp.int32, (CHUNK, 2 * CHUNK), 0)
    kj = lax.broadcasted_iota(jnp.int32, (CHUNK, 2 * CHUNK), 1)
    rel = CHUNK + qi - kj
    in_window = (rel >= 0) & (rel < WINDOW)

    sub = lax.broadcasted_iota(jnp.int32, (CHUNK, V7X_LANES), 0)
    lan = lax.broadcasted_iota(jnp.int32, (CHUNK, V7X_LANES), 1)
    low = lan < HEAD_DIM
    same_head = (sub // HEAD_DIM) == (lan // HEAD_DIM)
    head_avg = jnp.where(same_head, 1.0 / HEAD_DIM, 0.0).astype(BF16)
    row_sum = jnp.ones((2 * CHUNK, V7X_LANES), BF16)

    def chunk_body(c, carry):
        rows = pl.ds(pl.multiple_of(c * CHUNK, CHUNK), CHUNK)

        first = jnp.logical_and(step == 0, c == 0)
        valid = in_window & (kj >= jnp.where(first, CHUNK, 0))
        lanes = lambda off, p: slice(off + p * V7X_LANES, off + (p + 1) * V7X_LANES)

        q2s, v2s, s_ret, kvs = [], [], [], []
        for p in range(RET_HEADS // 2):
            q2 = proj_s[rows, lanes(OFF_RQ, p)]
            k2 = proj_s[rows, lanes(OFF_RK, p)]
            v2 = proj_s[rows, lanes(OFF_RV, p)]
            q_ab = jnp.concatenate([jnp.where(low, q2, 0.0), jnp.where(low, 0.0, q2)], axis=0).astype(BF16)
            s_ret.append(lax.dot_general(q_ab, k2.astype(BF16), NT_DIMS, preferred_element_type=F32))
            kvs.append(lax.dot_general((k2 * kw_ref[p]).astype(BF16), v2.astype(BF16), TN_DIMS,
                                       preferred_element_type=F32))
            q2s.append(q2)
            v2s.append(v2)

        kc = proj_s[rows, OFF_AK:OFF_AK + KV_WIDTH]
        vc = proj_s[rows, OFF_AV:OFF_AV + KV_WIDTH]
        kband = jnp.concatenate([kprev_s[...], kc], axis=0)
        vband = jnp.concatenate([vprev_s[...], vc], axis=0)
        kprev_s[...] = kc
        vprev_s[...] = vc
        kband_r = pltpu.roll(kband, HEAD_DIM, 1)
        vband_r = pltpu.roll(vband, HEAD_DIM, 1)
        low2 = jnp.concatenate([low, low], axis=0)
        s_att, v_cats = [], []
        for g in range(SWA_KV_HEADS):
            k_even, k_odd = (kband, kband_r) if g == 0 else (kband_r, kband)
            v_low, v_high = (vband, vband_r) if g == 0 else (vband_r, vband)
            v_cats.append(jnp.concatenate([jnp.where(low2, v_low, 0.0), jnp.where(low2, 0.0, v_high)],
                                          axis=0).astype(BF16))
            q_off = OFF_AQ + g * SWA_GROUP * HEAD_DIM
            qt0 = proj_s[rows, q_off:q_off + V7X_LANES]
            qt1 = proj_s[rows, q_off + V7X_LANES:q_off + 2 * V7X_LANES]
            q_even = jnp.concatenate([jnp.where(low, qt0, 0.0), jnp.where(low, qt1, 0.0)], axis=0)
            q_odd = jnp.concatenate([jnp.where(low, 0.0, qt0), jnp.where(low, 0.0, qt1)], axis=0)
            s_even = lax.dot_general(q_even.astype(BF16), k_even.astype(BF16), NT_DIMS,
                                     preferred_element_type=F32)
            s_odd = lax.dot_general(q_odd.astype(BF16), k_odd.astype(BF16), NT_DIMS,
                                    preferred_element_type=F32)
            s_att.append((s_even[:CHUNK], s_odd[:CHUNK]))
            s_att.append((s_even[CHUNK:], s_odd[CHUNK:]))

        ys = []
        for p in range(RET_HEADS // 2):
            s_a = (s_ret[p][:CHUNK] * dec_ref[2 * p]).astype(BF16)
            s_b = (s_ret[p][CHUNK:] * dec_ref[2 * p + 1]).astype(BF16)
            st = state_s[p]
            lhs = jnp.concatenate([s_a, s_b, (q2s[p] * qw_ref[p]).astype(BF16)], axis=1)
            rhs = jnp.concatenate([jnp.where(low, v2s[p], 0.0), jnp.where(low, 0.0, v2s[p]), st],
                                  axis=0).astype(BF16)
            ys.append(jnp.dot(lhs, rhs, preferred_element_type=F32))
            state_s[p] = st * cd_ref[p] + jnp.where(same_head, kvs[p], 0.0)

        def soft(s, h):
            s = jnp.where(valid, s * scale, NEG_INF)
            sink = sink_ref[h]
            m = jnp.maximum(jnp.max(s, axis=-1, keepdims=True), sink)
            e = jnp.exp(s - m).astype(BF16)
            den = jnp.dot(e, row_sum, preferred_element_type=F32) + jnp.exp(sink - m)
            return e, den

        es, dens = [], []
        for t, (sa, sb) in enumerate(s_att):
            e_a, den_a = soft(sa, 2 * t)
            e_b, den_b = soft(sb, 2 * t + 1)
            es.append(jnp.concatenate([e_a, e_b], axis=1))
            dens.append(jnp.where(low, den_a, den_b))

        mus = [_head_mean(y, head_avg) for y in ys]
        for t in range(SWA_Q_HEADS // 2):
            o = jnp.dot(es[t], v_cats[t // (SWA_GROUP // 2)], preferred_element_type=F32) / dens[t]
            mix_s[rows, lanes(RET_WIDTH, t)] = o * asc_ref[:, lanes(0, t)]

        ds = [y - mu for y, mu in zip(ys, mus)]
        variances = [_head_mean(d * d, head_avg) for d in ds]
        for p in range(RET_HEADS // 2):
            yn = ds[p] * lax.rsqrt(variances[p] + GN_EPS)
            gate = proj_s[rows, lanes(OFF_RG, p)]
            mix_s[rows, lanes(0, p)] = _silu(gate) * (yn * gn_ref[:, lanes(0, p)])
        return carry

    lax.fori_loop(0, MIX_CHUNKS, chunk_body, 0)

    y = jnp.dot(mix_s[...].astype(BF16), wout_ref[...], preferred_element_type=F32)
    y = y + DEEPNORM_ALPHA * x_ref[...]
    o_ref[...] = _layer_norm(y, g1_ref[...], b1_ref[...])


def _rope_tables(seq, rot_dim, theta):
    half = rot_dim // 2
    inv_freq = theta ** (-jnp.arange(half, dtype=F32) / half)
    ang = jnp.arange(seq, dtype=F32)[:, None] * inv_freq[None, :]
    cos, sin = jnp.cos(ang), jnp.sin(ang)
    rest = HEAD_DIM - rot_dim
    cos_h = jnp.concatenate([cos, cos, jnp.ones((seq, rest), F32)], axis=1)
    sin_h = jnp.concatenate([-sin, sin, jnp.zeros((seq, rest), F32)], axis=1)
    reps = V7X_LANES // HEAD_DIM
    return jnp.tile(cos_h, (1, reps)), jnp.tile(sin_h, (1, reps))


def _retention_tables():
    h = RET_HEADS
    log_g = jnp.log1p(-(2.0 ** (-5.0 - jnp.arange(h, dtype=F32))))
    idx = jnp.arange(CHUNK, dtype=F32)
    diff = idx[:, None] - idx[None, :]
    intra = jnp.where(diff[None] >= 0, jnp.exp(jnp.maximum(diff, 0.0)[None] * log_g[:, None, None]), 0.0)
    k_w = jnp.exp((CHUNK - 1 - idx)[:, None] * log_g[None, :])
    q_w = jnp.exp((idx + 1)[:, None] * log_g[None, :])
    chunk_decay = jnp.exp(CHUNK * log_g)
    pairs = lambda t: jnp.repeat(t.T.reshape(h // 2, 2, -1), HEAD_DIM, axis=1).transpose(0, 2, 1)
    return intra, pairs(k_w), pairs(q_w), pairs(chunk_decay[None, :])


def _mixer(x, w_in, w_out, gn_gain, attn_scale, sinks, ln_g, ln_b):
    b, s, d = x.shape
    cr, sr = _rope_tables(s, HEAD_DIM, RET_THETA)
    ca, sa = _rope_tables(s, SWA_ROT_DIM, SWA_THETA)
    dec, kw, qw, cd = _retention_tables()
    const = lambda shape: pl.BlockSpec(shape, lambda bi, i: (0,) * len(shape))
    tab = pl.BlockSpec((MIX_TILE, V7X_LANES), lambda bi, i: (i, 0))
    smem = pl.BlockSpec(memory_space=pltpu.SMEM)
    return pl.pallas_call(
        _mixer_kernel,
        name="mixer",
        grid=(b, s // MIX_TILE),
        in_specs=[
            pl.BlockSpec((None, MIX_TILE, d), lambda bi, i: (bi, i, 0)),
            const((d, IN_WIDTH)), const((MIX_WIDTH, d)),
            tab, tab, tab, tab,
            const((RET_HEADS, CHUNK, CHUNK)), const((RET_HEADS // 2, CHUNK, V7X_LANES)),
            const((RET_HEADS // 2, CHUNK, V7X_LANES)), const((RET_HEADS // 2, 1, V7X_LANES)),
            const((1, RET_WIDTH)), const((1, SWA_WIDTH)), smem,
            const((1, d)), const((1, d)),
        ],
        out_specs=pl.BlockSpec((None, MIX_TILE, d), lambda bi, i: (bi, i, 0)),
        out_shape=jax.ShapeDtypeStruct((b, s, d), F32),
        scratch_shapes=[
            pltpu.VMEM((MIX_TILE, IN_WIDTH), F32),
            pltpu.VMEM((MIX_TILE, MIX_WIDTH), F32),
            pltpu.VMEM((RET_HEADS // 2, V7X_LANES, V7X_LANES), F32),
            pltpu.VMEM((CHUNK, KV_WIDTH), F32),
            pltpu.VMEM((CHUNK, KV_WIDTH), F32),
        ],
        compiler_params=pltpu.CompilerParams(
            dimension_semantics=("parallel", "arbitrary"),
            vmem_limit_bytes=V7X_VMEM_LIMIT_BYTES),
    )(x, w_in.astype(BF16), w_out.astype(BF16), cr, sr, ca, sa, dec, kw, qw, cd,
      gn_gain.reshape(1, -1), attn_scale.reshape(1, -1), sinks, ln_g.reshape(1, -1), ln_b.reshape(1, -1))


def _router_kernel(x_ref, wr_ref, bias_ref, tri_ref, wt_ref, rk_ref, cnt_ref):
    tt = x_ref.shape[0]
    x = x_ref[...]
    xh = x.astype(BF16)
    xl = (x - xh.astype(F32)).astype(BF16)
    w = wr_ref[...]
    wh = w.astype(BF16)
    wl = (w - wh.astype(F32)).astype(BF16)
    dg = functools.partial(lax.dot_general, dimension_numbers=NT_DIMS, preferred_element_type=F32)
    logits = dg(wh, xh) + (dg(wh, xl) + dg(wl, xh))
    scores = jax.nn.sigmoid(logits)
    biased = scores + bias_ref[...]

    shape3 = (N_GROUPS, GROUP_SIZE, tt)
    grp = lax.broadcasted_iota(jnp.int32, shape3, 0)
    mem = lax.broadcasted_iota(jnp.int32, shape3, 1)
    b3 = biased.reshape(shape3)

    m1 = jnp.max(b3, axis=1, keepdims=True)
    i1 = jnp.min(jnp.where(b3 == m1, mem, GROUP_SIZE), axis=1, keepdims=True)
    m2 = jnp.max(jnp.where(mem == i1, -jnp.inf, b3), axis=1, keepdims=True)
    gs = jnp.broadcast_to(m1 + m2, shape3)

    gmask = jnp.zeros(shape3, F32)
    for _ in range(TOPK_GROUPS):
        m = jnp.max(gs, axis=0, keepdims=True)
        pick = grp == jnp.min(jnp.where(gs == m, grp, N_GROUPS), axis=0, keepdims=True)
        gmask = jnp.where(pick, 1.0, gmask)
        gs = jnp.where(pick, -jnp.inf, gs)

    eid = grp * GROUP_SIZE + mem
    cand = jnp.where(gmask > 0.0, b3, NEG_INF)
    sel = jnp.zeros(shape3, F32)
    for _ in range(TOP_K):
        m = jnp.max(jnp.max(cand, axis=0, keepdims=True), axis=1, keepdims=True)
        first = jnp.where(cand == m, eid, N_EXPERTS)
        first = jnp.min(jnp.min(first, axis=0, keepdims=True), axis=1, keepdims=True)
        pick = eid == first
        sel = jnp.where(pick, 1.0, sel)
        cand = jnp.where(pick, -jnp.inf, cand)

    sel = sel.reshape(N_EXPERTS, tt)
    wsel = scores * sel
    comb = wsel / jnp.sum(wsel, axis=0, keepdims=True) * ROUTED_SCALE
    wt_ref[...] = comb

    lane = lax.broadcasted_iota(jnp.int32, (N_EXPERTS, V7X_LANES), 1)
    cnt = jnp.zeros((N_EXPERTS, V7X_LANES), F32)
    for sb in range(tt // ROUTE_SUB):
        cols = slice(sb * ROUTE_SUB, (sb + 1) * ROUTE_SUB)
        sel_sb = sel[:, cols]
        rank = jnp.dot(sel_sb.astype(BF16), tri_ref[...], preferred_element_type=F32)
        rk_ref[:, cols] = jnp.where(sel_sb > 0.0, rank, -1.0)
        cnt = jnp.where(lane == sb, jnp.sum(sel_sb, axis=1, keepdims=True), cnt)
    cnt_ref[...] = cnt


def _router(x1, w_router, router_bias):
    t, d = x1.shape
    n_steps = t // ROUTE_TILE
    tri = (jnp.arange(ROUTE_SUB)[:, None] < jnp.arange(ROUTE_SUB)[None, :]).astype(BF16)
    return pl.pallas_call(
        _router_kernel,
        name="router",
        grid=(n_steps,),
        in_specs=[
            pl.BlockSpec((ROUTE_TILE, d), lambda i: (i, 0)),
            pl.BlockSpec((N_EXPERTS, d), lambda i: (0, 0)),
            pl.BlockSpec((N_EXPERTS, 1), lambda i: (0, 0)),
            pl.BlockSpec((ROUTE_SUB, ROUTE_SUB), lambda i: (0, 0)),
        ],
        out_specs=[
            pl.BlockSpec((N_EXPERTS, ROUTE_TILE), lambda i: (0, i)),
            pl.BlockSpec((N_EXPERTS, ROUTE_TILE), lambda i: (0, i)),
            pl.BlockSpec((None, N_EXPERTS, V7X_LANES), lambda i: (i, 0, 0)),
        ],
        out_shape=[
            jax.ShapeDtypeStruct((N_EXPERTS, t), F32),
            jax.ShapeDtypeStruct((N_EXPERTS, t), F32),
            jax.ShapeDtypeStruct((n_steps, N_EXPERTS, V7X_LANES), F32),
        ],
        compiler_params=pltpu.CompilerParams(
            dimension_semantics=("parallel",),
            vmem_limit_bytes=V7X_VMEM_LIMIT_BYTES),
    )(x1, w_router.T, router_bias.reshape(-1, 1), tri)


def _slot_onehot(rk_ref, group, values):
    rid = lax.broadcasted_iota(jnp.int32, (SLOT_ROWS, ROUTE_SUB), 0).astype(F32).astype(BF16)
    pieces = []
    for i in range(SLOT_GROUP):
        e = group * SLOT_GROUP + i
        rank = rk_ref[e:e + 1, :].astype(BF16)
        pieces.append(jnp.where(rid == rank, values(e), jnp.zeros((), BF16)))
    return jnp.concatenate(pieces, axis=0)


def _dispatch_kernel(x_ref, rk_ref, xs_ref):
    xb = x_ref[...].astype(BF16)
    one = jnp.ones((), BF16)
    for g in range(N_EXPERTS // SLOT_GROUP):
        onehot = _slot_onehot(rk_ref, g, lambda e: one)
        rows = jnp.dot(onehot, xb, preferred_element_type=F32).astype(BF16)
        xs_ref[g * SLOT_GROUP:(g + 1) * SLOT_GROUP] = rows.reshape(SLOT_GROUP, SLOT_ROWS, rows.shape[1])


def _dispatch(x1, rk):
    t, d = x1.shape
    n_sub = t // ROUTE_SUB
    return pl.pallas_call(
        _dispatch_kernel,
        name="dispatch",
        grid=(n_sub,),
        in_specs=[
            pl.BlockSpec((ROUTE_SUB, d), lambda j: (j, 0)),
            pl.BlockSpec((N_EXPERTS, ROUTE_SUB), lambda j: (0, j)),
        ],
        out_specs=pl.BlockSpec((N_EXPERTS, SLOT_ROWS, d), lambda j: (0, j, 0)),
        out_shape=jax.ShapeDtypeStruct((N_EXPERTS, n_sub * SLOT_ROWS, d), BF16),
        compiler_params=pltpu.CompilerParams(
            dimension_semantics=("parallel",),
            vmem_limit_bytes=V7X_VMEM_LIMIT_BYTES),
    )(x1, rk)


def _expert_ffn(x, w1, w3, w2):
    h = _silu(jnp.dot(x, w1, preferred_element_type=F32))
    h = h * jnp.dot(x, w3, preferred_element_type=F32)
    return jnp.dot(h.astype(BF16), w2, preferred_element_type=F32).astype(BF16)


def _expert_kernel(xs_ref, w1_ref, w3_ref, w2_ref, ys_ref, w1b_ref, w3b_ref, w2b_ref, w1_s, w3_s, w2_s):
    @pl.when(pl.program_id(1) == 0)
    def _():
        for src, dst, out in ((w1_ref, w1_s, w1b_ref), (w3_ref, w3_s, w3b_ref), (w2_ref, w2_s, w2b_ref)):
            wb = src[...].astype(BF16)
            dst[...] = wb
            out[...] = wb

    block = xs_ref.shape[0]
    tile = EXPERT_TILE if block % EXPERT_TILE == 0 else block
    for q in range(block // tile):
        rows = slice(q * tile, (q + 1) * tile)
        ys_ref[rows, :] = _expert_ffn(xs_ref[rows, :], w1_s[...], w3_s[...], w2_s[...])


def _experts(xs, w1, w3, w2):
    n_e, seg, d = xs.shape
    block = EXPERT_BLOCK if seg % EXPERT_BLOCK == 0 else seg
    row_spec = pl.BlockSpec((None, block, d), lambda e, t: (e, t, 0))
    return pl.pallas_call(
        _expert_kernel,
        name="experts",
        grid=(n_e, seg // block),
        in_specs=[
            row_spec,
            pl.BlockSpec((None, d, EXPERT_DIM), lambda e, t: (e, 0, 0)),
            pl.BlockSpec((None, d, EXPERT_DIM), lambda e, t: (e, 0, 0)),
            pl.BlockSpec((None, EXPERT_DIM, d), lambda e, t: (e, 0, 0)),
        ],
        out_specs=[
            row_spec,
            pl.BlockSpec((None, d, EXPERT_DIM), lambda e, t: (e, 0, 0)),
            pl.BlockSpec((None, d, EXPERT_DIM), lambda e, t: (e, 0, 0)),
            pl.BlockSpec((None, EXPERT_DIM, d), lambda e, t: (e, 0, 0)),
        ],
        out_shape=[
            jax.ShapeDtypeStruct(xs.shape, BF16),
            jax.ShapeDtypeStruct(w1.shape, BF16),
            jax.ShapeDtypeStruct(w3.shape, BF16),
            jax.ShapeDtypeStruct(w2.shape, BF16),
        ],
        scratch_shapes=[
            pltpu.VMEM((d, EXPERT_DIM), BF16),
            pltpu.VMEM((d, EXPERT_DIM), BF16),
            pltpu.VMEM((EXPERT_DIM, d), BF16),
        ],
        compiler_params=pltpu.CompilerParams(
            dimension_semantics=("parallel", "arbitrary"),
            vmem_limit_bytes=V7X_VMEM_LIMIT_BYTES),
    )(xs, w1, w3, w2)


def _combine_kernel(ovn_ref, ovl_ref, x_ref, p_ref, rk_ref, wt_ref, ys_ref,
                    ws1_ref, ws3_ref, ws2_ref, wg_ref, bg_ref, wp_ref, g2_ref, b2_ref,
                    w1_hbm, w3_hbm, w2_hbm, o_ref, acc_s, w1_s, w3_s, w2_s, sems):
    j = pl.program_id(0)
    n_over = ovn_ref[j]

    def fetch(k, half):
        e = ovl_ref[j * N_EXPERTS + k]
        return [pltpu.make_async_copy(src.at[e], dst.at[half], sems.at[half, i])
                for i, (src, dst) in enumerate(((w1_hbm, w1_s), (w3_hbm, w3_s), (w2_hbm, w2_s)))]

    @pl.when(n_over > 0)
    def _():
        for cp in fetch(0, 0):
            cp.start()

    x = x_ref[...]
    xb = x.astype(BF16)
    hs = _silu(jnp.dot(xb, ws1_ref[...], preferred_element_type=F32))
    hs = hs * jnp.dot(xb, ws3_ref[...], preferred_element_type=F32)
    shared = jnp.dot(hs.astype(BF16), ws2_ref[...], preferred_element_type=F32)
    gate = jax.nn.sigmoid(jnp.dot(xb, wg_ref[...], preferred_element_type=F32) + bg_ref[...])
    side = jnp.dot(p_ref[...].astype(BF16), wp_ref[...], preferred_element_type=F32)
    acc_s[...] = DEEPNORM_ALPHA * x + shared + gate * side

    for g in range(N_EXPERTS // SLOT_GROUP):
        scat = _slot_onehot(rk_ref, g, lambda e: wt_ref[e:e + 1, :].astype(BF16))
        rows = ys_ref[g * SLOT_GROUP:(g + 1) * SLOT_GROUP].reshape(SLOT_GROUP * SLOT_ROWS, ys_ref.shape[2])
        acc_s[...] += lax.dot_general(scat, rows, TN_DIMS, preferred_element_type=F32)

    sub = lax.broadcasted_iota(jnp.int32, (ROUTE_SUB, ROUTE_SUB), 0)
    lan = lax.broadcasted_iota(jnp.int32, (ROUTE_SUB, ROUTE_SUB), 1)

    def overflow(k, carry):
        half = k % 2

        @pl.when(k + 1 < n_over)
        def _():
            for cp in fetch(k + 1, 1 - half):
                cp.start()

        e = ovl_ref[j * N_EXPERTS + k]
        rank = rk_ref[pl.ds(e, 1), :]
        late = jnp.where(rank >= SLOT_ROWS, wt_ref[pl.ds(e, 1), :], 0.0)
        pick = jnp.where(sub == lan, jnp.broadcast_to(late, (ROUTE_SUB, ROUTE_SUB)), 0.0).astype(BF16)
        for cp in fetch(k, half):
            cp.wait()
        y = _expert_ffn(x_ref[...].astype(BF16), w1_s[half], w3_s[half], w2_s[half])
        acc_s[...] += jnp.dot(pick, y, preferred_element_type=F32)
        return carry

    lax.fori_loop(0, n_over, overflow, 0)
    o_ref[...] = _layer_norm(acc_s[...], g2_ref[...], b2_ref[...])


def _combine(x1, p, rk, wt, ys, ov_n, ov_list, w1, w3, w2, ws1, ws3, ws2, wg, bg, wp, ln_g, ln_b):
    t, d = x1.shape
    n_sub = t // ROUTE_SUB
    const = lambda shape: pl.BlockSpec(shape, lambda j, *_: (0,) * len(shape))
    hbm = pl.BlockSpec(memory_space=pl.ANY)
    return pl.pallas_call(
        _combine_kernel,
        name="combine",
        grid_spec=pltpu.PrefetchScalarGridSpec(
            num_scalar_prefetch=2,
            grid=(n_sub,),
            in_specs=[
                pl.BlockSpec((ROUTE_SUB, d), lambda j, *_: (j, 0)),
                pl.BlockSpec((ROUTE_SUB, PLE_DIM), lambda j, *_: (j, 0)),
                pl.BlockSpec((N_EXPERTS, ROUTE_SUB), lambda j, *_: (0, j)),
                pl.BlockSpec((N_EXPERTS, ROUTE_SUB), lambda j, *_: (0, j)),
                pl.BlockSpec((N_EXPERTS, SLOT_ROWS, d), lambda j, *_: (0, j, 0)),
                const((d, SHARED_DIM)), const((d, SHARED_DIM)), const((SHARED_DIM, d)),
                const((d, d)), const((1, d)), const((PLE_DIM, d)), const((1, d)), const((1, d)),
                hbm, hbm, hbm,
            ],
            out_specs=pl.BlockSpec((ROUTE_SUB, d), lambda j, *_: (j, 0)),
            scratch_shapes=[
                pltpu.VMEM((ROUTE_SUB, d), F32),
                pltpu.VMEM((2, d, EXPERT_DIM), BF16),
                pltpu.VMEM((2, d, EXPERT_DIM), BF16),
                pltpu.VMEM((2, EXPERT_DIM, d), BF16),
                pltpu.SemaphoreType.DMA((2, 3)),
            ],
        ),
        out_shape=jax.ShapeDtypeStruct((t, d), F32),
        compiler_params=pltpu.CompilerParams(
            dimension_semantics=("arbitrary",),
            vmem_limit_bytes=V7X_VMEM_LIMIT_BYTES),
    )(ov_n, ov_list, x1, p, rk, wt, ys,
      ws1.astype(BF16), ws3.astype(BF16), ws2.astype(BF16),
      wg.astype(BF16), bg.reshape(1, -1), wp.astype(BF16), ln_g.reshape(1, -1), ln_b.reshape(1, -1),
      w1, w3, w2)


def _overflow_lists(cnt):
    over = cnt > SLOT_ROWS
    place = jnp.cumsum(over, axis=1) - 1
    slot = jnp.arange(N_EXPERTS)
    hit = over[:, :, None] & (place[:, :, None] == slot[None, None, :])
    ov_list = jnp.sum(jnp.where(hit, slot[None, :, None], 0), axis=1)
    return jnp.sum(over, axis=1).astype(jnp.int32), ov_list.reshape(-1).astype(jnp.int32)


def _moe_block(x1, p, w_router, router_bias, w1, w3, w2, ws1, ws3, ws2, wg, bg, wp, ln_g, ln_b):
    t, d = x1.shape
    n_sub = t // ROUTE_SUB
    wt, rk, cnt = _router(x1, w_router, router_bias)
    per_step = ROUTE_TILE // ROUTE_SUB
    cnt = jnp.swapaxes(cnt[:, :, :per_step], 1, 2).reshape(n_sub, N_EXPERTS).astype(jnp.int32)
    ov_n, ov_list = _overflow_lists(cnt)
    xs = _dispatch(x1, rk)
    ys, w1b, w3b, w2b = _experts(xs, w1, w3, w2)
    return _combine(x1, p, rk, wt, ys, ov_n, ov_list, w1b, w3b, w2b, ws1, ws3, ws2, wg, bg, wp, ln_g, ln_b)


def kernel(x, p, w_in, ret_gn_gain, attn_scale, sinks, w_out, ln1_g, ln1_b, w_router, router_bias,
           w1, w3, w2, ws1, ws3, ws2, w_ple_gate, b_ple_gate, w_ple_proj, ln2_g, ln2_b):
    b, s, d = x.shape
    h = x
    for i in range(DEPTH):
        h = _mixer(h, w_in[i], w_out[i], ret_gn_gain[i], attn_scale[i], sinks[i], ln1_g[i], ln1_b[i])
        h2 = _moe_block(h.reshape(b * s, d), p[i].reshape(b * s, -1), w_router[i], router_bias[i],
                        w1[i], w3[i], w2[i], ws1[i], ws3[i], ws2[i],
                        w_ple_gate[i], b_ple_gate[i], w_ple_proj[i], ln2_g[i], ln2_b[i])
        h = h2.reshape(b, s, d)
    return h
```

```python
import functools

import jax
import jax.numpy as jnp
from jax import lax
from jax.experimental import pallas as pl
from jax.experimental.pallas import tpu as pltpu

F32 = jnp.float32
BF16 = jnp.bfloat16

D_MODEL = 1024
HEAD_DIM = 64
RET_HEADS = 8
SWA_Q_HEADS = 8
SWA_KV_HEADS = 2
SWA_GROUP = SWA_Q_HEADS // SWA_KV_HEADS
RET_WIDTH = RET_HEADS * HEAD_DIM
SWA_WIDTH = SWA_Q_HEADS * HEAD_DIM
KV_WIDTH = SWA_KV_HEADS * HEAD_DIM
MIX_WIDTH = RET_WIDTH + SWA_WIDTH
IN_WIDTH = 4 * RET_WIDTH + SWA_WIDTH + 2 * KV_WIDTH
CHUNK = 128
WINDOW = 128
RET_THETA = 10000.0
SWA_THETA = 500000.0
SWA_ROT_DIM = HEAD_DIM // 4
N_EXPERTS = 64
TOP_K = 8
N_GROUPS = 8
GROUP_SIZE = N_EXPERTS // N_GROUPS
TOPK_GROUPS = 4
EXPERT_DIM = 256
SHARED_DIM = 256
ROUTED_SCALE = 2.5
PLE_DIM = 256
LN_EPS = 1e-5
GN_EPS = 1e-6
NEG_INF = -1e30
DEPTH = 1
DEEPNORM_ALPHA = (2.0 * DEPTH) ** 0.25

OFF_RQ = 0
OFF_RK = RET_WIDTH
OFF_RV = 2 * RET_WIDTH
OFF_RG = 3 * RET_WIDTH
OFF_AQ = 4 * RET_WIDTH
OFF_AK = OFF_AQ + SWA_WIDTH
OFF_AV = OFF_AK + KV_WIDTH

V7X_LANES = 128
V7X_VMEM_LIMIT_BYTES = 56 * 1024 * 1024

MIX_CHUNKS = 4
MIX_TILE = MIX_CHUNKS * CHUNK
ROUTE_SUB = 256
ROUTE_TILE = 512
SLOT_ROWS = 48
SLOT_GROUP = 16
EXPERT_TILE = 512
EXPERT_BLOCK = 3072

NT_DIMS = (((1,), (1,)), ((), ()))
TN_DIMS = (((0,), (0,)), ((), ()))


def _layer_norm(y, g, b):
    mu = jnp.mean(y, axis=-1, keepdims=True)
    d = y - mu
    var = jnp.mean(d * d, axis=-1, keepdims=True)
    return d * lax.rsqrt(var + LN_EPS) * g + b


def _silu(v):
    return v * jax.nn.sigmoid(v)


def _rope_inplace(buf, col0, n_groups, cos, sin, near, shift, scale):
    for g in range(n_groups):
        cols = slice(col0 + g * V7X_LANES, col0 + (g + 1) * V7X_LANES)
        xg = buf[:, cols]
        partner = jnp.where(near, pltpu.roll(xg, V7X_LANES - shift, 1), pltpu.roll(xg, shift, 1))
        y = xg * cos + partner * sin
        if scale != 1.0:
            y = y * scale
        buf[:, cols] = y


def _head_mean(y, head_avg):
    hi = y.astype(BF16)
    lo = (y - hi.astype(F32)).astype(BF16)
    return (jnp.dot(hi, head_avg, preferred_element_type=F32)
            + jnp.dot(lo, head_avg, preferred_element_type=F32))


def _mixer_kernel(x_ref, win_ref, wout_ref, cr_ref, sr_ref, ca_ref, sa_ref,
                  dec_ref, kw_ref, qw_ref, cd_ref, gn_ref, asc_ref, sink_ref,
                  g1_ref, b1_ref, o_ref, proj_s, mix_s, state_s, kprev_s, vprev_s):
    step = pl.program_id(1)

    @pl.when(step == 0)
    def _():
        state_s[...] = jnp.zeros_like(state_s)
        kprev_s[...] = jnp.zeros_like(kprev_s)
        vprev_s[...] = jnp.zeros_like(vprev_s)

    proj_s[...] = jnp.dot(x_ref[...].astype(BF16), win_ref[...], preferred_element_type=F32)

    lane = lax.broadcasted_iota(jnp.int32, (MIX_TILE, V7X_LANES), 1) % HEAD_DIM
    scale = HEAD_DIM ** -0.5
    cr, sr = cr_ref[...], sr_ref[...]
    _rope_inplace(proj_s, OFF_RQ, RET_WIDTH // V7X_LANES, cr, sr, lane < HEAD_DIM // 2, HEAD_DIM // 2, 1.0)
    _rope_inplace(proj_s, OFF_RK, RET_WIDTH // V7X_LANES, cr, sr, lane < HEAD_DIM // 2, HEAD_DIM // 2, scale)
    ca, sa = ca_ref[...], sa_ref[...]
    _rope_inplace(proj_s, OFF_AQ, SWA_WIDTH // V7X_LANES, ca, sa, lane < SWA_ROT_DIM // 2, SWA_ROT_DIM // 2, 1.0)
    _rope_inplace(proj_s, OFF_AK, KV_WIDTH // V7X_LANES, ca, sa, lane < SWA_ROT_DIM // 2, SWA_ROT_DIM // 2, 1.0)

    qi = lax.broadcasted_iota(jnp.int32, (CHUNK, 2 * CHUNK), 0)
    kj = lax.broadcasted_iota(jnp.int32, (CHUNK, 2 * CHUNK), 1)
    rel = CHUNK + qi - kj
    in_window = (rel >= 0) & (rel < WINDOW)

    sub = lax.broadcasted_iota(jnp.int32, (CHUNK, V7X_LANES), 0)
    lan = lax.broadcasted_iota(jnp.int32, (CHUNK, V7X_LANES), 1)
    low = lan < HEAD_DIM
    same_head = (sub // HEAD_DIM) == (lan // HEAD_DIM)
    head_avg = jnp.where(same_head, 1.0 / HEAD_DIM, 0.0).astype(BF16)
    row_sum = jnp.ones((2 * CHUNK, V7X_LANES), BF16)

    def chunk_body(c, carry):
        rows = pl.ds(pl.multiple_of(c * CHUNK, CHUNK), CHUNK)

        first = jnp.logical_and(step == 0, c == 0)
        valid = in_window & (kj >= jnp.where(first, CHUNK, 0))
        lanes = lambda off, p: slice(off + p * V7X_LANES, off + (p + 1) * V7X_LANES)

        q2s, v2s, s_ret, kvs = [], [], [], []
        for p in range(RET_HEADS // 2):
            q2 = proj_s[rows, lanes(OFF_RQ, p)]
            k2 = proj_s[rows, lanes(OFF_RK, p)]
            v2 = proj_s[rows, lanes(OFF_RV, p)]
            q_ab = jnp.concatenate([jnp.where(low, q2, 0.0), jnp.where(low, 0.0, q2)], axis=0).astype(BF16)
            s_ret.append(lax.dot_general(q_ab, k2.astype(BF16), NT_DIMS, preferred_element_type=F32))
            kvs.append(lax.dot_general((k2 * kw_ref[p]).astype(BF16), v2.astype(BF16), TN_DIMS,
                                       preferred_element_type=F32))
            q2s.append(q2)
            v2s.append(v2)

        kc = proj_s[rows, OFF_AK:OFF_AK + KV_WIDTH]
        vc = proj_s[rows, OFF_AV:OFF_AV + KV_WIDTH]
        kband = jnp.concatenate([kprev_s[...], kc], axis=0)
        vband = jnp.concatenate([vprev_s[...], vc], axis=0)
        kprev_s[...] = kc
        vprev_s[...] = vc
        kband_r = pltpu.roll(kband, HEAD_DIM, 1)
        vband_r = pltpu.roll(vband, HEAD_DIM, 1)
        low2 = jnp.concatenate([low, low], axis=0)
        s_att, v_cats = [], []
        for g in range(SWA_KV_HEADS):
            k_even, k_odd = (kband, kband_r) if g == 0 else (kband_r, kband)
            v_low, v_high = (vband, vband_r) if g == 0 else (vband_r, vband)
            v_cats.append(jnp.concatenate([jnp.where(low2, v_low, 0.0), jnp.where(low2, 0.0, v_high)],
                                          axis=0).astype(BF16))
            q_off = OFF_AQ + g * SWA_GROUP * HEAD_DIM
            qt0 = proj_s[rows, q_off:q_off + V7X_LANES]
            qt1 = proj_s[rows, q_off + V7X_LANES:q_off + 2 * V7X_LANES]
            q_even = jnp.concatenate([jnp.where(low, qt0, 0.0), jnp.where(low, qt1, 0.0)], axis=0)
            q_odd = jnp.concatenate([jnp.where(low, 0.0, qt0), jnp.where(low, 0.0, qt1)], axis=0)
            s_even = lax.dot_general(q_even.astype(BF16), k_even.astype(BF16), NT_DIMS,
                                     preferred_element_type=F32)
            s_odd = lax.dot_general(q_odd.astype(BF16), k_odd.astype(BF16), NT_DIMS,
                                    preferred_element_type=F32)
            s_att.append((s_even[:CHUNK], s_odd[:CHUNK]))
            s_att.append((s_even[CHUNK:], s_odd[CHUNK:]))

        ys = []
        for p in range(RET_HEADS // 2):
            s_a = (s_ret[p][:CHUNK] * dec_ref[2 * p]).astype(BF16)
            s_b = (s_ret[p][CHUNK:] * dec_ref[2 * p + 1]).astype(BF16)
            st = state_s[p]
            lhs = jnp.concatenate([s_a, s_b, (q2s[p] * qw_ref[p]).astype(BF16)], axis=1)
            rhs = jnp.concatenate([jnp.where(low, v2s[p], 0.0), jnp.where(low, 0.0, v2s[p]), st],
                                  axis=0).astype(BF16)
            ys.append(jnp.dot(lhs, rhs, preferred_element_type=F32))
            state_s[p] = st * cd_ref[p] + jnp.where(same_head, kvs[p], 0.0)

        def soft(s, h):
            s = jnp.where(valid, s * scale, NEG_INF)
            sink = sink_ref[h]
            m = jnp.maximum(jnp.max(s, axis=-1, keepdims=True), sink)
            e = jnp.exp(s - m).astype(BF16)
            den = jnp.dot(e, row_sum, preferred_element_type=F32) + jnp.exp(sink - m)
            return e, den

        es, dens = [], []
        for t, (sa, sb) in enumerate(s_att):
            e_a, den_a = soft(sa, 2 * t)
            e_b, den_b = soft(sb, 2 * t + 1)
            es.append(jnp.concatenate([e_a, e_b], axis=1))
            dens.append(jnp.where(low, den_a, den_b))

        mus = [_head_mean(y, head_avg) for y in ys]
        for t in range(SWA_Q_HEADS // 2):
            o = jnp.dot(es[t], v_cats[t // (SWA_GROUP // 2)], preferred_element_type=F32) / dens[t]
            mix_s[rows, lanes(RET_WIDTH, t)] = o * asc_ref[:, lanes(0, t)]

        ds = [y - mu for y, mu in zip(ys, mus)]
        variances = [_head_mean(d * d, head_avg) for d in ds]
        for p in range(RET_HEADS // 2):
            yn = ds[p] * lax.rsqrt(variances[p] + GN_EPS)
            gate = proj_s[rows, lanes(OFF_RG, p)]
            mix_s[rows, lanes(0, p)] = _silu(gate) * (yn * gn_ref[:, lanes(0, p)])
        return carry

    lax.fori_loop(0, MIX_CHUNKS, chunk_body, 0)

    y = jnp.dot(mix_s[...].astype(BF16), wout_ref[...], preferred_element_type=F32)
    y = y + DEEPNORM_ALPHA * x_ref[...]
    o_ref[...] = _layer_norm(y, g1_ref[...], b1_ref[...])


def _rope_tables(seq, rot_dim, theta):
    half = rot_dim // 2
    inv_freq = theta ** (-jnp.arange(half, dtype=F32) / half)
    ang = jnp.arange(seq, dtype=F32)[:, None] * inv_freq[None, :]
    cos, sin = jnp.cos(ang), jnp.sin(ang)
    rest = HEAD_DIM - rot_dim
    cos_h = jnp.concatenate([cos, cos, jnp.ones((seq, rest), F32)], axis=1)
    sin_h = jnp.concatenate([-sin, sin, jnp.zeros((seq, rest), F32)], axis=1)
    reps = V7X_LANES // HEAD_DIM
    return jnp.tile(cos_h, (1, reps)), jnp.tile(sin_h, (1, reps))


def _retention_tables():
    h = RET_HEADS
    log_g = jnp.log1p(-(2.0 ** (-5.0 - jnp.arange(h, dtype=F32))))
    idx = jnp.arange(CHUNK, dtype=F32)
    diff = idx[:, None] - idx[None, :]
    intra = jnp.where(diff[None] >= 0, jnp.exp(jnp.maximum(diff, 0.0)[None] * log_g[:, None, None]), 0.0)
    k_w = jnp.exp((CHUNK - 1 - idx)[:, None] * log_g[None, :])
    q_w = jnp.exp((idx + 1)[:, None] * log_g[None, :])
    chunk_decay = jnp.exp(CHUNK * log_g)
    pairs = lambda t: jnp.repeat(t.T.reshape(h // 2, 2, -1), HEAD_DIM, axis=1).transpose(0, 2, 1)
    return intra, pairs(k_w), pairs(q_w), pairs(chunk_decay[None, :])


def _mixer(x, w_in, w_out, gn_gain, attn_scale, sinks, ln_g, ln_b):
    b, s, d = x.shape
    cr, sr = _rope_tables(s, HEAD_DIM, RET_THETA)
    ca, sa = _rope_tables(s, SWA_ROT_DIM, SWA_THETA)
    dec, kw, qw, cd = _retention_tables()
    const = lambda shape: pl.BlockSpec(shape, lambda bi, i: (0,) * len(shape))
    tab = pl.BlockSpec((MIX_TILE, V7X_LANES), lambda bi, i: (i, 0))
    smem = pl.BlockSpec(memory_space=pltpu.SMEM)
    return pl.pallas_call(
        _mixer_kernel,
        name="mixer",
        grid=(b, s // MIX_TILE),
        in_specs=[
            pl.BlockSpec((None, MIX_TILE, d), lambda bi, i: (bi, i, 0)),
            const((d, IN_WIDTH)), const((MIX_WIDTH, d)),
            tab, tab, tab, tab,
            const((RET_HEADS, CHUNK, CHUNK)), const((RET_HEADS // 2, CHUNK, V7X_LANES)),
            const((RET_HEADS // 2, CHUNK, V7X_LANES)), const((RET_HEADS // 2, 1, V7X_LANES)),
            const((1, RET_WIDTH)), const((1, SWA_WIDTH)), smem,
            const((1, d)), const((1, d)),
        ],
        out_specs=pl.BlockSpec((None, MIX_TILE, d), lambda bi, i: (bi, i, 0)),
        out_shape=jax.ShapeDtypeStruct((b, s, d), F32),
        scratch_shapes=[
            pltpu.VMEM((MIX_TILE, IN_WIDTH), F32),
            pltpu.VMEM((MIX_TILE, MIX_WIDTH), F32),
            pltpu.VMEM((RET_HEADS // 2, V7X_LANES, V7X_LANES), F32),
            pltpu.VMEM((CHUNK, KV_WIDTH), F32),
            pltpu.VMEM((CHUNK, KV_WIDTH), F32),
        ],
        compiler_params=pltpu.CompilerParams(
            dimension_semantics=("parallel", "arbitrary"),
            vmem_limit_bytes=V7X_VMEM_LIMIT_BYTES),
    )(x, w_in.astype(BF16), w_out.astype(BF16), cr, sr, ca, sa, dec, kw, qw, cd,
      gn_gain.reshape(1, -1), attn_scale.reshape(1, -1), sinks, ln_g.reshape(1, -1), ln_b.reshape(1, -1))


def _route(x, wr_ref, bias_ref, tri_ref, between):
    tt = x.shape[0]
    xh = x.astype(BF16)
    xl = (x - xh.astype(F32)).astype(BF16)
    w = wr_ref[...]
    wh = w.astype(BF16)
    wl = (w - wh.astype(F32)).astype(BF16)
    dg = functools.partial(lax.dot_general, dimension_numbers=NT_DIMS, preferred_element_type=F32)
    logits = dg(wh, xh) + (dg(wh, xl) + dg(wl, xh))
    scores = jax.nn.sigmoid(logits)
    biased = scores + bias_ref[...]

    shape3 = (N_GROUPS, GROUP_SIZE, tt)
    grp = lax.broadcasted_iota(jnp.int32, shape3, 0)
    mem = lax.broadcasted_iota(jnp.int32, shape3, 1)
    b3 = biased.reshape(shape3)

    m1 = jnp.max(b3, axis=1, keepdims=True)
    i1 = jnp.min(jnp.where(b3 == m1, mem, GROUP_SIZE), axis=1, keepdims=True)
    m2 = jnp.max(jnp.where(mem == i1, -jnp.inf, b3), axis=1, keepdims=True)
    gs = jnp.broadcast_to(m1 + m2, shape3)

    gmask = jnp.zeros(shape3, F32)
    for _ in range(TOPK_GROUPS):
        m = jnp.max(gs, axis=0, keepdims=True)
        pick = grp == jnp.min(jnp.where(gs == m, grp, N_GROUPS), axis=0, keepdims=True)
        gmask = jnp.where(pick, 1.0, gmask)
        gs = jnp.where(pick, -jnp.inf, gs)
        between()

    eid = grp * GROUP_SIZE + mem
    cand = jnp.where(gmask > 0.0, b3, NEG_INF)
    sel = jnp.zeros(shape3, F32)
    for _ in range(TOP_K):
        m = jnp.max(jnp.max(cand, axis=0, keepdims=True), axis=1, keepdims=True)
        first = jnp.where(cand == m, eid, N_EXPERTS)
        first = jnp.min(jnp.min(first, axis=0, keepdims=True), axis=1, keepdims=True)
        pick = eid == first
        sel = jnp.where(pick, 1.0, sel)
        cand = jnp.where(pick, -jnp.inf, cand)
        between()

    sel = sel.reshape(N_EXPERTS, tt)
    wsel = scores * sel
    comb = wsel / jnp.sum(wsel, axis=0, keepdims=True) * ROUTED_SCALE

    lane = lax.broadcasted_iota(jnp.int32, (N_EXPERTS, V7X_LANES), 1)
    cnt = jnp.zeros((N_EXPERTS, V7X_LANES), F32)
    ranks = []
    for sb in range(tt // ROUTE_SUB):
        cols = slice(sb * ROUTE_SUB, (sb + 1) * ROUTE_SUB)
        sel_sb = sel[:, cols]
        rank = jnp.dot(sel_sb.astype(BF16), tri_ref[...], preferred_element_type=F32)
        ranks.append(jnp.where(sel_sb > 0.0, rank, -1.0))
        cnt = jnp.where(lane == sb, jnp.sum(sel_sb, axis=1, keepdims=True), cnt)
    return comb, jnp.concatenate(ranks, axis=1), cnt


def _route_dispatch_kernel(x_ref, wr_ref, bias_ref, tri_ref, wt_ref, rk_ref, cnt_ref, xs_ref, xb_s, rk_s):
    step = pl.program_id(0)

    @pl.when(step == 0)
    def _():
        xb_s[...] = jnp.zeros_like(xb_s)
        rk_s[...] = jnp.full(rk_s.shape, -1.0, F32)

    prev = (step + 1) % 2
    one = jnp.ones((), BF16)

    def place(sb, g):
        cols = slice(sb * ROUTE_SUB, (sb + 1) * ROUTE_SUB)
        onehot = _slot_onehot(lambda e: rk_s[prev, e:e + 1, cols], g, lambda e: one)
        rows = jnp.dot(onehot, xb_s[prev, cols, :], preferred_element_type=F32).astype(BF16)
        xs_ref[g * SLOT_GROUP:(g + 1) * SLOT_GROUP, sb * SLOT_ROWS:(sb + 1) * SLOT_ROWS, :] = (
            rows.reshape(SLOT_GROUP, SLOT_ROWS, rows.shape[1]))

    units = iter([functools.partial(place, sb, g) for sb in range(ROUTE_TILE // ROUTE_SUB)
                  for g in range(N_EXPERTS // SLOT_GROUP)])
    x = x_ref[...]
    comb, ranks, cnt = _route(x, wr_ref, bias_ref, tri_ref, lambda: next(units, lambda: None)())
    for unit in units:
        unit()

    wt_ref[...] = comb
    rk_ref[...] = ranks
    cnt_ref[...] = cnt
    xb_s[step % 2] = x.astype(BF16)
    rk_s[step % 2] = ranks


def _route_dispatch(x1, w_router, router_bias):
    t, d = x1.shape
    n_tiles = t // ROUTE_TILE
    tile_rows = ROUTE_TILE // ROUTE_SUB * SLOT_ROWS
    tri = (jnp.arange(ROUTE_SUB)[:, None] < jnp.arange(ROUTE_SUB)[None, :]).astype(BF16)
    routed = lambda s: jnp.minimum(s, n_tiles - 1)
    placed = lambda s: jnp.maximum(s - 1, 0)
    return pl.pallas_call(
        _route_dispatch_kernel,
        name="route_dispatch",
        grid=(n_tiles + 1,),
        in_specs=[
            pl.BlockSpec((ROUTE_TILE, d), lambda s: (routed(s), 0)),
            pl.BlockSpec((N_EXPERTS, d), lambda s: (0, 0)),
            pl.BlockSpec((N_EXPERTS, 1), lambda s: (0, 0)),
            pl.BlockSpec((ROUTE_SUB, ROUTE_SUB), lambda s: (0, 0)),
        ],
        out_specs=[
            pl.BlockSpec((N_EXPERTS, ROUTE_TILE), lambda s: (0, routed(s))),
            pl.BlockSpec((N_EXPERTS, ROUTE_TILE), lambda s: (0, routed(s))),
            pl.BlockSpec((None, N_EXPERTS, V7X_LANES), lambda s: (routed(s), 0, 0)),
            pl.BlockSpec((N_EXPERTS, tile_rows, d), lambda s: (0, placed(s), 0)),
        ],
        out_shape=[
            jax.ShapeDtypeStruct((N_EXPERTS, t), F32),
            jax.ShapeDtypeStruct((N_EXPERTS, t), F32),
            jax.ShapeDtypeStruct((n_tiles, N_EXPERTS, V7X_LANES), F32),
            jax.ShapeDtypeStruct((N_EXPERTS, n_tiles * tile_rows, d), BF16),
        ],
        scratch_shapes=[
            pltpu.VMEM((2, ROUTE_TILE, d), BF16),
            pltpu.VMEM((2, N_EXPERTS, ROUTE_TILE), F32),
        ],
        compiler_params=pltpu.CompilerParams(
            dimension_semantics=("arbitrary",),
            vmem_limit_bytes=V7X_VMEM_LIMIT_BYTES),
    )(x1, w_router.T, router_bias.reshape(-1, 1), tri)


def _slot_onehot(rank_row, group, values):
    rid = lax.broadcasted_iota(jnp.int32, (SLOT_ROWS, ROUTE_SUB), 0).astype(F32).astype(BF16)
    pieces = []
    for i in range(SLOT_GROUP):
        e = group * SLOT_GROUP + i
        pieces.append(jnp.where(rid == rank_row(e).astype(BF16), values(e), jnp.zeros((), BF16)))
    return jnp.concatenate(pieces, axis=0)


def _expert_ffn(x, w1, w3, w2):
    h = _silu(jnp.dot(x, w1, preferred_element_type=F32))
    h = h * jnp.dot(x, w3, preferred_element_type=F32)
    return jnp.dot(h.astype(BF16), w2, preferred_element_type=F32).astype(BF16)


def _expert_kernel(xs_ref, w1_ref, w3_ref, w2_ref, ys_ref, w1b_ref, w3b_ref, w2b_ref, w1_s, w3_s, w2_s):
    @pl.when(pl.program_id(1) == 0)
    def _():
        for src, dst, out in ((w1_ref, w1_s, w1b_ref), (w3_ref, w3_s, w3b_ref), (w2_ref, w2_s, w2b_ref)):
            wb = src[...].astype(BF16)
            dst[...] = wb
            out[...] = wb

    block = xs_ref.shape[0]
    tile = EXPERT_TILE if block % EXPERT_TILE == 0 else block
    for q in range(block // tile):
        rows = slice(q * tile, (q + 1) * tile)
        ys_ref[rows, :] = _expert_ffn(xs_ref[rows, :], w1_s[...], w3_s[...], w2_s[...])


def _experts(xs, w1, w3, w2):
    n_e, seg, d = xs.shape
    block = EXPERT_BLOCK if seg % EXPERT_BLOCK == 0 else seg
    row_spec = pl.BlockSpec((None, block, d), lambda e, t: (e, t, 0))
    return pl.pallas_call(
        _expert_kernel,
        name="experts",
        grid=(n_e, seg // block),
        in_specs=[
            row_spec,
            pl.BlockSpec((None, d, EXPERT_DIM), lambda e, t: (e, 0, 0)),
            pl.BlockSpec((None, d, EXPERT_DIM), lambda e, t: (e, 0, 0)),
            pl.BlockSpec((None, EXPERT_DIM, d), lambda e, t: (e, 0, 0)),
        ],
        out_specs=[
            row_spec,
            pl.BlockSpec((None, d, EXPERT_DIM), lambda e, t: (e, 0, 0)),
            pl.BlockSpec((None, d, EXPERT_DIM), lambda e, t: (e, 0, 0)),
            pl.BlockSpec((None, EXPERT_DIM, d), lambda e, t: (e, 0, 0)),
        ],
        out_shape=[
            jax.ShapeDtypeStruct(xs.shape, BF16),
            jax.ShapeDtypeStruct(w1.shape, BF16),
            jax.ShapeDtypeStruct(w3.shape, BF16),
            jax.ShapeDtypeStruct(w2.shape, BF16),
        ],
        scratch_shapes=[
            pltpu.VMEM((d, EXPERT_DIM), BF16),
            pltpu.VMEM((d, EXPERT_DIM), BF16),
            pltpu.VMEM((EXPERT_DIM, d), BF16),
        ],
        compiler_params=pltpu.CompilerParams(
            dimension_semantics=("parallel", "arbitrary"),
            vmem_limit_bytes=V7X_VMEM_LIMIT_BYTES),
    )(xs, w1, w3, w2)


def _combine_kernel(ovn_ref, ovl_ref, x_ref, p_ref, rk_ref, wt_ref, ys_ref,
                    ws1_ref, ws3_ref, ws2_ref, wg_ref, bg_ref, wp_ref, g2_ref, b2_ref,
                    w1_hbm, w3_hbm, w2_hbm, o_ref, acc_s, w1_s, w3_s, w2_s, sems):
    j = pl.program_id(0)
    n_over = ovn_ref[j]

    def fetch(k, half):
        e = ovl_ref[j * N_EXPERTS + k]
        return [pltpu.make_async_copy(src.at[e], dst.at[half], sems.at[half, i])
                for i, (src, dst) in enumerate(((w1_hbm, w1_s), (w3_hbm, w3_s), (w2_hbm, w2_s)))]

    @pl.when(n_over > 0)
    def _():
        for cp in fetch(0, 0):
            cp.start()

    x = x_ref[...]
    xb = x.astype(BF16)
    hs = _silu(jnp.dot(xb, ws1_ref[...], preferred_element_type=F32))
    hs = hs * jnp.dot(xb, ws3_ref[...], preferred_element_type=F32)
    shared = jnp.dot(hs.astype(BF16), ws2_ref[...], preferred_element_type=F32)
    gate = jax.nn.sigmoid(jnp.dot(xb, wg_ref[...], preferred_element_type=F32) + bg_ref[...])
    side = jnp.dot(p_ref[...].astype(BF16), wp_ref[...], preferred_element_type=F32)
    acc_s[...] = DEEPNORM_ALPHA * x + shared + gate * side

    for g in range(N_EXPERTS // SLOT_GROUP):
        scat = _slot_onehot(lambda e: rk_ref[e:e + 1, :], g, lambda e: wt_ref[e:e + 1, :].astype(BF16))
        rows = ys_ref[g * SLOT_GROUP:(g + 1) * SLOT_GROUP].reshape(SLOT_GROUP * SLOT_ROWS, ys_ref.shape[2])
        acc_s[...] += lax.dot_general(scat, rows, TN_DIMS, preferred_element_type=F32)

    sub = lax.broadcasted_iota(jnp.int32, (ROUTE_SUB, ROUTE_SUB), 0)
    lan = lax.broadcasted_iota(jnp.int32, (ROUTE_SUB, ROUTE_SUB), 1)

    def overflow(k, carry):
        half = k % 2

        @pl.when(k + 1 < n_over)
        def _():
            for cp in fetch(k + 1, 1 - half):
                cp.start()

        e = ovl_ref[j * N_EXPERTS + k]
        rank = rk_ref[pl.ds(e, 1), :]
        late = jnp.where(rank >= SLOT_ROWS, wt_ref[pl.ds(e, 1), :], 0.0)
        pick = jnp.where(sub == lan, jnp.broadcast_to(late, (ROUTE_SUB, ROUTE_SUB)), 0.0).astype(BF16)
        for cp in fetch(k, half):
            cp.wait()
        y = _expert_ffn(x_ref[...].astype(BF16), w1_s[half], w3_s[half], w2_s[half])
        acc_s[...] += jnp.dot(pick, y, preferred_element_type=F32)
        return carry

    lax.fori_loop(0, n_over, overflow, 0)
    o_ref[...] = _layer_norm(acc_s[...], g2_ref[...], b2_ref[...])


def _combine(x1, p, rk, wt, ys, ov_n, ov_list, w1, w3, w2, ws1, ws3, ws2, wg, bg, wp, ln_g, ln_b):
    t, d = x1.shape
    n_sub = t // ROUTE_SUB
    const = lambda shape: pl.BlockSpec(shape, lambda j, *_: (0,) * len(shape))
    hbm = pl.BlockSpec(memory_space=pl.ANY)
    return pl.pallas_call(
        _combine_kernel,
        name="combine",
        grid_spec=pltpu.PrefetchScalarGridSpec(
            num_scalar_prefetch=2,
            grid=(n_sub,),
            in_specs=[
                pl.BlockSpec((ROUTE_SUB, d), lambda j, *_: (j, 0)),
                pl.BlockSpec((ROUTE_SUB, PLE_DIM), lambda j, *_: (j, 0)),
                pl.BlockSpec((N_EXPERTS, ROUTE_SUB), lambda j, *_: (0, j)),
                pl.BlockSpec((N_EXPERTS, ROUTE_SUB), lambda j, *_: (0, j)),
                pl.BlockSpec((N_EXPERTS, SLOT_ROWS, d), lambda j, *_: (0, j, 0)),
                const((d, SHARED_DIM)), const((d, SHARED_DIM)), const((SHARED_DIM, d)),
                const((d, d)), const((1, d)), const((PLE_DIM, d)), const((1, d)), const((1, d)),
                hbm, hbm, hbm,
            ],
            out_specs=pl.BlockSpec((ROUTE_SUB, d), lambda j, *_: (j, 0)),
            scratch_shapes=[
                pltpu.VMEM((ROUTE_SUB, d), F32),
                pltpu.VMEM((2, d, EXPERT_DIM), BF16),
                pltpu.VMEM((2, d, EXPERT_DIM), BF16),
                pltpu.VMEM((2, EXPERT_DIM, d), BF16),
                pltpu.SemaphoreType.DMA((2, 3)),
            ],
        ),
        out_shape=jax.ShapeDtypeStruct((t, d), F32),
        compiler_params=pltpu.CompilerParams(
            dimension_semantics=("arbitrary",),
            vmem_limit_bytes=V7X_VMEM_LIMIT_BYTES),
    )(ov_n, ov_list, x1, p, rk, wt, ys,
      ws1.astype(BF16), ws3.astype(BF16), ws2.astype(BF16),
      wg.astype(BF16), bg.reshape(1, -1), wp.astype(BF16), ln_g.reshape(1, -1), ln_b.reshape(1, -1),
      w1, w3, w2)


def _overflow_lists(cnt):
    over = cnt > SLOT_ROWS
    place = jnp.cumsum(over, axis=1) - 1
    slot = jnp.arange(N_EXPERTS)
    hit = over[:, :, None] & (place[:, :, None] == slot[None, None, :])
    ov_list = jnp.sum(jnp.where(hit, slot[None, :, None], 0), axis=1)
    return jnp.sum(over, axis=1).astype(jnp.int32), ov_list.reshape(-1).astype(jnp.int32)


def _moe_block(x1, p, w_router, router_bias, w1, w3, w2, ws1, ws3, ws2, wg, bg, wp, ln_g, ln_b):
    t, d = x1.shape
    n_sub = t // ROUTE_SUB
    wt, rk, cnt, xs = _route_dispatch(x1, w_router, router_bias)
    per_step = ROUTE_TILE // ROUTE_SUB
    cnt = jnp.swapaxes(cnt[:, :, :per_step], 1, 2).reshape(n_sub, N_EXPERTS).astype(jnp.int32)
    ov_n, ov_list = _overflow_lists(cnt)
    ys, w1b, w3b, w2b = _experts(xs, w1, w3, w2)
    return _combine(x1, p, rk, wt, ys, ov_n, ov_list, w1b, w3b, w2b, ws1, ws3, ws2, wg, bg, wp, ln_g, ln_b)


def kernel(x, p, w_in, ret_gn_gain, attn_scale, sinks, w_out, ln1_g, ln1_b, w_router, router_bias,
           w1, w3, w2, ws1, ws3, ws2, w_ple_gate, b_ple_gate, w_ple_proj, ln2_g, ln2_b):
    b, s, d = x.shape
    h = x
    for i in range(DEPTH):
        h = _mixer(h, w_in[i], w_out[i], ret_gn_gain[i], attn_scale[i], sinks[i], ln1_g[i], ln1_b[i])
        h2 = _moe_block(h.reshape(b * s, d), p[i].reshape(b * s, -1), w_router[i], router_bias[i],
                        w1[i], w3[i], w2[i], ws1[i], ws3[i], ws2[i],
                        w_ple_gate[i], b_ple_gate[i], w_ple_proj[i], ln2_g[i], ln2_b[i])
        h = h2.reshape(b, s, d)
    return h
```

```python
import functools

import jax
import jax.numpy as jnp
from jax import lax
from jax.experimental import pallas as pl
from jax.experimental.pallas import tpu as pltpu

F32 = jnp.float32
BF16 = jnp.bfloat16

D_MODEL = 1024
HEAD_DIM = 64
RET_HEADS = 8
SWA_Q_HEADS = 8
SWA_KV_HEADS = 2
SWA_GROUP = SWA_Q_HEADS // SWA_KV_HEADS
RET_WIDTH = RET_HEADS * HEAD_DIM
SWA_WIDTH = SWA_Q_HEADS * HEAD_DIM
KV_WIDTH = SWA_KV_HEADS * HEAD_DIM
MIX_WIDTH = RET_WIDTH + SWA_WIDTH
IN_WIDTH = 4 * RET_WIDTH + SWA_WIDTH + 2 * KV_WIDTH
CHUNK = 128
WINDOW = 128
RET_THETA = 10000.0
SWA_THETA = 500000.0
SWA_ROT_DIM = HEAD_DIM // 4
N_EXPERTS = 64
TOP_K = 8
N_GROUPS = 8
GROUP_SIZE = N_EXPERTS // N_GROUPS
TOPK_GROUPS = 4
EXPERT_DIM = 256
SHARED_DIM = 256
ROUTED_SCALE = 2.5
PLE_DIM = 256
LN_EPS = 1e-5
GN_EPS = 1e-6
NEG_INF = -1e30
DEPTH = 1
DEEPNORM_ALPHA = (2.0 * DEPTH) ** 0.25

OFF_RQ = 0
OFF_RK = RET_WIDTH
OFF_RV = 2 * RET_WIDTH
OFF_RG = 3 * RET_WIDTH
OFF_AQ = 4 * RET_WIDTH
OFF_AK = OFF_AQ + SWA_WIDTH
OFF_AV = OFF_AK + KV_WIDTH

V7X_LANES = 128
V7X_VMEM_LIMIT_BYTES = 56 * 1024 * 1024

MIX_CHUNKS = 4
MIX_TILE = MIX_CHUNKS * CHUNK
ROUTE_SUB = 256
ROUTE_TILE = 512
SLOT_ROWS = 48
SLOT_GROUP = 16
COMBINE_SUBS = 2
EXPERT_TILE = 512
EXPERT_BLOCK = 3072

NT_DIMS = (((1,), (1,)), ((), ()))
TN_DIMS = (((0,), (0,)), ((), ()))


def _layer_norm(y, g, b):
    mu = jnp.mean(y, axis=-1, keepdims=True)
    d = y - mu
    var = jnp.mean(d * d, axis=-1, keepdims=True)
    return d * lax.rsqrt(var + LN_EPS) * g + b


def _silu(v):
    return v * jax.nn.sigmoid(v)


def _rope_inplace(buf, col0, n_groups, cos, sin, near, shift, scale):
    for g in range(n_groups):
        cols = slice(col0 + g * V7X_LANES, col0 + (g + 1) * V7X_LANES)
        xg = buf[:, cols]
        partner = jnp.where(near, pltpu.roll(xg, V7X_LANES - shift, 1), pltpu.roll(xg, shift, 1))
        y = xg * cos + partner * sin
        if scale != 1.0:
            y = y * scale
        buf[:, cols] = y


def _head_mean(y, head_avg):
    hi = y.astype(BF16)
    lo = (y - hi.astype(F32)).astype(BF16)
    return (jnp.dot(hi, head_avg, preferred_element_type=F32)
            + jnp.dot(lo, head_avg, preferred_element_type=F32))


def _mixer_kernel(x_ref, win_ref, wout_ref, cr_ref, sr_ref, ca_ref, sa_ref,
                  dec_ref, kw_ref, qw_ref, cd_ref, gn_ref, asc_ref, sink_ref,
                  g1_ref, b1_ref, o_ref, proj_s, mix_s, state_s, kprev_s, vprev_s):
    step = pl.program_id(1)

    @pl.when(step == 0)
    def _():
        state_s[...] = jnp.zeros_like(state_s)
        kprev_s[...] = jnp.zeros_like(kprev_s)
        vprev_s[...] = jnp.zeros_like(vprev_s)

    proj_s[...] = jnp.dot(x_ref[...].astype(BF16), win_ref[...], preferred_element_type=F32)

    lane = lax.broadcasted_iota(jnp.int32, (MIX_TILE, V7X_LANES), 1) % HEAD_DIM
    scale = HEAD_DIM ** -0.5
    cr, sr = cr_ref[...], sr_ref[...]
    _rope_inplace(proj_s, OFF_RQ, RET_WIDTH // V7X_LANES, cr, sr, lane < HEAD_DIM // 2, HEAD_DIM // 2, 1.0)
    _rope_inplace(proj_s, OFF_RK, RET_WIDTH // V7X_LANES, cr, sr, lane < HEAD_DIM // 2, HEAD_DIM // 2, scale)
    ca, sa = ca_ref[...], sa_ref[...]
    _rope_inplace(proj_s, OFF_AQ, SWA_WIDTH // V7X_LANES, ca, sa, lane < SWA_ROT_DIM // 2, SWA_ROT_DIM // 2, 1.0)
    _rope_inplace(proj_s, OFF_AK, KV_WIDTH // V7X_LANES, ca, sa, lane < SWA_ROT_DIM // 2, SWA_ROT_DIM // 2, 1.0)

    qi = lax.broadcasted_iota(jnp.int32, (CHUNK, 2 * CHUNK), 0)
    kj = lax.broadcasted_iota(jnp.int32, (CHUNK, 2 * CHUNK), 1)
    rel = CHUNK + qi - kj
    in_window = (rel >= 0) & (rel < WINDOW)

    sub = lax.broadcasted_iota(jnp.int32, (CHUNK, V7X_LANES), 0)
    lan = lax.broadcasted_iota(jnp.int32, (CHUNK, V7X_LANES), 1)
    low = lan < HEAD_DIM
    same_head = (sub // HEAD_DIM) == (lan // HEAD_DIM)
    head_avg = jnp.where(same_head, 1.0 / HEAD_DIM, 0.0).astype(BF16)
    row_sum = jnp.ones((2 * CHUNK, V7X_LANES), BF16)

    def chunk_stages(c):
        rows = slice(c * CHUNK, (c + 1) * CHUNK)

        valid = in_window & (kj >= jnp.where(step == 0, CHUNK, 0)) if c == 0 else in_window
        lanes = lambda off, p: slice(off + p * V7X_LANES, off + (p + 1) * V7X_LANES)

        q2s, v2s, s_ret, kvs = [], [], [], []
        for p in range(RET_HEADS // 2):
            q2 = proj_s[rows, lanes(OFF_RQ, p)]
            k2 = proj_s[rows, lanes(OFF_RK, p)]
            v2 = proj_s[rows, lanes(OFF_RV, p)]
            q_ab = jnp.concatenate([jnp.where(low, q2, 0.0), jnp.where(low, 0.0, q2)], axis=0).astype(BF16)
            s_ret.append(lax.dot_general(q_ab, k2.astype(BF16), NT_DIMS, preferred_element_type=F32))
            kvs.append(lax.dot_general((k2 * kw_ref[p]).astype(BF16), v2.astype(BF16), TN_DIMS,
                                       preferred_element_type=F32))
            q2s.append(q2)
            v2s.append(v2)

        kc = proj_s[rows, OFF_AK:OFF_AK + KV_WIDTH]
        vc = proj_s[rows, OFF_AV:OFF_AV + KV_WIDTH]
        kband = jnp.concatenate([kprev_s[...], kc], axis=0)
        vband = jnp.concatenate([vprev_s[...], vc], axis=0)
        kprev_s[...] = kc
        vprev_s[...] = vc
        kband_r = pltpu.roll(kband, HEAD_DIM, 1)
        vband_r = pltpu.roll(vband, HEAD_DIM, 1)
        low2 = jnp.concatenate([low, low], axis=0)
        s_att, v_cats = [], []
        for g in range(SWA_KV_HEADS):
            k_even, k_odd = (kband, kband_r) if g == 0 else (kband_r, kband)
            v_low, v_high = (vband, vband_r) if g == 0 else (vband_r, vband)
            v_cats.append(jnp.concatenate([jnp.where(low2, v_low, 0.0), jnp.where(low2, 0.0, v_high)],
                                          axis=0).astype(BF16))
            q_off = OFF_AQ + g * SWA_GROUP * HEAD_DIM
            qt0 = proj_s[rows, q_off:q_off + V7X_LANES]
            qt1 = proj_s[rows, q_off + V7X_LANES:q_off + 2 * V7X_LANES]
            q_even = jnp.concatenate([jnp.where(low, qt0, 0.0), jnp.where(low, qt1, 0.0)], axis=0)
            q_odd = jnp.concatenate([jnp.where(low, 0.0, qt0), jnp.where(low, 0.0, qt1)], axis=0)
            s_even = lax.dot_general(q_even.astype(BF16), k_even.astype(BF16), NT_DIMS,
                                     preferred_element_type=F32)
            s_odd = lax.dot_general(q_odd.astype(BF16), k_odd.astype(BF16), NT_DIMS,
                                    preferred_element_type=F32)
            s_att.append((s_even[:CHUNK], s_odd[:CHUNK]))
            s_att.append((s_even[CHUNK:], s_odd[CHUNK:]))
        yield

        ys = []
        for p in range(RET_HEADS // 2):
            s_a = (s_ret[p][:CHUNK] * dec_ref[2 * p]).astype(BF16)
            s_b = (s_ret[p][CHUNK:] * dec_ref[2 * p + 1]).astype(BF16)
            st = state_s[p]
            lhs = jnp.concatenate([s_a, s_b, (q2s[p] * qw_ref[p]).astype(BF16)], axis=1)
            rhs = jnp.concatenate([jnp.where(low, v2s[p], 0.0), jnp.where(low, 0.0, v2s[p]), st],
                                  axis=0).astype(BF16)
            ys.append(jnp.dot(lhs, rhs, preferred_element_type=F32))
            state_s[p] = st * cd_ref[p] + jnp.where(same_head, kvs[p], 0.0)

        def soft(s, h):
            s = jnp.where(valid, s * scale, NEG_INF)
            sink = sink_ref[h]
            m = jnp.maximum(jnp.max(s, axis=-1, keepdims=True), sink)
            e = jnp.exp(s - m).astype(BF16)
            den = jnp.dot(e, row_sum, preferred_element_type=F32) + jnp.exp(sink - m)
            return e, den

        es, dens = [], []
        for t, (sa, sb) in enumerate(s_att):
            e_a, den_a = soft(sa, 2 * t)
            e_b, den_b = soft(sb, 2 * t + 1)
            es.append(jnp.concatenate([e_a, e_b], axis=1))
            dens.append(jnp.where(low, den_a, den_b))
        yield

        mus = [_head_mean(y, head_avg) for y in ys]
        for t in range(SWA_Q_HEADS // 2):
            o = jnp.dot(es[t], v_cats[t // (SWA_GROUP // 2)], preferred_element_type=F32) / dens[t]
            mix_s[rows, lanes(RET_WIDTH, t)] = o * asc_ref[:, lanes(0, t)]
        yield

        ds = [y - mu for y, mu in zip(ys, mus)]
        variances = [_head_mean(d * d, head_avg) for d in ds]
        for p in range(RET_HEADS // 2):
            yn = ds[p] * lax.rsqrt(variances[p] + GN_EPS)
            gate = proj_s[rows, lanes(OFF_RG, p)]
            mix_s[rows, lanes(0, p)] = _silu(gate) * (yn * gn_ref[:, lanes(0, p)])

    chunks = [chunk_stages(c) for c in range(MIX_CHUNKS)]
    for c in range(MIX_CHUNKS):
        next(chunks[c])
        if c > 0:
            next(chunks[c - 1], None)
        next(chunks[c])
        next(chunks[c])
    next(chunks[-1], None)

    y = jnp.dot(mix_s[...].astype(BF16), wout_ref[...], preferred_element_type=F32)
    y = y + DEEPNORM_ALPHA * x_ref[...]
    o_ref[...] = _layer_norm(y, g1_ref[...], b1_ref[...])


def _rope_tables(seq, rot_dim, theta):
    half = rot_dim // 2
    inv_freq = theta ** (-jnp.arange(half, dtype=F32) / half)
    ang = jnp.arange(seq, dtype=F32)[:, None] * inv_freq[None, :]
    cos, sin = jnp.cos(ang), jnp.sin(ang)
    rest = HEAD_DIM - rot_dim
    cos_h = jnp.concatenate([cos, cos, jnp.ones((seq, rest), F32)], axis=1)
    sin_h = jnp.concatenate([-sin, sin, jnp.zeros((seq, rest), F32)], axis=1)
    reps = V7X_LANES // HEAD_DIM
    return jnp.tile(cos_h, (1, reps)), jnp.tile(sin_h, (1, reps))


def _retention_tables():
    h = RET_HEADS
    log_g = jnp.log1p(-(2.0 ** (-5.0 - jnp.arange(h, dtype=F32))))
    idx = jnp.arange(CHUNK, dtype=F32)
    diff = idx[:, None] - idx[None, :]
    intra = jnp.where(diff[None] >= 0, jnp.exp(jnp.maximum(diff, 0.0)[None] * log_g[:, None, None]), 0.0)
    k_w = jnp.exp((CHUNK - 1 - idx)[:, None] * log_g[None, :])
    q_w = jnp.exp((idx + 1)[:, None] * log_g[None, :])
    chunk_decay = jnp.exp(CHUNK * log_g)
    pairs = lambda t: jnp.repeat(t.T.reshape(h // 2, 2, -1), HEAD_DIM, axis=1).transpose(0, 2, 1)
    return intra, pairs(k_w), pairs(q_w), pairs(chunk_decay[None, :])


def _mixer(x, w_in, w_out, gn_gain, attn_scale, sinks, ln_g, ln_b):
    b, s, d = x.shape
    cr, sr = _rope_tables(s, HEAD_DIM, RET_THETA)
    ca, sa = _rope_tables(s, SWA_ROT_DIM, SWA_THETA)
    dec, kw, qw, cd = _retention_tables()
    const = lambda shape: pl.BlockSpec(shape, lambda bi, i: (0,) * len(shape))
    tab = pl.BlockSpec((MIX_TILE, V7X_LANES), lambda bi, i: (i, 0))
    smem = pl.BlockSpec(memory_space=pltpu.SMEM)
    return pl.pallas_call(
        _mixer_kernel,
        name="mixer",
        grid=(b, s // MIX_TILE),
        in_specs=[
            pl.BlockSpec((None, MIX_TILE, d), lambda bi, i: (bi, i, 0)),
            const((d, IN_WIDTH)), const((MIX_WIDTH, d)),
            tab, tab, tab, tab,
            const((RET_HEADS, CHUNK, CHUNK)), const((RET_HEADS // 2, CHUNK, V7X_LANES)),
            const((RET_HEADS // 2, CHUNK, V7X_LANES)), const((RET_HEADS // 2, 1, V7X_LANES)),
            const((1, RET_WIDTH)), const((1, SWA_WIDTH)), smem,
            const((1, d)), const((1, d)),
        ],
        out_specs=pl.BlockSpec((None, MIX_TILE, d), lambda bi, i: (bi, i, 0)),
        out_shape=jax.ShapeDtypeStruct((b, s, d), F32),
        scratch_shapes=[
            pltpu.VMEM((MIX_TILE, IN_WIDTH), F32),
            pltpu.VMEM((MIX_TILE, MIX_WIDTH), F32),
            pltpu.VMEM((RET_HEADS // 2, V7X_LANES, V7X_LANES), F32),
            pltpu.VMEM((CHUNK, KV_WIDTH), F32),
            pltpu.VMEM((CHUNK, KV_WIDTH), F32),
        ],
        compiler_params=pltpu.CompilerParams(
            dimension_semantics=("parallel", "arbitrary"),
            vmem_limit_bytes=V7X_VMEM_LIMIT_BYTES),
    )(x, w_in.astype(BF16), w_out.astype(BF16), cr, sr, ca, sa, dec, kw, qw, cd,
      gn_gain.reshape(1, -1), attn_scale.reshape(1, -1), sinks, ln_g.reshape(1, -1), ln_b.reshape(1, -1))


def _route(x, wr_ref, bias_ref, tri_ref, between):
    tt = x.shape[0]
    xh = x.astype(BF16)
    xl = (x - xh.astype(F32)).astype(BF16)
    w = wr_ref[...]
    wh = w.astype(BF16)
    wl = (w - wh.astype(F32)).astype(BF16)
    dot = functools.partial(jnp.dot, preferred_element_type=F32)
    logits = (dot(xh, wh) + (dot(xl, wh) + dot(xh, wl))).T
    scores = jax.nn.sigmoid(logits)
    biased = scores + bias_ref[...]

    shape3 = (N_GROUPS, GROUP_SIZE, tt)
    grp = lax.broadcasted_iota(jnp.int32, shape3, 0)
    mem = lax.broadcasted_iota(jnp.int32, shape3, 1)
    b3 = biased.reshape(shape3)

    m1 = jnp.max(b3, axis=1, keepdims=True)
    i1 = jnp.min(jnp.where(b3 == m1, mem, GROUP_SIZE), axis=1, keepdims=True)
    m2 = jnp.max(jnp.where(mem == i1, -jnp.inf, b3), axis=1, keepdims=True)
    gs = jnp.broadcast_to(m1 + m2, shape3)

    gmask = jnp.zeros(shape3, F32)
    for _ in range(TOPK_GROUPS):
        m = jnp.max(gs, axis=0, keepdims=True)
        pick = grp == jnp.min(jnp.where(gs == m, grp, N_GROUPS), axis=0, keepdims=True)
        gmask = jnp.where(pick, 1.0, gmask)
        gs = jnp.where(pick, -jnp.inf, gs)
        between()

    eid = grp * GROUP_SIZE + mem
    cand = jnp.where(gmask > 0.0, b3, NEG_INF)
    sel = jnp.zeros(shape3, F32)
    for _ in range(TOP_K):
        m = jnp.max(jnp.max(cand, axis=0, keepdims=True), axis=1, keepdims=True)
        first = jnp.where(cand == m, eid, N_EXPERTS)
        first = jnp.min(jnp.min(first, axis=0, keepdims=True), axis=1, keepdims=True)
        pick = eid == first
        sel = jnp.where(pick, 1.0, sel)
        cand = jnp.where(pick, -jnp.inf, cand)
        between()

    sel = sel.reshape(N_EXPERTS, tt)
    wsel = scores * sel
    comb = wsel / jnp.sum(wsel, axis=0, keepdims=True) * ROUTED_SCALE

    lane = lax.broadcasted_iota(jnp.int32, (N_EXPERTS, V7X_LANES), 1)
    cnt = jnp.zeros((N_EXPERTS, V7X_LANES), F32)
    ranks = []
    for sb in range(tt // ROUTE_SUB):
        cols = slice(sb * ROUTE_SUB, (sb + 1) * ROUTE_SUB)
        sel_sb = sel[:, cols]
        rank = jnp.dot(sel_sb.astype(BF16), tri_ref[...], preferred_element_type=F32)
        ranks.append(jnp.where(sel_sb > 0.0, rank, -1.0))
        cnt = jnp.where(lane == sb, jnp.sum(sel_sb, axis=1, keepdims=True), cnt)
    return comb, jnp.concatenate(ranks, axis=1), cnt


def _route_dispatch_kernel(x_ref, wr_ref, bias_ref, tri_ref, wt_ref, rk_ref, cnt_ref, xs_ref, xb_s, rk_s):
    step = pl.program_id(0)

    @pl.when(step == 0)
    def _():
        xb_s[...] = jnp.zeros_like(xb_s)
        rk_s[...] = jnp.full(rk_s.shape, -1.0, F32)

    prev = (step + 1) % 2
    one = jnp.ones((), BF16)

    def place(sb, g):
        cols = slice(sb * ROUTE_SUB, (sb + 1) * ROUTE_SUB)
        onehot = _slot_onehot(lambda e: rk_s[prev, e:e + 1, cols], g, lambda e: one)
        rows = jnp.dot(onehot, xb_s[prev, cols, :], preferred_element_type=F32).astype(BF16)
        xs_ref[g * SLOT_GROUP:(g + 1) * SLOT_GROUP, sb * SLOT_ROWS:(sb + 1) * SLOT_ROWS, :] = (
            rows.reshape(SLOT_GROUP, SLOT_ROWS, rows.shape[1]))

    units = iter([functools.partial(place, sb, g) for sb in range(ROUTE_TILE // ROUTE_SUB)
                  for g in range(N_EXPERTS // SLOT_GROUP)])
    x = x_ref[...]
    comb, ranks, cnt = _route(x, wr_ref, bias_ref, tri_ref, lambda: next(units, lambda: None)())
    for unit in units:
        unit()

    wt_ref[...] = comb
    rk_ref[...] = ranks
    cnt_ref[...] = cnt
    xb_s[step % 2] = x.astype(BF16)
    rk_s[step % 2] = ranks


def _route_dispatch(x1, w_router, router_bias):
    t, d = x1.shape
    n_tiles = t // ROUTE_TILE
    tile_rows = ROUTE_TILE // ROUTE_SUB * SLOT_ROWS
    tri = (jnp.arange(ROUTE_SUB)[:, None] < jnp.arange(ROUTE_SUB)[None, :]).astype(BF16)
    routed = lambda s: jnp.minimum(s, n_tiles - 1)
    placed = lambda s: jnp.maximum(s - 1, 0)
    return pl.pallas_call(
        _route_dispatch_kernel,
        name="route_dispatch",
        grid=(n_tiles + 1,),
        in_specs=[
            pl.BlockSpec((ROUTE_TILE, d), lambda s: (routed(s), 0)),
            pl.BlockSpec((d, N_EXPERTS), lambda s: (0, 0)),
            pl.BlockSpec((N_EXPERTS, 1), lambda s: (0, 0)),
            pl.BlockSpec((ROUTE_SUB, ROUTE_SUB), lambda s: (0, 0)),
        ],
        out_specs=[
            pl.BlockSpec((N_EXPERTS, ROUTE_TILE), lambda s: (0, routed(s))),
            pl.BlockSpec((N_EXPERTS, ROUTE_TILE), lambda s: (0, routed(s))),
            pl.BlockSpec((None, N_EXPERTS, V7X_LANES), lambda s: (routed(s), 0, 0)),
            pl.BlockSpec((N_EXPERTS, tile_rows, d), lambda s: (0, placed(s), 0)),
        ],
        out_shape=[
            jax.ShapeDtypeStruct((N_EXPERTS, t), F32),
            jax.ShapeDtypeStruct((N_EXPERTS, t), F32),
            jax.ShapeDtypeStruct((n_tiles, N_EXPERTS, V7X_LANES), F32),
            jax.ShapeDtypeStruct((N_EXPERTS, n_tiles * tile_rows, d), BF16),
        ],
        scratch_shapes=[
            pltpu.VMEM((2, ROUTE_TILE, d), BF16),
            pltpu.VMEM((2, N_EXPERTS, ROUTE_TILE), F32),
        ],
        compiler_params=pltpu.CompilerParams(
            dimension_semantics=("arbitrary",),
            vmem_limit_bytes=V7X_VMEM_LIMIT_BYTES),
    )(x1, w_router, router_bias.reshape(-1, 1), tri)


def _slot_onehot(rank_row, group, values):
    rid = lax.broadcasted_iota(jnp.int32, (SLOT_ROWS, ROUTE_SUB), 0).astype(F32).astype(BF16)
    pieces = []
    for i in range(SLOT_GROUP):
        e = group * SLOT_GROUP + i
        pieces.append(jnp.where(rid == rank_row(e).astype(BF16), values(e), jnp.zeros((), BF16)))
    return jnp.concatenate(pieces, axis=0)


def _expert_ffn(x, w1, w3, w2):
    h = _silu(jnp.dot(x, w1, preferred_element_type=F32))
    h = h * jnp.dot(x, w3, preferred_element_type=F32)
    return jnp.dot(h.astype(BF16), w2, preferred_element_type=F32).astype(BF16)


def _expert_kernel(xs_ref, w1_ref, w3_ref, w2_ref, ys_ref, w1b_ref, w3b_ref, w2b_ref, w1_s, w3_s, w2_s):
    @pl.when(pl.program_id(1) == 0)
    def _():
        for src, dst, out in ((w1_ref, w1_s, w1b_ref), (w3_ref, w3_s, w3b_ref), (w2_ref, w2_s, w2b_ref)):
            wb = src[...].astype(BF16)
            dst[...] = wb
            out[...] = wb

    block = xs_ref.shape[0]
    tile = EXPERT_TILE if block % EXPERT_TILE == 0 else block
    for q in range(block // tile):
        rows = slice(q * tile, (q + 1) * tile)
        ys_ref[rows, :] = _expert_ffn(xs_ref[rows, :], w1_s[...], w3_s[...], w2_s[...])


def _experts(xs, w1, w3, w2):
    n_e, seg, d = xs.shape
    block = EXPERT_BLOCK if seg % EXPERT_BLOCK == 0 else seg
    row_spec = pl.BlockSpec((None, block, d), lambda e, t: (e, t, 0))
    return pl.pallas_call(
        _expert_kernel,
        name="experts",
        grid=(n_e, seg // block),
        in_specs=[
            row_spec,
            pl.BlockSpec((None, d, EXPERT_DIM), lambda e, t: (e, 0, 0)),
            pl.BlockSpec((None, d, EXPERT_DIM), lambda e, t: (e, 0, 0)),
            pl.BlockSpec((None, EXPERT_DIM, d), lambda e, t: (e, 0, 0)),
        ],
        out_specs=[
            row_spec,
            pl.BlockSpec((None, d, EXPERT_DIM), lambda e, t: (e, 0, 0)),
            pl.BlockSpec((None, d, EXPERT_DIM), lambda e, t: (e, 0, 0)),
            pl.BlockSpec((None, EXPERT_DIM, d), lambda e, t: (e, 0, 0)),
        ],
        out_shape=[
            jax.ShapeDtypeStruct(xs.shape, BF16),
            jax.ShapeDtypeStruct(w1.shape, BF16),
            jax.ShapeDtypeStruct(w3.shape, BF16),
            jax.ShapeDtypeStruct(w2.shape, BF16),
        ],
        scratch_shapes=[
            pltpu.VMEM((d, EXPERT_DIM), BF16),
            pltpu.VMEM((d, EXPERT_DIM), BF16),
            pltpu.VMEM((EXPERT_DIM, d), BF16),
        ],
        compiler_params=pltpu.CompilerParams(
            dimension_semantics=("parallel", "arbitrary"),
            vmem_limit_bytes=V7X_VMEM_LIMIT_BYTES),
    )(xs, w1, w3, w2)


def _combine_kernel(ovn_ref, ovl_ref, x_ref, p_ref, rk_ref, wt_ref, ys_ref,
                    ws1_ref, ws3_ref, ws2_ref, wg_ref, bg_ref, wp_ref, g2_ref, b2_ref,
                    w1_hbm, w3_hbm, w2_hbm, o_ref, acc_s, w1_s, w3_s, w2_s, sems):
    j = pl.program_id(0)
    n_over = ovn_ref[j]
    pairs = COMBINE_SUBS * N_EXPERTS

    def fetch(k, half):
        e = ovl_ref[j * pairs + k] % N_EXPERTS
        return [pltpu.make_async_copy(src.at[e], dst.at[half], sems.at[half, i])
                for i, (src, dst) in enumerate(((w1_hbm, w1_s), (w3_hbm, w3_s), (w2_hbm, w2_s)))]

    @pl.when(n_over > 0)
    def _():
        for cp in fetch(0, 0):
            cp.start()

    x = x_ref[...]
    xb = x.astype(BF16)
    hs = _silu(jnp.dot(xb, ws1_ref[...], preferred_element_type=F32))
    hs = hs * jnp.dot(xb, ws3_ref[...], preferred_element_type=F32)
    shared = jnp.dot(hs.astype(BF16), ws2_ref[...], preferred_element_type=F32)
    gate = jax.nn.sigmoid(jnp.dot(xb, wg_ref[...], preferred_element_type=F32) + bg_ref[...])
    side = jnp.dot(p_ref[...].astype(BF16), wp_ref[...], preferred_element_type=F32)
    acc_s[...] = DEEPNORM_ALPHA * x + shared + gate * side

    for sb in range(COMBINE_SUBS):
        cols = slice(sb * ROUTE_SUB, (sb + 1) * ROUTE_SUB)
        for g in range(N_EXPERTS // SLOT_GROUP):
            scat = _slot_onehot(lambda e: rk_ref[e:e + 1, cols], g,
                                lambda e: wt_ref[e:e + 1, cols].astype(BF16))
            rows = ys_ref[g * SLOT_GROUP:(g + 1) * SLOT_GROUP, sb * SLOT_ROWS:(sb + 1) * SLOT_ROWS, :]
            rows = rows.reshape(SLOT_GROUP * SLOT_ROWS, ys_ref.shape[2])
            acc_s[cols, :] += lax.dot_general(scat, rows, TN_DIMS, preferred_element_type=F32)

    sub = lax.broadcasted_iota(jnp.int32, (ROUTE_SUB, ROUTE_SUB), 0)
    lan = lax.broadcasted_iota(jnp.int32, (ROUTE_SUB, ROUTE_SUB), 1)

    def overflow(k, carry):
        half = k % 2

        @pl.when(k + 1 < n_over)
        def _():
            for cp in fetch(k + 1, 1 - half):
                cp.start()

        pair = ovl_ref[j * pairs + k]
        sb, e = pair // N_EXPERTS, pair % N_EXPERTS
        rank = rk_ref[pl.ds(e, 1), :]
        late = jnp.where(rank >= SLOT_ROWS, wt_ref[pl.ds(e, 1), :], 0.0)
        late = sum(jnp.where(sb == i, late[:, i * ROUTE_SUB:(i + 1) * ROUTE_SUB], 0.0)
                   for i in range(COMBINE_SUBS))
        pick = jnp.where(sub == lan, jnp.broadcast_to(late, (ROUTE_SUB, ROUTE_SUB)), 0.0).astype(BF16)
        rows = pl.ds(pl.multiple_of(sb * ROUTE_SUB, ROUTE_SUB), ROUTE_SUB)
        for cp in fetch(k, half):
            cp.wait()
        y = _expert_ffn(x_ref[rows, :].astype(BF16), w1_s[half], w3_s[half], w2_s[half])
        acc_s[rows, :] += jnp.dot(pick, y, preferred_element_type=F32)
        return carry

    lax.fori_loop(0, n_over, overflow, 0)
    o_ref[...] = _layer_norm(acc_s[...], g2_ref[...], b2_ref[...])


def _combine(x1, p, rk, wt, ys, ov_n, ov_list, w1, w3, w2, ws1, ws3, ws2, wg, bg, wp, ln_g, ln_b):
    t, d = x1.shape
    tile = COMBINE_SUBS * ROUTE_SUB
    const = lambda shape: pl.BlockSpec(shape, lambda j, *_: (0,) * len(shape))
    hbm = pl.BlockSpec(memory_space=pl.ANY)
    return pl.pallas_call(
        _combine_kernel,
        name="combine",
        grid_spec=pltpu.PrefetchScalarGridSpec(
            num_scalar_prefetch=2,
            grid=(t // tile,),
            in_specs=[
                pl.BlockSpec((tile, d), lambda j, *_: (j, 0)),
                pl.BlockSpec((tile, PLE_DIM), lambda j, *_: (j, 0)),
                pl.BlockSpec((N_EXPERTS, tile), lambda j, *_: (0, j)),
                pl.BlockSpec((N_EXPERTS, tile), lambda j, *_: (0, j)),
                pl.BlockSpec((N_EXPERTS, COMBINE_SUBS * SLOT_ROWS, d), lambda j, *_: (0, j, 0)),
                const((d, SHARED_DIM)), const((d, SHARED_DIM)), const((SHARED_DIM, d)),
                const((d, d)), const((1, d)), const((PLE_DIM, d)), const((1, d)), const((1, d)),
                hbm, hbm, hbm,
            ],
            out_specs=pl.BlockSpec((tile, d), lambda j, *_: (j, 0)),
            scratch_shapes=[
                pltpu.VMEM((tile, d), F32),
                pltpu.VMEM((2, d, EXPERT_DIM), BF16),
                pltpu.VMEM((2, d, EXPERT_DIM), BF16),
                pltpu.VMEM((2, EXPERT_DIM, d), BF16),
                pltpu.SemaphoreType.DMA((2, 3)),
            ],
        ),
        out_shape=jax.ShapeDtypeStruct((t, d), F32),
        compiler_params=pltpu.CompilerParams(
            dimension_semantics=("arbitrary",),
            vmem_limit_bytes=V7X_VMEM_LIMIT_BYTES),
    )(ov_n, ov_list, x1, p, rk, wt, ys,
      ws1.astype(BF16), ws3.astype(BF16), ws2.astype(BF16),
      wg.astype(BF16), bg.reshape(1, -1), wp.astype(BF16), ln_g.reshape(1, -1), ln_b.reshape(1, -1),
      w1, w3, w2)


def _overflow_lists(cnt):
    pairs = COMBINE_SUBS * N_EXPERTS
    over = (cnt > SLOT_ROWS).reshape(-1, pairs)
    place = jnp.cumsum(over, axis=1) - 1
    slot = jnp.arange(pairs)
    hit = over[:, :, None] & (place[:, :, None] == slot[None, None, :])
    ov_list = jnp.sum(jnp.where(hit, slot[None, :, None], 0), axis=1)
    return jnp.sum(over, axis=1).astype(jnp.int32), ov_list.reshape(-1).astype(jnp.int32)


def _moe_block(x1, p, w_router, router_bias, w1, w3, w2, ws1, ws3, ws2, wg, bg, wp, ln_g, ln_b):
    t, d = x1.shape
    n_sub = t // ROUTE_SUB
    wt, rk, cnt, xs = _route_dispatch(x1, w_router, router_bias)
    per_step = ROUTE_TILE // ROUTE_SUB
    cnt = jnp.swapaxes(cnt[:, :, :per_step], 1, 2).reshape(n_sub, N_EXPERTS).astype(jnp.int32)
    ov_n, ov_list = _overflow_lists(cnt)
    ys, w1b, w3b, w2b = _experts(xs, w1, w3, w2)
    return _combine(x1, p, rk, wt, ys, ov_n, ov_list, w1b, w3b, w2b, ws1, ws3, ws2, wg, bg, wp, ln_g, ln_b)


def kernel(x, p, w_in, ret_gn_gain, attn_scale, sinks, w_out, ln1_g, ln1_b, w_router, router_bias,
           w1, w3, w2, ws1, ws3, ws2, w_ple_gate, b_ple_gate, w_ple_proj, ln2_g, ln2_b):
    b, s, d = x.shape
    h = x
    for i in range(DEPTH):
        h = _mixer(h, w_in[i], w_out[i], ret_gn_gain[i], attn_scale[i], sinks[i], ln1_g[i], ln1_b[i])
        h2 = _moe_block(h.reshape(b * s, d), p[i].reshape(b * s, -1), w_router[i], router_bias[i],
                        w1[i], w3[i], w2[i], ws1[i], ws3[i], ws2[i],
                        w_ple_gate[i], b_ple_gate[i], w_ple_proj[i], ln2_g[i], ln2_b[i])
        h = h2.reshape(b, s, d)
    return h
```

```python
import functools

import jax
import jax.numpy as jnp
from jax import lax
from jax.experimental import pallas as pl
from jax.experimental.pallas import tpu as pltpu

F32 = jnp.float32
BF16 = jnp.bfloat16

D_MODEL = 1024
HEAD_DIM = 64
RET_HEADS = 8
SWA_Q_HEADS = 8
SWA_KV_HEADS = 2
SWA_GROUP = SWA_Q_HEADS // SWA_KV_HEADS
RET_WIDTH = RET_HEADS * HEAD_DIM
SWA_WIDTH = SWA_Q_HEADS * HEAD_DIM
KV_WIDTH = SWA_KV_HEADS * HEAD_DIM
MIX_WIDTH = RET_WIDTH + SWA_WIDTH
IN_WIDTH = 4 * RET_WIDTH + SWA_WIDTH + 2 * KV_WIDTH
CHUNK = 128
WINDOW = 128
RET_THETA = 10000.0
SWA_THETA = 500000.0
SWA_ROT_DIM = HEAD_DIM // 4
N_EXPERTS = 64
TOP_K = 8
N_GROUPS = 8
GROUP_SIZE = N_EXPERTS // N_GROUPS
TOPK_GROUPS = 4
EXPERT_DIM = 256
SHARED_DIM = 256
ROUTED_SCALE = 2.5
PLE_DIM = 256
LN_EPS = 1e-5
GN_EPS = 1e-6
NEG_INF = -1e30
DEPTH = 1
DEEPNORM_ALPHA = (2.0 * DEPTH) ** 0.25

OFF_RQ = 0
OFF_RK = RET_WIDTH
OFF_RV = 2 * RET_WIDTH
OFF_RG = 3 * RET_WIDTH
OFF_AQ = 4 * RET_WIDTH
OFF_AK = OFF_AQ + SWA_WIDTH
OFF_AV = OFF_AK + KV_WIDTH

V7X_LANES = 128
V7X_VMEM_LIMIT_BYTES = 56 * 1024 * 1024

MIX_CHUNKS = 4
MIX_TILE = MIX_CHUNKS * CHUNK
ROUTE_SUB = 256
ROUTE_TILE = 512
SLOT_ROWS = 48
SLOT_GROUP = 16
COMBINE_SUBS = 2
EXPERT_TILE = 512
EXPERT_BLOCK = 3072

NT_DIMS = (((1,), (1,)), ((), ()))
TN_DIMS = (((0,), (0,)), ((), ()))


def _layer_norm(y, g, b):
    mu = jnp.mean(y, axis=-1, keepdims=True)
    d = y - mu
    var = jnp.mean(d * d, axis=-1, keepdims=True)
    return d * lax.rsqrt(var + LN_EPS) * g + b


def _silu(v):
    return v * jax.nn.sigmoid(v)


def _rope_inplace(buf, col0, n_groups, cos, sin, near, shift, scale):
    for g in range(n_groups):
        cols = slice(col0 + g * V7X_LANES, col0 + (g + 1) * V7X_LANES)
        xg = buf[:, cols]
        partner = jnp.where(near, pltpu.roll(xg, V7X_LANES - shift, 1), pltpu.roll(xg, shift, 1))
        y = xg * cos + partner * sin
        if scale != 1.0:
            y = y * scale
        buf[:, cols] = y


def _head_mean(y, head_avg):
    hi = y.astype(BF16)
    lo = (y - hi.astype(F32)).astype(BF16)
    return jnp.dot(jnp.concatenate([hi, lo], axis=1), head_avg, preferred_element_type=F32)


def _mixer_kernel(x_ref, win_ref, wout_ref, cr_ref, sr_ref, ca_ref, sa_ref,
                  dec_ref, kw_ref, qw_ref, cd_ref, gn_ref, asc_ref, sink_ref,
                  g1_ref, b1_ref, o_ref, proj_s, mix_s, state_s, kprev_s, vprev_s):
    step = pl.program_id(1)

    @pl.when(step == 0)
    def _():
        state_s[...] = jnp.zeros_like(state_s)
        kprev_s[...] = jnp.zeros_like(kprev_s)
        vprev_s[...] = jnp.zeros_like(vprev_s)

    proj_s[...] = jnp.dot(x_ref[...].astype(BF16), win_ref[...], preferred_element_type=F32)

    lane = lax.broadcasted_iota(jnp.int32, (MIX_TILE, V7X_LANES), 1) % HEAD_DIM
    scale = HEAD_DIM ** -0.5
    cr, sr = cr_ref[...], sr_ref[...]
    _rope_inplace(proj_s, OFF_RQ, RET_WIDTH // V7X_LANES, cr, sr, lane < HEAD_DIM // 2, HEAD_DIM // 2, 1.0)
    _rope_inplace(proj_s, OFF_RK, RET_WIDTH // V7X_LANES, cr, sr, lane < HEAD_DIM // 2, HEAD_DIM // 2, scale)
    ca, sa = ca_ref[...], sa_ref[...]
    _rope_inplace(proj_s, OFF_AQ, SWA_WIDTH // V7X_LANES, ca, sa, lane < SWA_ROT_DIM // 2, SWA_ROT_DIM // 2, scale)
    _rope_inplace(proj_s, OFF_AK, KV_WIDTH // V7X_LANES, ca, sa, lane < SWA_ROT_DIM // 2, SWA_ROT_DIM // 2, 1.0)

    qi = lax.broadcasted_iota(jnp.int32, (CHUNK, 2 * CHUNK), 0)
    kj = lax.broadcasted_iota(jnp.int32, (CHUNK, 2 * CHUNK), 1)
    rel = CHUNK + qi - kj
    in_window = (rel >= 0) & (rel < WINDOW)

    sub = lax.broadcasted_iota(jnp.int32, (CHUNK, V7X_LANES), 0)
    lan = lax.broadcasted_iota(jnp.int32, (CHUNK, V7X_LANES), 1)
    low = lan < HEAD_DIM
    same_head = (sub // HEAD_DIM) == (lan // HEAD_DIM)
    head_avg = jnp.where(same_head, 1.0 / HEAD_DIM, 0.0).astype(BF16)
    head_avg = jnp.concatenate([head_avg, head_avg], axis=0)

    def chunk_stages(c):
        rows = slice(c * CHUNK, (c + 1) * CHUNK)

        valid = in_window & (kj >= jnp.where(step == 0, CHUNK, 0)) if c == 0 else in_window
        lanes = lambda off, p: slice(off + p * V7X_LANES, off + (p + 1) * V7X_LANES)

        q2s, v2s, s_ret, kvs = [], [], [], []
        for p in range(RET_HEADS // 2):
            q2 = proj_s[rows, lanes(OFF_RQ, p)]
            k2 = proj_s[rows, lanes(OFF_RK, p)]
            v2 = proj_s[rows, lanes(OFF_RV, p)]
            q_ab = jnp.concatenate([jnp.where(low, q2, 0.0), jnp.where(low, 0.0, q2)], axis=0).astype(BF16)
            s_ret.append(lax.dot_general(q_ab, k2.astype(BF16), NT_DIMS, preferred_element_type=F32))
            kvs.append(lax.dot_general((k2 * kw_ref[p]).astype(BF16), v2.astype(BF16), TN_DIMS,
                                       preferred_element_type=F32))
            q2s.append(q2)
            v2s.append(v2)

        kc = proj_s[rows, OFF_AK:OFF_AK + KV_WIDTH]
        vc = proj_s[rows, OFF_AV:OFF_AV + KV_WIDTH]
        kband = jnp.concatenate([kprev_s[...], kc], axis=0)
        vband = jnp.concatenate([vprev_s[...], vc], axis=0)
        kprev_s[...] = kc
        vprev_s[...] = vc
        kband_r = pltpu.roll(kband, HEAD_DIM, 1)
        vband_r = pltpu.roll(vband, HEAD_DIM, 1)
        low2 = jnp.concatenate([low, low], axis=0)
        s_att, v_cats = [], []
        for g in range(SWA_KV_HEADS):
            k_even, k_odd = (kband, kband_r) if g == 0 else (kband_r, kband)
            v_low, v_high = (vband, vband_r) if g == 0 else (vband_r, vband)
            v_cats.append(jnp.concatenate([jnp.where(low2, v_low, 0.0), jnp.where(low2, 0.0, v_high)],
                                          axis=0).astype(BF16))
            q_off = OFF_AQ + g * SWA_GROUP * HEAD_DIM
            qt0 = proj_s[rows, q_off:q_off + V7X_LANES]
            qt1 = proj_s[rows, q_off + V7X_LANES:q_off + 2 * V7X_LANES]
            q_even = jnp.concatenate([jnp.where(low, qt0, 0.0), jnp.where(low, qt1, 0.0)], axis=0)
            q_odd = jnp.concatenate([jnp.where(low, 0.0, qt0), jnp.where(low, 0.0, qt1)], axis=0)
            s_even = lax.dot_general(q_even.astype(BF16), k_even.astype(BF16), NT_DIMS,
                                     preferred_element_type=F32)
            s_odd = lax.dot_general(q_odd.astype(BF16), k_odd.astype(BF16), NT_DIMS,
                                    preferred_element_type=F32)
            s_att.append((s_even[:CHUNK], s_odd[:CHUNK]))
            s_att.append((s_even[CHUNK:], s_odd[CHUNK:]))
        yield

        ys = []
        for p in range(RET_HEADS // 2):
            s_a = (s_ret[p][:CHUNK] * dec_ref[2 * p]).astype(BF16)
            s_b = (s_ret[p][CHUNK:] * dec_ref[2 * p + 1]).astype(BF16)
            st = state_s[p]
            lhs = jnp.concatenate([s_a, s_b, (q2s[p] * qw_ref[p]).astype(BF16)], axis=1)
            rhs = jnp.concatenate([jnp.where(low, v2s[p], 0.0), jnp.where(low, 0.0, v2s[p]), st],
                                  axis=0).astype(BF16)
            ys.append(jnp.dot(lhs, rhs, preferred_element_type=F32))
            state_s[p] = st * cd_ref[p] + jnp.where(same_head, kvs[p], 0.0)

        def soft(s, h):
            s = jnp.where(valid, s, NEG_INF)
            sink = sink_ref[h]
            m = jnp.maximum(jnp.max(s, axis=-1, keepdims=True), sink)
            e = jnp.exp(s - m)
            den = jnp.sum(e, axis=-1, keepdims=True) + jnp.exp(sink - m)
            return e.astype(BF16), den

        es, dens = [], []
        for t, (sa, sb) in enumerate(s_att):
            e_a, den_a = soft(sa, 2 * t)
            e_b, den_b = soft(sb, 2 * t + 1)
            es.append(jnp.concatenate([e_a, e_b], axis=1))
            dens.append(jnp.where(low, den_a, den_b))
        yield

        mus = [_head_mean(y, head_avg) for y in ys]
        for t in range(SWA_Q_HEADS // 2):
            o = jnp.dot(es[t], v_cats[t // (SWA_GROUP // 2)], preferred_element_type=F32) / dens[t]
            mix_s[rows, lanes(RET_WIDTH, t)] = o * asc_ref[:, lanes(0, t)]
        yield

        ds = [y - mu for y, mu in zip(ys, mus)]
        variances = [_head_mean(d * d, head_avg) for d in ds]
        for p in range(RET_HEADS // 2):
            yn = ds[p] * lax.rsqrt(variances[p] + GN_EPS)
            gate = proj_s[rows, lanes(OFF_RG, p)]
            mix_s[rows, lanes(0, p)] = _silu(gate) * (yn * gn_ref[:, lanes(0, p)])

    chunks = [chunk_stages(c) for c in range(MIX_CHUNKS)]
    for c in range(MIX_CHUNKS):
        next(chunks[c])
        if c > 0:
            next(chunks[c - 1], None)
        next(chunks[c])
        next(chunks[c])
    next(chunks[-1], None)

    y = jnp.dot(mix_s[...].astype(BF16), wout_ref[...], preferred_element_type=F32)
    y = y + DEEPNORM_ALPHA * x_ref[...]
    o_ref[...] = _layer_norm(y, g1_ref[...], b1_ref[...])


def _rope_tables(seq, rot_dim, theta):
    half = rot_dim // 2
    inv_freq = theta ** (-jnp.arange(half, dtype=F32) / half)
    ang = jnp.arange(seq, dtype=F32)[:, None] * inv_freq[None, :]
    cos, sin = jnp.cos(ang), jnp.sin(ang)
    rest = HEAD_DIM - rot_dim
    cos_h = jnp.concatenate([cos, cos, jnp.ones((seq, rest), F32)], axis=1)
    sin_h = jnp.concatenate([-sin, sin, jnp.zeros((seq, rest), F32)], axis=1)
    reps = V7X_LANES // HEAD_DIM
    return jnp.tile(cos_h, (1, reps)), jnp.tile(sin_h, (1, reps))


def _retention_tables():
    h = RET_HEADS
    log_g = jnp.log1p(-(2.0 ** (-5.0 - jnp.arange(h, dtype=F32))))
    idx = jnp.arange(CHUNK, dtype=F32)
    diff = idx[:, None] - idx[None, :]
    intra = jnp.where(diff[None] >= 0, jnp.exp(jnp.maximum(diff, 0.0)[None] * log_g[:, None, None]), 0.0)
    k_w = jnp.exp((CHUNK - 1 - idx)[:, None] * log_g[None, :])
    q_w = jnp.exp((idx + 1)[:, None] * log_g[None, :])
    chunk_decay = jnp.exp(CHUNK * log_g)
    pairs = lambda t: jnp.repeat(t.T.reshape(h // 2, 2, -1), HEAD_DIM, axis=1).transpose(0, 2, 1)
    return intra, pairs(k_w), pairs(q_w), pairs(chunk_decay[None, :])


def _mixer(x, w_in, w_out, gn_gain, attn_scale, sinks, ln_g, ln_b):
    b, s, d = x.shape
    cr, sr = _rope_tables(s, HEAD_DIM, RET_THETA)
    ca, sa = _rope_tables(s, SWA_ROT_DIM, SWA_THETA)
    dec, kw, qw, cd = _retention_tables()
    const = lambda shape: pl.BlockSpec(shape, lambda bi, i: (0,) * len(shape))
    tab = pl.BlockSpec((MIX_TILE, V7X_LANES), lambda bi, i: (i, 0))
    smem = pl.BlockSpec(memory_space=pltpu.SMEM)
    return pl.pallas_call(
        _mixer_kernel,
        name="mixer",
        grid=(b, s // MIX_TILE),
        in_specs=[
            pl.BlockSpec((None, MIX_TILE, d), lambda bi, i: (bi, i, 0)),
            const((d, IN_WIDTH)), const((MIX_WIDTH, d)),
            tab, tab, tab, tab,
            const((RET_HEADS, CHUNK, CHUNK)), const((RET_HEADS // 2, CHUNK, V7X_LANES)),
            const((RET_HEADS // 2, CHUNK, V7X_LANES)), const((RET_HEADS // 2, 1, V7X_LANES)),
            const((1, RET_WIDTH)), const((1, SWA_WIDTH)), smem,
            const((1, d)), const((1, d)),
        ],
        out_specs=pl.BlockSpec((None, MIX_TILE, d), lambda bi, i: (bi, i, 0)),
        out_shape=jax.ShapeDtypeStruct((b, s, d), F32),
        scratch_shapes=[
            pltpu.VMEM((MIX_TILE, IN_WIDTH), F32),
            pltpu.VMEM((MIX_TILE, MIX_WIDTH), F32),
            pltpu.VMEM((RET_HEADS // 2, V7X_LANES, V7X_LANES), F32),
            pltpu.VMEM((CHUNK, KV_WIDTH), F32),
            pltpu.VMEM((CHUNK, KV_WIDTH), F32),
        ],
        compiler_params=pltpu.CompilerParams(
            dimension_semantics=("parallel", "arbitrary"),
            vmem_limit_bytes=V7X_VMEM_LIMIT_BYTES),
    )(x, w_in.astype(BF16), w_out.astype(BF16), cr, sr, ca, sa, dec, kw, qw, cd,
      gn_gain.reshape(1, -1), attn_scale.reshape(1, -1), sinks, ln_g.reshape(1, -1), ln_b.reshape(1, -1))


def _route(x, wr_ref, bias_ref, tri_ref, between):
    tt = x.shape[0]
    xh = x.astype(BF16)
    xl = (x - xh.astype(F32)).astype(BF16)
    w = wr_ref[...]
    wh = w.astype(BF16)
    wl = (w - wh.astype(F32)).astype(BF16)
    dot = functools.partial(jnp.dot, preferred_element_type=F32)
    hi = dot(xh, jnp.concatenate([wh, wl], axis=1))
    logits = (hi[:, :N_EXPERTS] + (dot(xl, wh) + hi[:, N_EXPERTS:])).T
    scores = jax.nn.sigmoid(logits)
    biased = scores + bias_ref[...]

    shape3 = (N_GROUPS, GROUP_SIZE, tt)
    grp = lax.broadcasted_iota(jnp.int32, shape3, 0)
    mem = lax.broadcasted_iota(jnp.int32, shape3, 1)
    b3 = biased.reshape(shape3)

    m1 = jnp.max(b3, axis=1, keepdims=True)
    i1 = jnp.min(jnp.where(b3 == m1, mem, GROUP_SIZE), axis=1, keepdims=True)
    m2 = jnp.max(jnp.where(mem == i1, -jnp.inf, b3), axis=1, keepdims=True)
    gs = jnp.broadcast_to(m1 + m2, shape3)

    gmask = jnp.zeros(shape3, F32)
    for _ in range(TOPK_GROUPS):
        m = jnp.max(gs, axis=0, keepdims=True)
        pick = grp == jnp.min(jnp.where(gs == m, grp, N_GROUPS), axis=0, keepdims=True)
        gmask = jnp.where(pick, 1.0, gmask)
        gs = jnp.where(pick, -jnp.inf, gs)
        between()

    eid = grp * GROUP_SIZE + mem
    cand = jnp.where(gmask > 0.0, b3, NEG_INF)
    sel = jnp.zeros(shape3, F32)
    for _ in range(TOP_K):
        m = jnp.max(jnp.max(cand, axis=0, keepdims=True), axis=1, keepdims=True)
        first = jnp.where(cand == m, eid, N_EXPERTS)
        first = jnp.min(jnp.min(first, axis=0, keepdims=True), axis=1, keepdims=True)
        pick = eid == first
        sel = jnp.where(pick, 1.0, sel)
        cand = jnp.where(pick, -jnp.inf, cand)
        between()

    sel = sel.reshape(N_EXPERTS, tt)
    wsel = scores * sel
    comb = wsel / jnp.sum(wsel, axis=0, keepdims=True) * ROUTED_SCALE

    lane = lax.broadcasted_iota(jnp.int32, (N_EXPERTS, V7X_LANES), 1)
    cnt = jnp.zeros((N_EXPERTS, V7X_LANES), F32)
    ranks = []
    for sb in range(tt // ROUTE_SUB):
        cols = slice(sb * ROUTE_SUB, (sb + 1) * ROUTE_SUB)
        sel_sb = sel[:, cols]
        rank = jnp.dot(sel_sb.astype(BF16), tri_ref[...], preferred_element_type=F32)
        ranks.append(jnp.where(sel_sb > 0.0, rank, -1.0))
        cnt = jnp.where(lane == sb, jnp.sum(sel_sb, axis=1, keepdims=True), cnt)
    return comb, jnp.concatenate(ranks, axis=1), cnt


def _route_dispatch_kernel(x_ref, wr_ref, bias_ref, tri_ref, wt_ref, rk_ref, cnt_ref, xs_ref, xb_s, rk_s):
    step = pl.program_id(0)

    @pl.when(step == 0)
    def _():
        xb_s[...] = jnp.zeros_like(xb_s)
        rk_s[...] = jnp.full(rk_s.shape, -1.0, F32)

    prev = (step + 1) % 2
    one = jnp.ones((), BF16)

    def place(sb, g):
        cols = slice(sb * ROUTE_SUB, (sb + 1) * ROUTE_SUB)
        onehot = _slot_onehot(lambda e: rk_s[prev, e:e + 1, cols], g, lambda e: one)
        rows = jnp.dot(onehot, xb_s[prev, cols, :], preferred_element_type=F32).astype(BF16)
        xs_ref[g * SLOT_GROUP:(g + 1) * SLOT_GROUP, sb * SLOT_ROWS:(sb + 1) * SLOT_ROWS, :] = (
            rows.reshape(SLOT_GROUP, SLOT_ROWS, rows.shape[1]))

    units = iter([functools.partial(place, sb, g) for sb in range(ROUTE_TILE // ROUTE_SUB)
                  for g in range(N_EXPERTS // SLOT_GROUP)])
    x = x_ref[...]
    comb, ranks, cnt = _route(x, wr_ref, bias_ref, tri_ref, lambda: next(units, lambda: None)())
    for unit in units:
        unit()

    wt_ref[...] = comb
    rk_ref[...] = ranks
    cnt_ref[...] = cnt
    xb_s[step % 2] = x.astype(BF16)
    rk_s[step % 2] = ranks


def _route_dispatch(x1, w_router, router_bias):
    t, d = x1.shape
    n_tiles = t // ROUTE_TILE
    tile_rows = ROUTE_TILE // ROUTE_SUB * SLOT_ROWS
    tri = (jnp.arange(ROUTE_SUB)[:, None] < jnp.arange(ROUTE_SUB)[None, :]).astype(BF16)
    routed = lambda s: jnp.minimum(s, n_tiles - 1)
    placed = lambda s: jnp.maximum(s - 1, 0)
    return pl.pallas_call(
        _route_dispatch_kernel,
        name="route_dispatch",
        grid=(n_tiles + 1,),
        in_specs=[
            pl.BlockSpec((ROUTE_TILE, d), lambda s: (routed(s), 0)),
            pl.BlockSpec((d, N_EXPERTS), lambda s: (0, 0)),
            pl.BlockSpec((N_EXPERTS, 1), lambda s: (0, 0)),
            pl.BlockSpec((ROUTE_SUB, ROUTE_SUB), lambda s: (0, 0)),
        ],
        out_specs=[
            pl.BlockSpec((N_EXPERTS, ROUTE_TILE), lambda s: (0, routed(s))),
            pl.BlockSpec((N_EXPERTS, ROUTE_TILE), lambda s: (0, routed(s))),
            pl.BlockSpec((None, N_EXPERTS, V7X_LANES), lambda s: (routed(s), 0, 0)),
            pl.BlockSpec((N_EXPERTS, tile_rows, d), lambda s: (0, placed(s), 0)),
        ],
        out_shape=[
            jax.ShapeDtypeStruct((N_EXPERTS, t), F32),
            jax.ShapeDtypeStruct((N_EXPERTS, t), F32),
            jax.ShapeDtypeStruct((n_tiles, N_EXPERTS, V7X_LANES), F32),
            jax.ShapeDtypeStruct((N_EXPERTS, n_tiles * tile_rows, d), BF16),
        ],
        scratch_shapes=[
            pltpu.VMEM((2, ROUTE_TILE, d), BF16),
            pltpu.VMEM((2, N_EXPERTS, ROUTE_TILE), F32),
        ],
        compiler_params=pltpu.CompilerParams(
            dimension_semantics=("arbitrary",),
            vmem_limit_bytes=V7X_VMEM_LIMIT_BYTES),
    )(x1, w_router, router_bias.reshape(-1, 1), tri)


def _slot_onehot(rank_row, group, values):
    rid = lax.broadcasted_iota(jnp.int32, (SLOT_ROWS, ROUTE_SUB), 0).astype(F32).astype(BF16)
    pieces = []
    for i in range(SLOT_GROUP):
        e = group * SLOT_GROUP + i
        pieces.append(jnp.where(rid == rank_row(e).astype(BF16), values(e), jnp.zeros((), BF16)))
    return jnp.concatenate(pieces, axis=0)


def _expert_ffn(x, w1, w3, w2):
    h = _silu(jnp.dot(x, w1, preferred_element_type=F32))
    h = h * jnp.dot(x, w3, preferred_element_type=F32)
    return jnp.dot(h.astype(BF16), w2, preferred_element_type=F32).astype(BF16)


def _expert_kernel(xs_ref, w1_ref, w3_ref, w2_ref, ys_ref, w1b_ref, w3b_ref, w2b_ref, w1_s, w3_s, w2_s):
    @pl.when(pl.program_id(1) == 0)
    def _():
        for src, dst, out in ((w1_ref, w1_s, w1b_ref), (w3_ref, w3_s, w3b_ref), (w2_ref, w2_s, w2b_ref)):
            wb = src[...].astype(BF16)
            dst[...] = wb
            out[...] = wb

    block = xs_ref.shape[0]
    tile = EXPERT_TILE if block % EXPERT_TILE == 0 else block
    for q in range(block // tile):
        rows = slice(q * tile, (q + 1) * tile)
        ys_ref[rows, :] = _expert_ffn(xs_ref[rows, :], w1_s[...], w3_s[...], w2_s[...])


def _experts(xs, w1, w3, w2):
    n_e, seg, d = xs.shape
    block = EXPERT_BLOCK if seg % EXPERT_BLOCK == 0 else seg
    row_spec = pl.BlockSpec((None, block, d), lambda e, t: (e, t, 0))
    return pl.pallas_call(
        _expert_kernel,
        name="experts",
        grid=(n_e, seg // block),
        in_specs=[
            row_spec,
            pl.BlockSpec((None, d, EXPERT_DIM), lambda e, t: (e, 0, 0)),
            pl.BlockSpec((None, d, EXPERT_DIM), lambda e, t: (e, 0, 0)),
            pl.BlockSpec((None, EXPERT_DIM, d), lambda e, t: (e, 0, 0)),
        ],
        out_specs=[
            row_spec,
            pl.BlockSpec((None, d, EXPERT_DIM), lambda e, t: (e, 0, 0)),
            pl.BlockSpec((None, d, EXPERT_DIM), lambda e, t: (e, 0, 0)),
            pl.BlockSpec((None, EXPERT_DIM, d), lambda e, t: (e, 0, 0)),
        ],
        out_shape=[
            jax.ShapeDtypeStruct(xs.shape, BF16),
            jax.ShapeDtypeStruct(w1.shape, BF16),
            jax.ShapeDtypeStruct(w3.shape, BF16),
            jax.ShapeDtypeStruct(w2.shape, BF16),
        ],
        scratch_shapes=[
            pltpu.VMEM((d, EXPERT_DIM), BF16),
            pltpu.VMEM((d, EXPERT_DIM), BF16),
            pltpu.VMEM((EXPERT_DIM, d), BF16),
        ],
        compiler_params=pltpu.CompilerParams(
            dimension_semantics=("parallel", "arbitrary"),
            vmem_limit_bytes=V7X_VMEM_LIMIT_BYTES),
    )(xs, w1, w3, w2)


def _combine_kernel(ovn_ref, ovl_ref, x_ref, p_ref, rk_ref, wt_ref, ys_ref,
                    ws1_ref, ws3_ref, ws2_ref, wg_ref, bg_ref, wp_ref, g2_ref, b2_ref,
                    w1_hbm, w3_hbm, w2_hbm, o_ref, acc_s, w1_s, w3_s, w2_s, sems):
    j = pl.program_id(0)
    n_over = ovn_ref[j]
    pairs = COMBINE_SUBS * N_EXPERTS

    def fetch(k, half):
        e = ovl_ref[j * pairs + k] % N_EXPERTS
        return [pltpu.make_async_copy(src.at[e], dst.at[half], sems.at[half, i])
                for i, (src, dst) in enumerate(((w1_hbm, w1_s), (w3_hbm, w3_s), (w2_hbm, w2_s)))]

    @pl.when(n_over > 0)
    def _():
        for cp in fetch(0, 0):
            cp.start()

    x = x_ref[...]
    xb = x.astype(BF16)
    hs = _silu(jnp.dot(xb, ws1_ref[...], preferred_element_type=F32))
    hs = hs * jnp.dot(xb, ws3_ref[...], preferred_element_type=F32)
    shared = jnp.dot(hs.astype(BF16), ws2_ref[...], preferred_element_type=F32)
    gate = jax.nn.sigmoid(jnp.dot(xb, wg_ref[...], preferred_element_type=F32) + bg_ref[...])
    side = jnp.dot(p_ref[...].astype(BF16), wp_ref[...], preferred_element_type=F32)
    acc_s[...] = DEEPNORM_ALPHA * x + shared + gate * side

    for sb in range(COMBINE_SUBS):
        cols = slice(sb * ROUTE_SUB, (sb + 1) * ROUTE_SUB)
        for g in range(N_EXPERTS // SLOT_GROUP):
            scat = _slot_onehot(lambda e: rk_ref[e:e + 1, cols], g,
                                lambda e: wt_ref[e:e + 1, cols].astype(BF16))
            rows = ys_ref[g * SLOT_GROUP:(g + 1) * SLOT_GROUP, sb * SLOT_ROWS:(sb + 1) * SLOT_ROWS, :]
            rows = rows.reshape(SLOT_GROUP * SLOT_ROWS, ys_ref.shape[2])
            acc_s[cols, :] += lax.dot_general(scat, rows, TN_DIMS, preferred_element_type=F32)

    sub = lax.broadcasted_iota(jnp.int32, (ROUTE_SUB, ROUTE_SUB), 0)
    lan = lax.broadcasted_iota(jnp.int32, (ROUTE_SUB, ROUTE_SUB), 1)

    def overflow(k, carry):
        half = k % 2

        @pl.when(k + 1 < n_over)
        def _():
            for cp in fetch(k + 1, 1 - half):
                cp.start()

        pair = ovl_ref[j * pairs + k]
        sb, e = pair // N_EXPERTS, pair % N_EXPERTS
        rank = rk_ref[pl.ds(e, 1), :]
        late = jnp.where(rank >= SLOT_ROWS, wt_ref[pl.ds(e, 1), :], 0.0)
        late = sum(jnp.where(sb == i, late[:, i * ROUTE_SUB:(i + 1) * ROUTE_SUB], 0.0)
                   for i in range(COMBINE_SUBS))
        pick = jnp.where(sub == lan, jnp.broadcast_to(late, (ROUTE_SUB, ROUTE_SUB)), 0.0).astype(BF16)
        rows = pl.ds(pl.multiple_of(sb * ROUTE_SUB, ROUTE_SUB), ROUTE_SUB)
        for cp in fetch(k, half):
            cp.wait()
        y = _expert_ffn(x_ref[rows, :].astype(BF16), w1_s[half], w3_s[half], w2_s[half])
        acc_s[rows, :] += jnp.dot(pick, y, preferred_element_type=F32)
        return carry

    lax.fori_loop(0, n_over, overflow, 0)
    o_ref[...] = _layer_norm(acc_s[...], g2_ref[...], b2_ref[...])


def _combine(x1, p, rk, wt, ys, ov_n, ov_list, w1, w3, w2, ws1, ws3, ws2, wg, bg, wp, ln_g, ln_b):
    t, d = x1.shape
    tile = COMBINE_SUBS * ROUTE_SUB
    const = lambda shape: pl.BlockSpec(shape, lambda j, *_: (0,) * len(shape))
    hbm = pl.BlockSpec(memory_space=pl.ANY)
    return pl.pallas_call(
        _combine_kernel,
        name="combine",
        grid_spec=pltpu.PrefetchScalarGridSpec(
            num_scalar_prefetch=2,
            grid=(t // tile,),
            in_specs=[
                pl.BlockSpec((tile, d), lambda j, *_: (j, 0)),
                pl.BlockSpec((tile, PLE_DIM), lambda j, *_: (j, 0)),
                pl.BlockSpec((N_EXPERTS, tile), lambda j, *_: (0, j)),
                pl.BlockSpec((N_EXPERTS, tile), lambda j, *_: (0, j)),
                pl.BlockSpec((N_EXPERTS, COMBINE_SUBS * SLOT_ROWS, d), lambda j, *_: (0, j, 0)),
                const((d, SHARED_DIM)), const((d, SHARED_DIM)), const((SHARED_DIM, d)),
                const((d, d)), const((1, d)), const((PLE_DIM, d)), const((1, d)), const((1, d)),
                hbm, hbm, hbm,
            ],
            out_specs=pl.BlockSpec((tile, d), lambda j, *_: (j, 0)),
            scratch_shapes=[
                pltpu.VMEM((tile, d), F32),
                pltpu.VMEM((2, d, EXPERT_DIM), BF16),
                pltpu.VMEM((2, d, EXPERT_DIM), BF16),
                pltpu.VMEM((2, EXPERT_DIM, d), BF16),
                pltpu.SemaphoreType.DMA((2, 3)),
            ],
        ),
        out_shape=jax.ShapeDtypeStruct((t, d), F32),
        compiler_params=pltpu.CompilerParams(
            dimension_semantics=("arbitrary",),
            vmem_limit_bytes=V7X_VMEM_LIMIT_BYTES),
    )(ov_n, ov_list, x1, p, rk, wt, ys,
      ws1.astype(BF16), ws3.astype(BF16), ws2.astype(BF16),
      wg.astype(BF16), bg.reshape(1, -1), wp.astype(BF16), ln_g.reshape(1, -1), ln_b.reshape(1, -1),
      w1, w3, w2)


def _overflow_lists(cnt):
    pairs = COMBINE_SUBS * N_EXPERTS
    over = (cnt > SLOT_ROWS).reshape(-1, pairs)
    place = jnp.cumsum(over, axis=1) - 1
    slot = jnp.arange(pairs)
    hit = over[:, :, None] & (place[:, :, None] == slot[None, None, :])
    ov_list = jnp.sum(jnp.where(hit, slot[None, :, None], 0), axis=1)
    return jnp.sum(over, axis=1).astype(jnp.int32), ov_list.reshape(-1).astype(jnp.int32)


def _moe_block(x1, p, w_router, router_bias, w1, w3, w2, ws1, ws3, ws2, wg, bg, wp, ln_g, ln_b):
    t, d = x1.shape
    n_sub = t // ROUTE_SUB
    wt, rk, cnt, xs = _route_dispatch(x1, w_router, router_bias)
    per_step = ROUTE_TILE // ROUTE_SUB
    cnt = jnp.swapaxes(cnt[:, :, :per_step], 1, 2).reshape(n_sub, N_EXPERTS).astype(jnp.int32)
    ov_n, ov_list = _overflow_lists(cnt)
    ys, w1b, w3b, w2b = _experts(xs, w1, w3, w2)
    return _combine(x1, p, rk, wt, ys, ov_n, ov_list, w1b, w3b, w2b, ws1, ws3, ws2, wg, bg, wp, ln_g, ln_b)


def kernel(x, p, w_in, ret_gn_gain, attn_scale, sinks, w_out, ln1_g, ln1_b, w_router, router_bias,
           w1, w3, w2, ws1, ws3, ws2, w_ple_gate, b_ple_gate, w_ple_proj, ln2_g, ln2_b):
    b, s, d = x.shape
    h = x
    for i in range(DEPTH):
        h = _mixer(h, w_in[i], w_out[i], ret_gn_gain[i], attn_scale[i], sinks[i], ln1_g[i], ln1_b[i])
        h2 = _moe_block(h.reshape(b * s, d), p[i].reshape(b * s, -1), w_router[i], router_bias[i],
                        w1[i], w3[i], w2[i], ws1[i], ws3[i], ws2[i],
                        w_ple_gate[i], b_ple_gate[i], w_ple_proj[i], ln2_g[i], ln2_b[i])
        h = h2.reshape(b, s, d)
    return h
```

```python
import functools

import jax
import jax.numpy as jnp
from jax import lax
from jax.experimental import pallas as pl
from jax.experimental.pallas import tpu as pltpu

F32 = jnp.float32
BF16 = jnp.bfloat16

D_MODEL = 1024
HEAD_DIM = 64
RET_HEADS = 8
SWA_Q_HEADS = 8
SWA_KV_HEADS = 2
SWA_GROUP = SWA_Q_HEADS // SWA_KV_HEADS
RET_WIDTH = RET_HEADS * HEAD_DIM
SWA_WIDTH = SWA_Q_HEADS * HEAD_DIM
KV_WIDTH = SWA_KV_HEADS * HEAD_DIM
MIX_WIDTH = RET_WIDTH + SWA_WIDTH
IN_WIDTH = 4 * RET_WIDTH + SWA_WIDTH + 2 * KV_WIDTH
CHUNK = 128
WINDOW = 128
RET_THETA = 10000.0
SWA_THETA = 500000.0
SWA_ROT_DIM = HEAD_DIM // 4
N_EXPERTS = 64
TOP_K = 8
N_GROUPS = 8
GROUP_SIZE = N_EXPERTS // N_GROUPS
TOPK_GROUPS = 4
EXPERT_DIM = 256
SHARED_DIM = 256
ROUTED_SCALE = 2.5
PLE_DIM = 256
LN_EPS = 1e-5
GN_EPS = 1e-6
NEG_INF = -1e30
DEPTH = 1
DEEPNORM_ALPHA = (2.0 * DEPTH) ** 0.25

OFF_RQ = 0
OFF_RK = RET_WIDTH
OFF_RV = 2 * RET_WIDTH
OFF_RG = 3 * RET_WIDTH
OFF_AQ = 4 * RET_WIDTH
OFF_AK = OFF_AQ + SWA_WIDTH
OFF_AV = OFF_AK + KV_WIDTH

V7X_LANES = 128
V7X_VMEM_LIMIT_BYTES = 56 * 1024 * 1024

MIX_CHUNKS = 8
MIX_TILE = MIX_CHUNKS * CHUNK
ROUTE_SUB = 256
ROUTE_TILE = 512
SLOT_ROWS = 48
SLOT_GROUP = 16
COMBINE_SUBS = 2
EXPERT_TILE = 512
EXPERT_BLOCK = 3072

NT_DIMS = (((1,), (1,)), ((), ()))
TN_DIMS = (((0,), (0,)), ((), ()))


def _layer_norm(y, g, b):
    mu = jnp.mean(y, axis=-1, keepdims=True)
    d = y - mu
    var = jnp.mean(d * d, axis=-1, keepdims=True)
    return d * lax.rsqrt(var + LN_EPS) * g + b


def _silu(v):
    return v * jax.nn.sigmoid(v)


def _rope_inplace(buf, col0, n_groups, cos, sin, near, shift, scale):
    for g in range(n_groups):
        cols = slice(col0 + g * V7X_LANES, col0 + (g + 1) * V7X_LANES)
        xg = buf[:, cols]
        partner = jnp.where(near, pltpu.roll(xg, V7X_LANES - shift, 1), pltpu.roll(xg, shift, 1))
        y = xg * cos + partner * sin
        if scale != 1.0:
            y = y * scale
        buf[:, cols] = y


def _head_mean(y, head_avg):
    hi = y.astype(BF16)
    lo = (y - hi.astype(F32)).astype(BF16)
    return jnp.dot(jnp.concatenate([hi, lo], axis=1), head_avg, preferred_element_type=F32)


def _mixer_kernel(x_ref, win_ref, wout_ref, cr_ref, sr_ref, ca_ref, sa_ref,
                  dec_ref, kw_ref, qw_ref, cd_ref, gn_ref, asc_ref, sink_ref,
                  g1_ref, b1_ref, o_ref, proj_s, mix_s, state_s, kprev_s, vprev_s):
    step = pl.program_id(1)

    @pl.when(step == 0)
    def _():
        state_s[...] = jnp.zeros_like(state_s)
        kprev_s[...] = jnp.zeros_like(kprev_s)
        vprev_s[...] = jnp.zeros_like(vprev_s)

    proj_s[...] = jnp.dot(x_ref[...].astype(BF16), win_ref[...], preferred_element_type=F32)

    lane = lax.broadcasted_iota(jnp.int32, (MIX_TILE, V7X_LANES), 1) % HEAD_DIM
    scale = HEAD_DIM ** -0.5
    cr, sr = cr_ref[...], sr_ref[...]
    _rope_inplace(proj_s, OFF_RQ, RET_WIDTH // V7X_LANES, cr, sr, lane < HEAD_DIM // 2, HEAD_DIM // 2, 1.0)
    _rope_inplace(proj_s, OFF_RK, RET_WIDTH // V7X_LANES, cr, sr, lane < HEAD_DIM // 2, HEAD_DIM // 2, scale)
    ca, sa = ca_ref[...], sa_ref[...]
    _rope_inplace(proj_s, OFF_AQ, SWA_WIDTH // V7X_LANES, ca, sa, lane < SWA_ROT_DIM // 2, SWA_ROT_DIM // 2, scale)
    _rope_inplace(proj_s, OFF_AK, KV_WIDTH // V7X_LANES, ca, sa, lane < SWA_ROT_DIM // 2, SWA_ROT_DIM // 2, 1.0)

    qi = lax.broadcasted_iota(jnp.int32, (CHUNK, 2 * CHUNK), 0)
    kj = lax.broadcasted_iota(jnp.int32, (CHUNK, 2 * CHUNK), 1)
    rel = CHUNK + qi - kj
    in_window = (rel >= 0) & (rel < WINDOW)

    sub = lax.broadcasted_iota(jnp.int32, (CHUNK, V7X_LANES), 0)
    lan = lax.broadcasted_iota(jnp.int32, (CHUNK, V7X_LANES), 1)
    low = lan < HEAD_DIM
    same_head = (sub // HEAD_DIM) == (lan // HEAD_DIM)
    head_avg = jnp.where(same_head, 1.0 / HEAD_DIM, 0.0).astype(BF16)
    head_avg = jnp.concatenate([head_avg, head_avg], axis=0)

    def chunk_stages(c):
        rows = slice(c * CHUNK, (c + 1) * CHUNK)

        valid = in_window & (kj >= jnp.where(step == 0, CHUNK, 0)) if c == 0 else in_window
        lanes = lambda off, p: slice(off + p * V7X_LANES, off + (p + 1) * V7X_LANES)

        q2s, v2s, s_ret, kvs = [], [], [], []
        for p in range(RET_HEADS // 2):
            q2 = proj_s[rows, lanes(OFF_RQ, p)]
            k2 = proj_s[rows, lanes(OFF_RK, p)]
            v2 = proj_s[rows, lanes(OFF_RV, p)]
            q_ab = jnp.concatenate([jnp.where(low, q2, 0.0), jnp.where(low, 0.0, q2)], axis=0).astype(BF16)
            s_ret.append(lax.dot_general(q_ab, k2.astype(BF16), NT_DIMS, preferred_element_type=F32))
            kvs.append(lax.dot_general((k2 * kw_ref[p]).astype(BF16), v2.astype(BF16), TN_DIMS,
                                       preferred_element_type=F32))
            q2s.append(q2)
            v2s.append(v2)

        kc = proj_s[rows, OFF_AK:OFF_AK + KV_WIDTH]
        vc = proj_s[rows, OFF_AV:OFF_AV + KV_WIDTH]
        kband = jnp.concatenate([kprev_s[...], kc], axis=0)
        vband = jnp.concatenate([vprev_s[...], vc], axis=0)
        kprev_s[...] = kc
        vprev_s[...] = vc
        kband_r = pltpu.roll(kband, HEAD_DIM, 1)
        vband_r = pltpu.roll(vband, HEAD_DIM, 1)
        low2 = jnp.concatenate([low, low], axis=0)
        s_att, v_cats = [], []
        for g in range(SWA_KV_HEADS):
            k_even, k_odd = (kband, kband_r) if g == 0 else (kband_r, kband)
            v_low, v_high = (vband, vband_r) if g == 0 else (vband_r, vband)
            v_cats.append(jnp.concatenate([jnp.where(low2, v_low, 0.0), jnp.where(low2, 0.0, v_high)],
                                          axis=0).astype(BF16))
            q_off = OFF_AQ + g * SWA_GROUP * HEAD_DIM
            qt0 = proj_s[rows, q_off:q_off + V7X_LANES]
            qt1 = proj_s[rows, q_off + V7X_LANES:q_off + 2 * V7X_LANES]
            q_even = jnp.concatenate([jnp.where(low, qt0, 0.0), jnp.where(low, qt1, 0.0)], axis=0)
            q_odd = jnp.concatenate([jnp.where(low, 0.0, qt0), jnp.where(low, 0.0, qt1)], axis=0)
            s_even = lax.dot_general(q_even.astype(BF16), k_even.astype(BF16), NT_DIMS,
                                     preferred_element_type=F32)
            s_odd = lax.dot_general(q_odd.astype(BF16), k_odd.astype(BF16), NT_DIMS,
                                    preferred_element_type=F32)
            s_att.append((s_even[:CHUNK], s_odd[:CHUNK]))
            s_att.append((s_even[CHUNK:], s_odd[CHUNK:]))
        yield

        ys = []
        for p in range(RET_HEADS // 2):
            s_a = (s_ret[p][:CHUNK] * dec_ref[2 * p]).astype(BF16)
            s_b = (s_ret[p][CHUNK:] * dec_ref[2 * p + 1]).astype(BF16)
            st = state_s[p]
            lhs = jnp.concatenate([s_a, s_b, (q2s[p] * qw_ref[p]).astype(BF16)], axis=1)
            rhs = jnp.concatenate([jnp.where(low, v2s[p], 0.0), jnp.where(low, 0.0, v2s[p]), st],
                                  axis=0).astype(BF16)
            ys.append(jnp.dot(lhs, rhs, preferred_element_type=F32))
            state_s[p] = st * cd_ref[p] + jnp.where(same_head, kvs[p], 0.0)

        def soft(s, h):
            s = jnp.where(valid, s, NEG_INF)
            sink = sink_ref[h]
            m = jnp.maximum(jnp.max(s, axis=-1, keepdims=True), sink)
            e = jnp.exp(s - m)
            den = jnp.sum(e, axis=-1, keepdims=True) + jnp.exp(sink - m)
            return e.astype(BF16), den

        es, dens = [], []
        for t, (sa, sb) in enumerate(s_att):
            e_a, den_a = soft(sa, 2 * t)
            e_b, den_b = soft(sb, 2 * t + 1)
            es.append(jnp.concatenate([e_a, e_b], axis=1))
            dens.append(jnp.where(low, den_a, den_b))
        yield

        mus = [_head_mean(y, head_avg) for y in ys]
        for t in range(SWA_Q_HEADS // 2):
            o = jnp.dot(es[t], v_cats[t // (SWA_GROUP // 2)], preferred_element_type=F32) / dens[t]
            mix_s[rows, lanes(RET_WIDTH, t)] = o * asc_ref[:, lanes(0, t)]
        yield

        ds = [y - mu for y, mu in zip(ys, mus)]
        variances = [_head_mean(d * d, head_avg) for d in ds]
        for p in range(RET_HEADS // 2):
            yn = ds[p] * lax.rsqrt(variances[p] + GN_EPS)
            gate = proj_s[rows, lanes(OFF_RG, p)]
            mix_s[rows, lanes(0, p)] = _silu(gate) * (yn * gn_ref[:, lanes(0, p)])

    chunks = [chunk_stages(c) for c in range(MIX_CHUNKS)]
    for c in range(MIX_CHUNKS):
        next(chunks[c])
        if c > 0:
            next(chunks[c - 1], None)
        next(chunks[c])
        next(chunks[c])
    next(chunks[-1], None)

    y = jnp.dot(mix_s[...].astype(BF16), wout_ref[...], preferred_element_type=F32)
    y = y + DEEPNORM_ALPHA * x_ref[...]
    o_ref[...] = _layer_norm(y, g1_ref[...], b1_ref[...])


def _rope_tables(seq, rot_dim, theta):
    half = rot_dim // 2
    inv_freq = theta ** (-jnp.arange(half, dtype=F32) / half)
    ang = jnp.arange(seq, dtype=F32)[:, None] * inv_freq[None, :]
    cos, sin = jnp.cos(ang), jnp.sin(ang)
    rest = HEAD_DIM - rot_dim
    cos_h = jnp.concatenate([cos, cos, jnp.ones((seq, rest), F32)], axis=1)
    sin_h = jnp.concatenate([-sin, sin, jnp.zeros((seq, rest), F32)], axis=1)
    reps = V7X_LANES // HEAD_DIM
    return jnp.tile(cos_h, (1, reps)), jnp.tile(sin_h, (1, reps))


def _retention_tables():
    h = RET_HEADS
    log_g = jnp.log1p(-(2.0 ** (-5.0 - jnp.arange(h, dtype=F32))))
    idx = jnp.arange(CHUNK, dtype=F32)
    diff = idx[:, None] - idx[None, :]
    intra = jnp.where(diff[None] >= 0, jnp.exp(jnp.maximum(diff, 0.0)[None] * log_g[:, None, None]), 0.0)
    k_w = jnp.exp((CHUNK - 1 - idx)[:, None] * log_g[None, :])
    q_w = jnp.exp((idx + 1)[:, None] * log_g[None, :])
    chunk_decay = jnp.exp(CHUNK * log_g)
    pairs = lambda t: jnp.repeat(t.T.reshape(h // 2, 2, -1), HEAD_DIM, axis=1).transpose(0, 2, 1)
    return intra, pairs(k_w), pairs(q_w), pairs(chunk_decay[None, :])


def _mixer(x, w_in, w_out, gn_gain, attn_scale, sinks, ln_g, ln_b):
    b, s, d = x.shape
    cr, sr = _rope_tables(s, HEAD_DIM, RET_THETA)
    ca, sa = _rope_tables(s, SWA_ROT_DIM, SWA_THETA)
    dec, kw, qw, cd = _retention_tables()
    const = lambda shape: pl.BlockSpec(shape, lambda bi, i: (0,) * len(shape))
    tab = pl.BlockSpec((MIX_TILE, V7X_LANES), lambda bi, i: (i, 0))
    smem = pl.BlockSpec(memory_space=pltpu.SMEM)
    return pl.pallas_call(
        _mixer_kernel,
        name="mixer",
        grid=(b, s // MIX_TILE),
        in_specs=[
            pl.BlockSpec((None, MIX_TILE, d), lambda bi, i: (bi, i, 0)),
            const((d, IN_WIDTH)), const((MIX_WIDTH, d)),
            tab, tab, tab, tab,
            const((RET_HEADS, CHUNK, CHUNK)), const((RET_HEADS // 2, CHUNK, V7X_LANES)),
            const((RET_HEADS // 2, CHUNK, V7X_LANES)), const((RET_HEADS // 2, 1, V7X_LANES)),
            const((1, RET_WIDTH)), const((1, SWA_WIDTH)), smem,
            const((1, d)), const((1, d)),
        ],
        out_specs=pl.BlockSpec((None, MIX_TILE, d), lambda bi, i: (bi, i, 0)),
        out_shape=jax.ShapeDtypeStruct((b, s, d), F32),
        scratch_shapes=[
            pltpu.VMEM((MIX_TILE, IN_WIDTH), F32),
            pltpu.VMEM((MIX_TILE, MIX_WIDTH), F32),
            pltpu.VMEM((RET_HEADS // 2, V7X_LANES, V7X_LANES), F32),
            pltpu.VMEM((CHUNK, KV_WIDTH), F32),
            pltpu.VMEM((CHUNK, KV_WIDTH), F32),
        ],
        compiler_params=pltpu.CompilerParams(
            dimension_semantics=("parallel", "arbitrary"),
            vmem_limit_bytes=V7X_VMEM_LIMIT_BYTES),
    )(x, w_in.astype(BF16), w_out.astype(BF16), cr, sr, ca, sa, dec, kw, qw, cd,
      gn_gain.reshape(1, -1), attn_scale.reshape(1, -1), sinks, ln_g.reshape(1, -1), ln_b.reshape(1, -1))


def _route(x, wr_ref, bias_ref, tri_ref, between):
    tt = x.shape[0]
    xh = x.astype(BF16)
    xl = (x - xh.astype(F32)).astype(BF16)
    w = wr_ref[...]
    wh = w.astype(BF16)
    wl = (w - wh.astype(F32)).astype(BF16)
    dot = functools.partial(jnp.dot, preferred_element_type=F32)
    hi = dot(xh, jnp.concatenate([wh, wl], axis=1))
    logits = (hi[:, :N_EXPERTS] + (dot(xl, wh) + hi[:, N_EXPERTS:])).T
    scores = jax.nn.sigmoid(logits)
    biased = scores + bias_ref[...]

    shape3 = (N_GROUPS, GROUP_SIZE, tt)
    grp = lax.broadcasted_iota(jnp.int32, shape3, 0)
    mem = lax.broadcasted_iota(jnp.int32, shape3, 1)
    b3 = biased.reshape(shape3)

    m1 = jnp.max(b3, axis=1, keepdims=True)
    i1 = jnp.min(jnp.where(b3 == m1, mem, GROUP_SIZE), axis=1, keepdims=True)
    m2 = jnp.max(jnp.where(mem == i1, -jnp.inf, b3), axis=1, keepdims=True)
    gs = jnp.broadcast_to(m1 + m2, shape3)

    gmask = jnp.zeros(shape3, F32)
    for _ in range(TOPK_GROUPS):
        m = jnp.max(gs, axis=0, keepdims=True)
        pick = grp == jnp.min(jnp.where(gs == m, grp, N_GROUPS), axis=0, keepdims=True)
        gmask = jnp.where(pick, 1.0, gmask)
        gs = jnp.where(pick, -jnp.inf, gs)
        between()

    eid = grp * GROUP_SIZE + mem
    cand = jnp.where(gmask > 0.0, b3, NEG_INF)
    sel = jnp.zeros(shape3, F32)
    for _ in range(TOP_K):
        m = jnp.max(jnp.max(cand, axis=0, keepdims=True), axis=1, keepdims=True)
        first = jnp.where(cand == m, eid, N_EXPERTS)
        first = jnp.min(jnp.min(first, axis=0, keepdims=True), axis=1, keepdims=True)
        pick = eid == first
        sel = jnp.where(pick, 1.0, sel)
        cand = jnp.where(pick, -jnp.inf, cand)
        between()

    sel = sel.reshape(N_EXPERTS, tt)
    wsel = scores * sel
    comb = wsel / jnp.sum(wsel, axis=0, keepdims=True) * ROUTED_SCALE

    lane = lax.broadcasted_iota(jnp.int32, (N_EXPERTS, V7X_LANES), 1)
    cnt = jnp.zeros((N_EXPERTS, V7X_LANES), F32)
    ranks = []
    for sb in range(tt // ROUTE_SUB):
        cols = slice(sb * ROUTE_SUB, (sb + 1) * ROUTE_SUB)
        sel_sb = sel[:, cols]
        rank = jnp.dot(sel_sb.astype(BF16), tri_ref[...], preferred_element_type=F32)
        ranks.append(jnp.where(sel_sb > 0.0, rank, -1.0))
        cnt = jnp.where(lane == sb, jnp.sum(sel_sb, axis=1, keepdims=True), cnt)
    return comb, jnp.concatenate(ranks, axis=1), cnt


def _route_dispatch_kernel(x_ref, wr_ref, bias_ref, tri_ref, wt_ref, rk_ref, cnt_ref, xs_ref, xb_s, rk_s):
    step = pl.program_id(0)

    @pl.when(step == 0)
    def _():
        xb_s[...] = jnp.zeros_like(xb_s)
        rk_s[...] = jnp.full(rk_s.shape, -1.0, F32)

    prev = (step + 1) % 2
    one = jnp.ones((), BF16)

    def place(sb, g):
        cols = slice(sb * ROUTE_SUB, (sb + 1) * ROUTE_SUB)
        onehot = _slot_onehot(lambda e: rk_s[prev, e:e + 1, cols], g, lambda e: one)
        rows = jnp.dot(onehot, xb_s[prev, cols, :], preferred_element_type=F32).astype(BF16)
        xs_ref[g * SLOT_GROUP:(g + 1) * SLOT_GROUP, sb * SLOT_ROWS:(sb + 1) * SLOT_ROWS, :] = (
            rows.reshape(SLOT_GROUP, SLOT_ROWS, rows.shape[1]))

    units = iter([functools.partial(place, sb, g) for sb in range(ROUTE_TILE // ROUTE_SUB)
                  for g in range(N_EXPERTS // SLOT_GROUP)])
    x = x_ref[...]
    comb, ranks, cnt = _route(x, wr_ref, bias_ref, tri_ref, lambda: next(units, lambda: None)())
    for unit in units:
        unit()

    wt_ref[...] = comb
    rk_ref[...] = ranks
    cnt_ref[...] = cnt
    xb_s[step % 2] = x.astype(BF16)
    rk_s[step % 2] = ranks


def _route_dispatch(x1, w_router, router_bias):
    t, d = x1.shape
    n_tiles = t // ROUTE_TILE
    tile_rows = ROUTE_TILE // ROUTE_SUB * SLOT_ROWS
    tri = (jnp.arange(ROUTE_SUB)[:, None] < jnp.arange(ROUTE_SUB)[None, :]).astype(BF16)
    routed = lambda s: jnp.minimum(s, n_tiles - 1)
    placed = lambda s: jnp.maximum(s - 1, 0)
    return pl.pallas_call(
        _route_dispatch_kernel,
        name="route_dispatch",
        grid=(n_tiles + 1,),
        in_specs=[
            pl.BlockSpec((ROUTE_TILE, d), lambda s: (routed(s), 0)),
            pl.BlockSpec((d, N_EXPERTS), lambda s: (0, 0)),
            pl.BlockSpec((N_EXPERTS, 1), lambda s: (0, 0)),
            pl.BlockSpec((ROUTE_SUB, ROUTE_SUB), lambda s: (0, 0)),
        ],
        out_specs=[
            pl.BlockSpec((N_EXPERTS, ROUTE_TILE), lambda s: (0, routed(s))),
            pl.BlockSpec((N_EXPERTS, ROUTE_TILE), lambda s: (0, routed(s))),
            pl.BlockSpec((None, N_EXPERTS, V7X_LANES), lambda s: (routed(s), 0, 0)),
            pl.BlockSpec((N_EXPERTS, tile_rows, d), lambda s: (0, placed(s), 0)),
        ],
        out_shape=[
            jax.ShapeDtypeStruct((N_EXPERTS, t), F32),
            jax.ShapeDtypeStruct((N_EXPERTS, t), F32),
            jax.ShapeDtypeStruct((n_tiles, N_EXPERTS, V7X_LANES), F32),
            jax.ShapeDtypeStruct((N_EXPERTS, n_tiles * tile_rows, d), BF16),
        ],
        scratch_shapes=[
            pltpu.VMEM((2, ROUTE_TILE, d), BF16),
            pltpu.VMEM((2, N_EXPERTS, ROUTE_TILE), F32),
        ],
        compiler_params=pltpu.CompilerParams(
            dimension_semantics=("arbitrary",),
            vmem_limit_bytes=V7X_VMEM_LIMIT_BYTES),
    )(x1, w_router, router_bias.reshape(-1, 1), tri)


def _slot_onehot(rank_row, group, values):
    rid = lax.broadcasted_iota(jnp.int32, (SLOT_ROWS, ROUTE_SUB), 0).astype(F32).astype(BF16)
    pieces = []
    for i in range(SLOT_GROUP):
        e = group * SLOT_GROUP + i
        pieces.append(jnp.where(rid == rank_row(e).astype(BF16), values(e), jnp.zeros((), BF16)))
    return jnp.concatenate(pieces, axis=0)


def _expert_ffn(x, w1, w3, w2):
    h = _silu(jnp.dot(x, w1, preferred_element_type=F32))
    h = h * jnp.dot(x, w3, preferred_element_type=F32)
    return jnp.dot(h.astype(BF16), w2, preferred_element_type=F32).astype(BF16)


def _expert_kernel(xs_ref, w1_ref, w3_ref, w2_ref, ys_ref, w1b_ref, w3b_ref, w2b_ref, w1_s, w3_s, w2_s):
    @pl.when(pl.program_id(1) == 0)
    def _():
        for src, dst, out in ((w1_ref, w1_s, w1b_ref), (w3_ref, w3_s, w3b_ref), (w2_ref, w2_s, w2b_ref)):
            wb = src[...].astype(BF16)
            dst[...] = wb
            out[...] = wb

    block = xs_ref.shape[0]
    tile = EXPERT_TILE if block % EXPERT_TILE == 0 else block
    for q in range(block // tile):
        rows = slice(q * tile, (q + 1) * tile)
        ys_ref[rows, :] = _expert_ffn(xs_ref[rows, :], w1_s[...], w3_s[...], w2_s[...])


def _experts(xs, w1, w3, w2):
    n_e, seg, d = xs.shape
    block = EXPERT_BLOCK if seg % EXPERT_BLOCK == 0 else seg
    row_spec = pl.BlockSpec((None, block, d), lambda e, t: (e, t, 0))
    return pl.pallas_call(
        _expert_kernel,
        name="experts",
        grid=(n_e, seg // block),
        in_specs=[
            row_spec,
            pl.BlockSpec((None, d, EXPERT_DIM), lambda e, t: (e, 0, 0)),
            pl.BlockSpec((None, d, EXPERT_DIM), lambda e, t: (e, 0, 0)),
            pl.BlockSpec((None, EXPERT_DIM, d), lambda e, t: (e, 0, 0)),
        ],
        out_specs=[
            row_spec,
            pl.BlockSpec((None, d, EXPERT_DIM), lambda e, t: (e, 0, 0)),
            pl.BlockSpec((None, d, EXPERT_DIM), lambda e, t: (e, 0, 0)),
            pl.BlockSpec((None, EXPERT_DIM, d), lambda e, t: (e, 0, 0)),
        ],
        out_shape=[
            jax.ShapeDtypeStruct(xs.shape, BF16),
            jax.ShapeDtypeStruct(w1.shape, BF16),
            jax.ShapeDtypeStruct(w3.shape, BF16),
            jax.ShapeDtypeStruct(w2.shape, BF16),
        ],
        scratch_shapes=[
            pltpu.VMEM((d, EXPERT_DIM), BF16),
            pltpu.VMEM((d, EXPERT_DIM), BF16),
            pltpu.VMEM((EXPERT_DIM, d), BF16),
        ],
        compiler_params=pltpu.CompilerParams(
            dimension_semantics=("parallel", "arbitrary"),
            vmem_limit_bytes=V7X_VMEM_LIMIT_BYTES),
    )(xs, w1, w3, w2)


def _combine_kernel(ovn_ref, ovl_ref, x_ref, p_ref, rk_ref, wt_ref, ys_ref,
                    ws1_ref, ws3_ref, ws2_ref, wg_ref, bg_ref, wp_ref, g2_ref, b2_ref,
                    w1_hbm, w3_hbm, w2_hbm, o_ref, acc_s, w1_s, w3_s, w2_s, sems):
    j = pl.program_id(0)
    n_over = ovn_ref[j]
    pairs = COMBINE_SUBS * N_EXPERTS

    def fetch(k, half):
        e = ovl_ref[j * pairs + k] % N_EXPERTS
        return [pltpu.make_async_copy(src.at[e], dst.at[half], sems.at[half, i])
                for i, (src, dst) in enumerate(((w1_hbm, w1_s), (w3_hbm, w3_s), (w2_hbm, w2_s)))]

    @pl.when(n_over > 0)
    def _():
        for cp in fetch(0, 0):
            cp.start()

    x = x_ref[...]
    xb = x.astype(BF16)
    hs = _silu(jnp.dot(xb, ws1_ref[...], preferred_element_type=F32))
    hs = hs * jnp.dot(xb, ws3_ref[...], preferred_element_type=F32)
    shared = jnp.dot(hs.astype(BF16), ws2_ref[...], preferred_element_type=F32)
    gate = jax.nn.sigmoid(jnp.dot(xb, wg_ref[...], preferred_element_type=F32) + bg_ref[...])
    side = jnp.dot(p_ref[...].astype(BF16), wp_ref[...], preferred_element_type=F32)
    acc_s[...] = DEEPNORM_ALPHA * x + shared + gate * side

    for sb in range(COMBINE_SUBS):
        cols = slice(sb * ROUTE_SUB, (sb + 1) * ROUTE_SUB)
        for g in range(N_EXPERTS // SLOT_GROUP):
            scat = _slot_onehot(lambda e: rk_ref[e:e + 1, cols], g,
                                lambda e: wt_ref[e:e + 1, cols].astype(BF16))
            rows = ys_ref[g * SLOT_GROUP:(g + 1) * SLOT_GROUP, sb * SLOT_ROWS:(sb + 1) * SLOT_ROWS, :]
            rows = rows.reshape(SLOT_GROUP * SLOT_ROWS, ys_ref.shape[2])
            acc_s[cols, :] += lax.dot_general(scat, rows, TN_DIMS, preferred_element_type=F32)

    sub = lax.broadcasted_iota(jnp.int32, (ROUTE_SUB, ROUTE_SUB), 0)
    lan = lax.broadcasted_iota(jnp.int32, (ROUTE_SUB, ROUTE_SUB), 1)

    def overflow(k, carry):
        half = k % 2

        @pl.when(k + 1 < n_over)
        def _():
            for cp in fetch(k + 1, 1 - half):
                cp.start()

        pair = ovl_ref[j * pairs + k]
        sb, e = pair // N_EXPERTS, pair % N_EXPERTS
        rank = rk_ref[pl.ds(e, 1), :]
        late = jnp.where(rank >= SLOT_ROWS, wt_ref[pl.ds(e, 1), :], 0.0)
        late = sum(jnp.where(sb == i, late[:, i * ROUTE_SUB:(i + 1) * ROUTE_SUB], 0.0)
                   for i in range(COMBINE_SUBS))
        pick = jnp.where(sub == lan, jnp.broadcast_to(late, (ROUTE_SUB, ROUTE_SUB)), 0.0).astype(BF16)
        rows = pl.ds(pl.multiple_of(sb * ROUTE_SUB, ROUTE_SUB), ROUTE_SUB)
        for cp in fetch(k, half):
            cp.wait()
        y = _expert_ffn(x_ref[rows, :].astype(BF16), w1_s[half], w3_s[half], w2_s[half])
        acc_s[rows, :] += jnp.dot(pick, y, preferred_element_type=F32)
        return carry

    lax.fori_loop(0, n_over, overflow, 0)
    o_ref[...] = _layer_norm(acc_s[...], g2_ref[...], b2_ref[...])


def _combine(x1, p, rk, wt, ys, ov_n, ov_list, w1, w3, w2, ws1, ws3, ws2, wg, bg, wp, ln_g, ln_b):
    t, d = x1.shape
    tile = COMBINE_SUBS * ROUTE_SUB
    const = lambda shape: pl.BlockSpec(shape, lambda j, *_: (0,) * len(shape))
    hbm = pl.BlockSpec(memory_space=pl.ANY)
    return pl.pallas_call(
        _combine_kernel,
        name="combine",
        grid_spec=pltpu.PrefetchScalarGridSpec(
            num_scalar_prefetch=2,
            grid=(t // tile,),
            in_specs=[
                pl.BlockSpec((tile, d), lambda j, *_: (j, 0)),
                pl.BlockSpec((tile, PLE_DIM), lambda j, *_: (j, 0)),
                pl.BlockSpec((N_EXPERTS, tile), lambda j, *_: (0, j)),
                pl.BlockSpec((N_EXPERTS, tile), lambda j, *_: (0, j)),
                pl.BlockSpec((N_EXPERTS, COMBINE_SUBS * SLOT_ROWS, d), lambda j, *_: (0, j, 0)),
                const((d, SHARED_DIM)), const((d, SHARED_DIM)), const((SHARED_DIM, d)),
                const((d, d)), const((1, d)), const((PLE_DIM, d)), const((1, d)), const((1, d)),
                hbm, hbm, hbm,
            ],
            out_specs=pl.BlockSpec((tile, d), lambda j, *_: (j, 0)),
            scratch_shapes=[
                pltpu.VMEM((tile, d), F32),
                pltpu.VMEM((2, d, EXPERT_DIM), BF16),
                pltpu.VMEM((2, d, EXPERT_DIM), BF16),
                pltpu.VMEM((2, EXPERT_DIM, d), BF16),
                pltpu.SemaphoreType.DMA((2, 3)),
            ],
        ),
        out_shape=jax.ShapeDtypeStruct((t, d), F32),
        compiler_params=pltpu.CompilerParams(
            dimension_semantics=("arbitrary",),
            vmem_limit_bytes=V7X_VMEM_LIMIT_BYTES),
    )(ov_n, ov_list, x1, p, rk, wt, ys,
      ws1.astype(BF16), ws3.astype(BF16), ws2.astype(BF16),
      wg.astype(BF16), bg.reshape(1, -1), wp.astype(BF16), ln_g.reshape(1, -1), ln_b.reshape(1, -1),
      w1, w3, w2)


def _overflow_lists(cnt):
    pairs = COMBINE_SUBS * N_EXPERTS
    over = (cnt > SLOT_ROWS).reshape(-1, pairs)
    place = jnp.cumsum(over, axis=1) - 1
    slot = jnp.arange(pairs)
    hit = over[:, :, None] & (place[:, :, None] == slot[None, None, :])
    ov_list = jnp.sum(jnp.where(hit, slot[None, :, None], 0), axis=1)
    return jnp.sum(over, axis=1).astype(jnp.int32), ov_list.reshape(-1).astype(jnp.int32)


def _moe_block(x1, p, w_router, router_bias, w1, w3, w2, ws1, ws3, ws2, wg, bg, wp, ln_g, ln_b):
    t, d = x1.shape
    n_sub = t // ROUTE_SUB
    wt, rk, cnt, xs = _route_dispatch(x1, w_router, router_bias)
    per_step = ROUTE_TILE // ROUTE_SUB
    cnt = jnp.swapaxes(cnt[:, :, :per_step], 1, 2).reshape(n_sub, N_EXPERTS).astype(jnp.int32)
    ov_n, ov_list = _overflow_lists(cnt)
    ys, w1b, w3b, w2b = _experts(xs, w1, w3, w2)
    return _combine(x1, p, rk, wt, ys, ov_n, ov_list, w1b, w3b, w2b, ws1, ws3, ws2, wg, bg, wp, ln_g, ln_b)


def kernel(x, p, w_in, ret_gn_gain, attn_scale, sinks, w_out, ln1_g, ln1_b, w_router, router_bias,
           w1, w3, w2, ws1, ws3, ws2, w_ple_gate, b_ple_gate, w_ple_proj, ln2_g, ln2_b):
    b, s, d = x.shape
    h = x
    for i in range(DEPTH):
        h = _mixer(h, w_in[i], w_out[i], ret_gn_gain[i], attn_scale[i], sinks[i], ln1_g[i], ln1_b[i])
        h2 = _moe_block(h.reshape(b * s, d), p[i].reshape(b * s, -1), w_router[i], router_bias[i],
                        w1[i], w3[i], w2[i], ws1[i], ws3[i], ws2[i],
                        w_ple_gate[i], b_ple_gate[i], w_ple_proj[i], ln2_g[i], ln2_b[i])
        h = h2.reshape(b, s, d)
    return h
```

```python
import functools

import jax
import jax.numpy as jnp
from jax import lax
from jax.experimental import pallas as pl
from jax.experimental.pallas import tpu as pltpu

F32 = jnp.float32
BF16 = jnp.bfloat16

D_MODEL = 1024
HEAD_DIM = 64
RET_HEADS = 8
SWA_Q_HEADS = 8
SWA_KV_HEADS = 2
SWA_GROUP = SWA_Q_HEADS // SWA_KV_HEADS
RET_WIDTH = RET_HEADS * HEAD_DIM
SWA_WIDTH = SWA_Q_HEADS * HEAD_DIM
KV_WIDTH = SWA_KV_HEADS * HEAD_DIM
MIX_WIDTH = RET_WIDTH + SWA_WIDTH
IN_WIDTH = 4 * RET_WIDTH + SWA_WIDTH + 2 * KV_WIDTH
CHUNK = 128
WINDOW = 128
RET_THETA = 10000.0
SWA_THETA = 500000.0
SWA_ROT_DIM = HEAD_DIM // 4
N_EXPERTS = 64
TOP_K = 8
N_GROUPS = 8
GROUP_SIZE = N_EXPERTS // N_GROUPS
TOPK_GROUPS = 4
EXPERT_DIM = 256
SHARED_DIM = 256
ROUTED_SCALE = 2.5
PLE_DIM = 256
LN_EPS = 1e-5
GN_EPS = 1e-6
NEG_INF = -1e30
DEPTH = 1
DEEPNORM_ALPHA = (2.0 * DEPTH) ** 0.25

OFF_RQ = 0
OFF_RK = RET_WIDTH
OFF_RV = 2 * RET_WIDTH
OFF_RG = 3 * RET_WIDTH
OFF_AQ = 4 * RET_WIDTH
OFF_AK = OFF_AQ + SWA_WIDTH
OFF_AV = OFF_AK + KV_WIDTH

V7X_LANES = 128
V7X_VMEM_LIMIT_BYTES = 56 * 1024 * 1024

MIX_CHUNKS = 8
MIX_TILE = MIX_CHUNKS * CHUNK
PROJ_COLS = 256
ROUTE_SUB = 256
ROUTE_TILE = 512
SLOT_ROWS = 48
SLOT_GROUP = 16
COMBINE_SUBS = 2
EXPERT_TILE = 512
EXPERT_BLOCK = 3072

NT_DIMS = (((1,), (1,)), ((), ()))
TN_DIMS = (((0,), (0,)), ((), ()))


def _layer_norm(y, g, b):
    mu = jnp.mean(y, axis=-1, keepdims=True)
    d = y - mu
    var = jnp.mean(d * d, axis=-1, keepdims=True)
    return d * lax.rsqrt(var + LN_EPS) * g + b


def _silu(v):
    return v * jax.nn.sigmoid(v)


def _rope_inplace(buf, rows, col0, n_groups, cos, sin, near, shift, scale):
    for g in range(n_groups):
        cols = slice(col0 + g * V7X_LANES, col0 + (g + 1) * V7X_LANES)
        xg = buf[rows, cols]
        partner = jnp.where(near, pltpu.roll(xg, V7X_LANES - shift, 1), pltpu.roll(xg, shift, 1))
        y = xg * cos + partner * sin
        if scale != 1.0:
            y = y * scale
        buf[rows, cols] = y


def _head_mean(y, head_avg):
    hi = y.astype(BF16)
    lo = (y - hi.astype(F32)).astype(BF16)
    return jnp.dot(jnp.concatenate([hi, lo], axis=1), head_avg, preferred_element_type=F32)


def _mixer_kernel(x_ref, win_ref, wout_ref, cr_ref, sr_ref, ca_ref, sa_ref,
                  dec_ref, kw_ref, qw_ref, cd_ref, gn_ref, asc_ref, sink_ref,
                  g1_ref, b1_ref, o_ref, proj_s, mix_s, state_s, kprev_s, vprev_s):
    step = pl.program_id(1)

    @pl.when(step == 0)
    def _():
        state_s[...] = jnp.zeros_like(state_s)
        kprev_s[...] = jnp.zeros_like(kprev_s)
        vprev_s[...] = jnp.zeros_like(vprev_s)

    half = MIX_TILE // 2
    top, bottom = slice(0, half), slice(half, MIX_TILE)
    lane = lax.broadcasted_iota(jnp.int32, (half, V7X_LANES), 1) % HEAD_DIM
    scale = HEAD_DIM ** -0.5

    def rope(rows):
        cr, sr = cr_ref[rows, :], sr_ref[rows, :]
        ret_near, ret_shift = lane < HEAD_DIM // 2, HEAD_DIM // 2
        _rope_inplace(proj_s, rows, OFF_RQ, RET_WIDTH // V7X_LANES, cr, sr, ret_near, ret_shift, 1.0)
        _rope_inplace(proj_s, rows, OFF_RK, RET_WIDTH // V7X_LANES, cr, sr, ret_near, ret_shift, scale)
        ca, sa = ca_ref[rows, :], sa_ref[rows, :]
        swa_near, swa_shift = lane < SWA_ROT_DIM // 2, SWA_ROT_DIM // 2
        _rope_inplace(proj_s, rows, OFF_AQ, SWA_WIDTH // V7X_LANES, ca, sa, swa_near, swa_shift, scale)
        _rope_inplace(proj_s, rows, OFF_AK, KV_WIDTH // V7X_LANES, ca, sa, swa_near, swa_shift, 1.0)

    xb = x_ref[...].astype(BF16)
    proj_s[top, :] = jnp.dot(xb[top], win_ref[...], preferred_element_type=F32)
    rope(top)

    def project_bottom(g):
        cols = slice(g * PROJ_COLS, (g + 1) * PROJ_COLS)
        proj_s[bottom, cols] = jnp.dot(xb[bottom], win_ref[:, cols], preferred_element_type=F32)

    pending = iter([functools.partial(project_bottom, g) for g in range(IN_WIDTH // PROJ_COLS)])
    fill = lambda: next(pending, lambda: None)()

    qi = lax.broadcasted_iota(jnp.int32, (CHUNK, 2 * CHUNK), 0)
    kj = lax.broadcasted_iota(jnp.int32, (CHUNK, 2 * CHUNK), 1)
    rel = CHUNK + qi - kj
    in_window = (rel >= 0) & (rel < WINDOW)

    sub = lax.broadcasted_iota(jnp.int32, (CHUNK, V7X_LANES), 0)
    lan = lax.broadcasted_iota(jnp.int32, (CHUNK, V7X_LANES), 1)
    low = lan < HEAD_DIM
    same_head = (sub // HEAD_DIM) == (lan // HEAD_DIM)
    head_avg = jnp.where(same_head, 1.0 / HEAD_DIM, 0.0).astype(BF16)
    head_avg = jnp.concatenate([head_avg, head_avg], axis=0)

    def chunk_stages(c):
        rows = slice(c * CHUNK, (c + 1) * CHUNK)

        valid = in_window & (kj >= jnp.where(step == 0, CHUNK, 0)) if c == 0 else in_window
        lanes = lambda off, p: slice(off + p * V7X_LANES, off + (p + 1) * V7X_LANES)

        q2s, v2s, s_ret, kvs = [], [], [], []
        for p in range(RET_HEADS // 2):
            q2 = proj_s[rows, lanes(OFF_RQ, p)]
            k2 = proj_s[rows, lanes(OFF_RK, p)]
            v2 = proj_s[rows, lanes(OFF_RV, p)]
            q_ab = jnp.concatenate([jnp.where(low, q2, 0.0), jnp.where(low, 0.0, q2)], axis=0).astype(BF16)
            s_ret.append(lax.dot_general(q_ab, k2.astype(BF16), NT_DIMS, preferred_element_type=F32))
            kvs.append(lax.dot_general((k2 * kw_ref[p]).astype(BF16), v2.astype(BF16), TN_DIMS,
                                       preferred_element_type=F32))
            q2s.append(q2)
            v2s.append(v2)

        kc = proj_s[rows, OFF_AK:OFF_AK + KV_WIDTH]
        vc = proj_s[rows, OFF_AV:OFF_AV + KV_WIDTH]
        kband = jnp.concatenate([kprev_s[...], kc], axis=0)
        vband = jnp.concatenate([vprev_s[...], vc], axis=0)
        kprev_s[...] = kc
        vprev_s[...] = vc
        kband_r = pltpu.roll(kband, HEAD_DIM, 1)
        vband_r = pltpu.roll(vband, HEAD_DIM, 1)
        low2 = jnp.concatenate([low, low], axis=0)
        s_att, v_cats = [], []
        for g in range(SWA_KV_HEADS):
            k_even, k_odd = (kband, kband_r) if g == 0 else (kband_r, kband)
            v_low, v_high = (vband, vband_r) if g == 0 else (vband_r, vband)
            v_cats.append(jnp.concatenate([jnp.where(low2, v_low, 0.0), jnp.where(low2, 0.0, v_high)],
                                          axis=0).astype(BF16))
            q_off = OFF_AQ + g * SWA_GROUP * HEAD_DIM
            qt0 = proj_s[rows, q_off:q_off + V7X_LANES]
            qt1 = proj_s[rows, q_off + V7X_LANES:q_off + 2 * V7X_LANES]
            q_even = jnp.concatenate([jnp.where(low, qt0, 0.0), jnp.where(low, qt1, 0.0)], axis=0)
            q_odd = jnp.concatenate([jnp.where(low, 0.0, qt0), jnp.where(low, 0.0, qt1)], axis=0)
            s_even = lax.dot_general(q_even.astype(BF16), k_even.astype(BF16), NT_DIMS,
                                     preferred_element_type=F32)
            s_odd = lax.dot_general(q_odd.astype(BF16), k_odd.astype(BF16), NT_DIMS,
                                    preferred_element_type=F32)
            s_att.append((s_even[:CHUNK], s_odd[:CHUNK]))
            s_att.append((s_even[CHUNK:], s_odd[CHUNK:]))
        yield

        ys = []
        for p in range(RET_HEADS // 2):
            s_a = (s_ret[p][:CHUNK] * dec_ref[2 * p]).astype(BF16)
            s_b = (s_ret[p][CHUNK:] * dec_ref[2 * p + 1]).astype(BF16)
            st = state_s[p]
            lhs = jnp.concatenate([s_a, s_b, (q2s[p] * qw_ref[p]).astype(BF16)], axis=1)
            rhs = jnp.concatenate([jnp.where(low, v2s[p], 0.0), jnp.where(low, 0.0, v2s[p]), st],
                                  axis=0).astype(BF16)
            ys.append(jnp.dot(lhs, rhs, preferred_element_type=F32))
            state_s[p] = st * cd_ref[p] + jnp.where(same_head, kvs[p], 0.0)

        def soft(s, h):
            s = jnp.where(valid, s, NEG_INF)
            sink = sink_ref[h]
            m = jnp.maximum(jnp.max(s, axis=-1, keepdims=True), sink)
            e = jnp.exp(s - m)
            den = jnp.sum(e, axis=-1, keepdims=True) + jnp.exp(sink - m)
            return e.astype(BF16), den

        es, dens = [], []
        for t, (sa, sb) in enumerate(s_att):
            e_a, den_a = soft(sa, 2 * t)
            e_b, den_b = soft(sb, 2 * t + 1)
            es.append(jnp.concatenate([e_a, e_b], axis=1))
            dens.append(jnp.where(low, den_a, den_b))
        yield

        mus = [_head_mean(y, head_avg) for y in ys]
        for t in range(SWA_Q_HEADS // 2):
            o = jnp.dot(es[t], v_cats[t // (SWA_GROUP // 2)], preferred_element_type=F32) / dens[t]
            mix_s[rows, lanes(RET_WIDTH, t)] = o * asc_ref[:, lanes(0, t)]
        yield

        ds = [y - mu for y, mu in zip(ys, mus)]
        variances = [_head_mean(d * d, head_avg) for d in ds]
        for p in range(RET_HEADS // 2):
            yn = ds[p] * lax.rsqrt(variances[p] + GN_EPS)
            gate = proj_s[rows, lanes(OFF_RG, p)]
            mix_s[rows, lanes(0, p)] = _silu(gate) * (yn * gn_ref[:, lanes(0, p)])

    chunks = [chunk_stages(c) for c in range(MIX_CHUNKS)]
    for c in range(MIX_CHUNKS):
        if c == MIX_CHUNKS // 2:
            for block in pending:
                block()
            rope(bottom)
        next(chunks[c])
        fill()
        if c > 0:
            next(chunks[c - 1], None)
            fill()
        next(chunks[c])
        fill()
        next(chunks[c])
        fill()
    next(chunks[-1], None)

    y = jnp.dot(mix_s[...].astype(BF16), wout_ref[...], preferred_element_type=F32)
    y = y + DEEPNORM_ALPHA * x_ref[...]
    o_ref[...] = _layer_norm(y, g1_ref[...], b1_ref[...])


def _rope_tables(seq, rot_dim, theta):
    half = rot_dim // 2
    inv_freq = theta ** (-jnp.arange(half, dtype=F32) / half)
    ang = jnp.arange(seq, dtype=F32)[:, None] * inv_freq[None, :]
    cos, sin = jnp.cos(ang), jnp.sin(ang)
    rest = HEAD_DIM - rot_dim
    cos_h = jnp.concatenate([cos, cos, jnp.ones((seq, rest), F32)], axis=1)
    sin_h = jnp.concatenate([-sin, sin, jnp.zeros((seq, rest), F32)], axis=1)
    reps = V7X_LANES // HEAD_DIM
    return jnp.tile(cos_h, (1, reps)), jnp.tile(sin_h, (1, reps))


def _retention_tables():
    h = RET_HEADS
    log_g = jnp.log1p(-(2.0 ** (-5.0 - jnp.arange(h, dtype=F32))))
    idx = jnp.arange(CHUNK, dtype=F32)
    diff = idx[:, None] - idx[None, :]
    intra = jnp.where(diff[None] >= 0, jnp.exp(jnp.maximum(diff, 0.0)[None] * log_g[:, None, None]), 0.0)
    k_w = jnp.exp((CHUNK - 1 - idx)[:, None] * log_g[None, :])
    q_w = jnp.exp((idx + 1)[:, None] * log_g[None, :])
    chunk_decay = jnp.exp(CHUNK * log_g)
    pairs = lambda t: jnp.repeat(t.T.reshape(h // 2, 2, -1), HEAD_DIM, axis=1).transpose(0, 2, 1)
    return intra, pairs(k_w), pairs(q_w), pairs(chunk_decay[None, :])


def _mixer(x, w_in, w_out, gn_gain, attn_scale, sinks, ln_g, ln_b):
    b, s, d = x.shape
    cr, sr = _rope_tables(s, HEAD_DIM, RET_THETA)
    ca, sa = _rope_tables(s, SWA_ROT_DIM, SWA_THETA)
    dec, kw, qw, cd = _retention_tables()
    const = lambda shape: pl.BlockSpec(shape, lambda bi, i: (0,) * len(shape))
    tab = pl.BlockSpec((MIX_TILE, V7X_LANES), lambda bi, i: (i, 0))
    smem = pl.BlockSpec(memory_space=pltpu.SMEM)
    return pl.pallas_call(
        _mixer_kernel,
        name="mixer",
        grid=(b, s // MIX_TILE),
        in_specs=[
            pl.BlockSpec((None, MIX_TILE, d), lambda bi, i: (bi, i, 0)),
            const((d, IN_WIDTH)), const((MIX_WIDTH, d)),
            tab, tab, tab, tab,
            const((RET_HEADS, CHUNK, CHUNK)), const((RET_HEADS // 2, CHUNK, V7X_LANES)),
            const((RET_HEADS // 2, CHUNK, V7X_LANES)), const((RET_HEADS // 2, 1, V7X_LANES)),
            const((1, RET_WIDTH)), const((1, SWA_WIDTH)), smem,
            const((1, d)), const((1, d)),
        ],
        out_specs=pl.BlockSpec((None, MIX_TILE, d), lambda bi, i: (bi, i, 0)),
        out_shape=jax.ShapeDtypeStruct((b, s, d), F32),
        scratch_shapes=[
            pltpu.VMEM((MIX_TILE, IN_WIDTH), F32),
            pltpu.VMEM((MIX_TILE, MIX_WIDTH), F32),
            pltpu.VMEM((RET_HEADS // 2, V7X_LANES, V7X_LANES), F32),
            pltpu.VMEM((CHUNK, KV_WIDTH), F32),
            pltpu.VMEM((CHUNK, KV_WIDTH), F32),
        ],
        compiler_params=pltpu.CompilerParams(
            dimension_semantics=("parallel", "arbitrary"),
            vmem_limit_bytes=V7X_VMEM_LIMIT_BYTES),
    )(x, w_in.astype(BF16), w_out.astype(BF16), cr, sr, ca, sa, dec, kw, qw, cd,
      gn_gain.reshape(1, -1), attn_scale.reshape(1, -1), sinks, ln_g.reshape(1, -1), ln_b.reshape(1, -1))


def _route(x, wr_ref, bias_ref, tri_ref, between):
    tt = x.shape[0]
    xh = x.astype(BF16)
    xl = (x - xh.astype(F32)).astype(BF16)
    w = wr_ref[...]
    wh = w.astype(BF16)
    wl = (w - wh.astype(F32)).astype(BF16)
    dot = functools.partial(jnp.dot, preferred_element_type=F32)
    hi = dot(xh, jnp.concatenate([wh, wl], axis=1))
    logits = (hi[:, :N_EXPERTS] + (dot(xl, wh) + hi[:, N_EXPERTS:])).T
    scores = jax.nn.sigmoid(logits)
    biased = scores + bias_ref[...]

    shape3 = (N_GROUPS, GROUP_SIZE, tt)
    grp = lax.broadcasted_iota(jnp.int32, shape3, 0)
    mem = lax.broadcasted_iota(jnp.int32, shape3, 1)
    b3 = biased.reshape(shape3)

    m1 = jnp.max(b3, axis=1, keepdims=True)
    i1 = jnp.min(jnp.where(b3 == m1, mem, GROUP_SIZE), axis=1, keepdims=True)
    m2 = jnp.max(jnp.where(mem == i1, -jnp.inf, b3), axis=1, keepdims=True)
    gs = jnp.broadcast_to(m1 + m2, shape3)

    gmask = jnp.zeros(shape3, F32)
    for _ in range(TOPK_GROUPS):
        m = jnp.max(gs, axis=0, keepdims=True)
        pick = grp == jnp.min(jnp.where(gs == m, grp, N_GROUPS), axis=0, keepdims=True)
        gmask = jnp.where(pick, 1.0, gmask)
        gs = jnp.where(pick, -jnp.inf, gs)
        between()

    eid = grp * GROUP_SIZE + mem
    cand = jnp.where(gmask > 0.0, b3, NEG_INF)
    sel = jnp.zeros(shape3, F32)
    for _ in range(TOP_K):
        m = jnp.max(jnp.max(cand, axis=0, keepdims=True), axis=1, keepdims=True)
        first = jnp.where(cand == m, eid, N_EXPERTS)
        first = jnp.min(jnp.min(first, axis=0, keepdims=True), axis=1, keepdims=True)
        pick = eid == first
        sel = jnp.where(pick, 1.0, sel)
        cand = jnp.where(pick, -jnp.inf, cand)
        between()

    sel = sel.reshape(N_EXPERTS, tt)
    wsel = scores * sel
    comb = wsel / jnp.sum(wsel, axis=0, keepdims=True) * ROUTED_SCALE

    lane = lax.broadcasted_iota(jnp.int32, (N_EXPERTS, V7X_LANES), 1)
    cnt = jnp.zeros((N_EXPERTS, V7X_LANES), F32)
    ranks = []
    for sb in range(tt // ROUTE_SUB):
        cols = slice(sb * ROUTE_SUB, (sb + 1) * ROUTE_SUB)
        sel_sb = sel[:, cols]
        rank = jnp.dot(sel_sb.astype(BF16), tri_ref[...], preferred_element_type=F32)
        ranks.append(jnp.where(sel_sb > 0.0, rank, -1.0))
        cnt = jnp.where(lane == sb, jnp.sum(sel_sb, axis=1, keepdims=True), cnt)
    return comb, jnp.concatenate(ranks, axis=1), cnt


def _route_dispatch_kernel(x_ref, wr_ref, bias_ref, tri_ref, wt_ref, rk_ref, cnt_ref, xs_ref, xb_s, rk_s):
    step = pl.program_id(0)

    @pl.when(step == 0)
    def _():
        xb_s[...] = jnp.zeros_like(xb_s)
        rk_s[...] = jnp.full(rk_s.shape, -1.0, F32)

    prev = (step + 1) % 2
    one = jnp.ones((), BF16)

    def place(sb, g):
        cols = slice(sb * ROUTE_SUB, (sb + 1) * ROUTE_SUB)
        onehot = _slot_onehot(lambda e: rk_s[prev, e:e + 1, cols], g, lambda e: one)
        rows = jnp.dot(onehot, xb_s[prev, cols, :], preferred_element_type=F32).astype(BF16)
        xs_ref[g * SLOT_GROUP:(g + 1) * SLOT_GROUP, sb * SLOT_ROWS:(sb + 1) * SLOT_ROWS, :] = (
            rows.reshape(SLOT_GROUP, SLOT_ROWS, rows.shape[1]))

    units = iter([functools.partial(place, sb, g) for sb in range(ROUTE_TILE // ROUTE_SUB)
                  for g in range(N_EXPERTS // SLOT_GROUP)])
    x = x_ref[...]
    comb, ranks, cnt = _route(x, wr_ref, bias_ref, tri_ref, lambda: next(units, lambda: None)())
    for unit in units:
        unit()

    wt_ref[...] = comb
    rk_ref[...] = ranks
    cnt_ref[...] = cnt
    xb_s[step % 2] = x.astype(BF16)
    rk_s[step % 2] = ranks


def _route_dispatch(x1, w_router, router_bias):
    t, d = x1.shape
    n_tiles = t // ROUTE_TILE
    tile_rows = ROUTE_TILE // ROUTE_SUB * SLOT_ROWS
    tri = (jnp.arange(ROUTE_SUB)[:, None] < jnp.arange(ROUTE_SUB)[None, :]).astype(BF16)
    routed = lambda s: jnp.minimum(s, n_tiles - 1)
    placed = lambda s: jnp.maximum(s - 1, 0)
    return pl.pallas_call(
        _route_dispatch_kernel,
        name="route_dispatch",
        grid=(n_tiles + 1,),
        in_specs=[
            pl.BlockSpec((ROUTE_TILE, d), lambda s: (routed(s), 0)),
            pl.BlockSpec((d, N_EXPERTS), lambda s: (0, 0)),
            pl.BlockSpec((N_EXPERTS, 1), lambda s: (0, 0)),
            pl.BlockSpec((ROUTE_SUB, ROUTE_SUB), lambda s: (0, 0)),
        ],
        out_specs=[
            pl.BlockSpec((N_EXPERTS, ROUTE_TILE), lambda s: (0, routed(s))),
            pl.BlockSpec((N_EXPERTS, ROUTE_TILE), lambda s: (0, routed(s))),
            pl.BlockSpec((None, N_EXPERTS, V7X_LANES), lambda s: (routed(s), 0, 0)),
            pl.BlockSpec((N_EXPERTS, tile_rows, d), lambda s: (0, placed(s), 0)),
        ],
        out_shape=[
            jax.ShapeDtypeStruct((N_EXPERTS, t), F32),
            jax.ShapeDtypeStruct((N_EXPERTS, t), F32),
            jax.ShapeDtypeStruct((n_tiles, N_EXPERTS, V7X_LANES), F32),
            jax.ShapeDtypeStruct((N_EXPERTS, n_tiles * tile_rows, d), BF16),
        ],
        scratch_shapes=[
            pltpu.VMEM((2, ROUTE_TILE, d), BF16),
            pltpu.VMEM((2, N_EXPERTS, ROUTE_TILE), F32),
        ],
        compiler_params=pltpu.CompilerParams(
            dimension_semantics=("arbitrary",),
            vmem_limit_bytes=V7X_VMEM_LIMIT_BYTES),
    )(x1, w_router, router_bias.reshape(-1, 1), tri)


def _slot_onehot(rank_row, group, values):
    rid = lax.broadcasted_iota(jnp.int32, (SLOT_ROWS, ROUTE_SUB), 0).astype(F32).astype(BF16)
    pieces = []
    for i in range(SLOT_GROUP):
        e = group * SLOT_GROUP + i
        pieces.append(jnp.where(rid == rank_row(e).astype(BF16), values(e), jnp.zeros((), BF16)))
    return jnp.concatenate(pieces, axis=0)


def _expert_ffn(x, w1, w3, w2):
    h = _silu(jnp.dot(x, w1, preferred_element_type=F32))
    h = h * jnp.dot(x, w3, preferred_element_type=F32)
    return jnp.dot(h.astype(BF16), w2, preferred_element_type=F32).astype(BF16)


def _expert_kernel(xs_ref, w1_ref, w3_ref, w2_ref, ys_ref, w1b_ref, w3b_ref, w2b_ref, w1_s, w3_s, w2_s):
    @pl.when(pl.program_id(1) == 0)
    def _():
        for src, dst, out in ((w1_ref, w1_s, w1b_ref), (w3_ref, w3_s, w3b_ref), (w2_ref, w2_s, w2b_ref)):
            wb = src[...].astype(BF16)
            dst[...] = wb
            out[...] = wb

    block = xs_ref.shape[0]
    tile = EXPERT_TILE if block % EXPERT_TILE == 0 else block
    for q in range(block // tile):
        rows = slice(q * tile, (q + 1) * tile)
        ys_ref[rows, :] = _expert_ffn(xs_ref[rows, :], w1_s[...], w3_s[...], w2_s[...])


def _experts(xs, w1, w3, w2):
    n_e, seg, d = xs.shape
    block = EXPERT_BLOCK if seg % EXPERT_BLOCK == 0 else seg
    row_spec = pl.BlockSpec((None, block, d), lambda e, t: (e, t, 0))
    return pl.pallas_call(
        _expert_kernel,
        name="experts",
        grid=(n_e, seg // block),
        in_specs=[
            row_spec,
            pl.BlockSpec((None, d, EXPERT_DIM), lambda e, t: (e, 0, 0)),
            pl.BlockSpec((None, d, EXPERT_DIM), lambda e, t: (e, 0, 0)),
            pl.BlockSpec((None, EXPERT_DIM, d), lambda e, t: (e, 0, 0)),
        ],
        out_specs=[
            row_spec,
            pl.BlockSpec((None, d, EXPERT_DIM), lambda e, t: (e, 0, 0)),
            pl.BlockSpec((None, d, EXPERT_DIM), lambda e, t: (e, 0, 0)),
            pl.BlockSpec((None, EXPERT_DIM, d), lambda e, t: (e, 0, 0)),
        ],
        out_shape=[
            jax.ShapeDtypeStruct(xs.shape, BF16),
            jax.ShapeDtypeStruct(w1.shape, BF16),
            jax.ShapeDtypeStruct(w3.shape, BF16),
            jax.ShapeDtypeStruct(w2.shape, BF16),
        ],
        scratch_shapes=[
            pltpu.VMEM((d, EXPERT_DIM), BF16),
            pltpu.VMEM((d, EXPERT_DIM), BF16),
            pltpu.VMEM((EXPERT_DIM, d), BF16),
        ],
        compiler_params=pltpu.CompilerParams(
            dimension_semantics=("parallel", "arbitrary"),
            vmem_limit_bytes=V7X_VMEM_LIMIT_BYTES),
    )(xs, w1, w3, w2)


def _combine_kernel(ovn_ref, ovl_ref, x_ref, p_ref, rk_ref, wt_ref, ys_ref,
                    ws1_ref, ws3_ref, ws2_ref, wg_ref, bg_ref, wp_ref, g2_ref, b2_ref,
                    w1_hbm, w3_hbm, w2_hbm, o_ref, acc_s, w1_s, w3_s, w2_s, sems):
    j = pl.program_id(0)
    n_over = ovn_ref[j]
    pairs = COMBINE_SUBS * N_EXPERTS

    def fetch(k, half):
        e = ovl_ref[j * pairs + k] % N_EXPERTS
        return [pltpu.make_async_copy(src.at[e], dst.at[half], sems.at[half, i])
                for i, (src, dst) in enumerate(((w1_hbm, w1_s), (w3_hbm, w3_s), (w2_hbm, w2_s)))]

    @pl.when(n_over > 0)
    def _():
        for cp in fetch(0, 0):
            cp.start()

    x = x_ref[...]
    xb = x.astype(BF16)
    hs = _silu(jnp.dot(xb, ws1_ref[...], preferred_element_type=F32))
    hs = hs * jnp.dot(xb, ws3_ref[...], preferred_element_type=F32)
    shared = jnp.dot(hs.astype(BF16), ws2_ref[...], preferred_element_type=F32)
    gate = jax.nn.sigmoid(jnp.dot(xb, wg_ref[...], preferred_element_type=F32) + bg_ref[...])
    side = jnp.dot(p_ref[...].astype(BF16), wp_ref[...], preferred_element_type=F32)
    acc_s[...] = DEEPNORM_ALPHA * x + shared + gate * side

    for sb in range(COMBINE_SUBS):
        cols = slice(sb * ROUTE_SUB, (sb + 1) * ROUTE_SUB)
        for g in range(N_EXPERTS // SLOT_GROUP):
            scat = _slot_onehot(lambda e: rk_ref[e:e + 1, cols], g,
                                lambda e: wt_ref[e:e + 1, cols].astype(BF16))
            rows = ys_ref[g * SLOT_GROUP:(g + 1) * SLOT_GROUP, sb * SLOT_ROWS:(sb + 1) * SLOT_ROWS, :]
            rows = rows.reshape(SLOT_GROUP * SLOT_ROWS, ys_ref.shape[2])
            acc_s[cols, :] += lax.dot_general(scat, rows, TN_DIMS, preferred_element_type=F32)

    sub = lax.broadcasted_iota(jnp.int32, (ROUTE_SUB, ROUTE_SUB), 0)
    lan = lax.broadcasted_iota(jnp.int32, (ROUTE_SUB, ROUTE_SUB), 1)

    def overflow(k, carry):
        half = k % 2

        @pl.when(k + 1 < n_over)
        def _():
            for cp in fetch(k + 1, 1 - half):
                cp.start()

        pair = ovl_ref[j * pairs + k]
        sb, e = pair // N_EXPERTS, pair % N_EXPERTS
        rank = rk_ref[pl.ds(e, 1), :]
        late = jnp.where(rank >= SLOT_ROWS, wt_ref[pl.ds(e, 1), :], 0.0)
        late = sum(jnp.where(sb == i, late[:, i * ROUTE_SUB:(i + 1) * ROUTE_SUB], 0.0)
                   for i in range(COMBINE_SUBS))
        pick = jnp.where(sub == lan, jnp.broadcast_to(late, (ROUTE_SUB, ROUTE_SUB)), 0.0).astype(BF16)
        rows = pl.ds(pl.multiple_of(sb * ROUTE_SUB, ROUTE_SUB), ROUTE_SUB)
        for cp in fetch(k, half):
            cp.wait()
        y = _expert_ffn(x_ref[rows, :].astype(BF16), w1_s[half], w3_s[half], w2_s[half])
        acc_s[rows, :] += jnp.dot(pick, y, preferred_element_type=F32)
        return carry

    lax.fori_loop(0, n_over, overflow, 0)
    o_ref[...] = _layer_norm(acc_s[...], g2_ref[...], b2_ref[...])


def _combine(x1, p, rk, wt, ys, ov_n, ov_list, w1, w3, w2, ws1, ws3, ws2, wg, bg, wp, ln_g, ln_b):
    t, d = x1.shape
    tile = COMBINE_SUBS * ROUTE_SUB
    const = lambda shape: pl.BlockSpec(shape, lambda j, *_: (0,) * len(shape))
    hbm = pl.BlockSpec(memory_space=pl.ANY)
    return pl.pallas_call(
        _combine_kernel,
        name="combine",
        grid_spec=pltpu.PrefetchScalarGridSpec(
            num_scalar_prefetch=2,
            grid=(t // tile,),
            in_specs=[
                pl.BlockSpec((tile, d), lambda j, *_: (j, 0)),
                pl.BlockSpec((tile, PLE_DIM), lambda j, *_: (j, 0)),
                pl.BlockSpec((N_EXPERTS, tile), lambda j, *_: (0, j)),
                pl.BlockSpec((N_EXPERTS, tile), lambda j, *_: (0, j)),
                pl.BlockSpec((N_EXPERTS, COMBINE_SUBS * SLOT_ROWS, d), lambda j, *_: (0, j, 0)),
                const((d, SHARED_DIM)), const((d, SHARED_DIM)), const((SHARED_DIM, d)),
                const((d, d)), const((1, d)), const((PLE_DIM, d)), const((1, d)), const((1, d)),
                hbm, hbm, hbm,
            ],
            out_specs=pl.BlockSpec((tile, d), lambda j, *_: (j, 0)),
            scratch_shapes=[
                pltpu.VMEM((tile, d), F32),
                pltpu.VMEM((2, d, EXPERT_DIM), BF16),
                pltpu.VMEM((2, d, EXPERT_DIM), BF16),
                pltpu.VMEM((2, EXPERT_DIM, d), BF16),
                pltpu.SemaphoreType.DMA((2, 3)),
            ],
        ),
        out_shape=jax.ShapeDtypeStruct((t, d), F32),
        compiler_params=pltpu.CompilerParams(
            dimension_semantics=("arbitrary",),
            vmem_limit_bytes=V7X_VMEM_LIMIT_BYTES),
    )(ov_n, ov_list, x1, p, rk, wt, ys,
      ws1.astype(BF16), ws3.astype(BF16), ws2.astype(BF16),
      wg.astype(BF16), bg.reshape(1, -1), wp.astype(BF16), ln_g.reshape(1, -1), ln_b.reshape(1, -1),
      w1, w3, w2)


def _overflow_lists(cnt):
    pairs = COMBINE_SUBS * N_EXPERTS
    over = (cnt > SLOT_ROWS).reshape(-1, pairs)
    place = jnp.cumsum(over, axis=1) - 1
    slot = jnp.arange(pairs)
    hit = over[:, :, None] & (place[:, :, None] == slot[None, None, :])
    ov_list = jnp.sum(jnp.where(hit, slot[None, :, None], 0), axis=1)
    return jnp.sum(over, axis=1).astype(jnp.int32), ov_list.reshape(-1).astype(jnp.int32)


def _moe_block(x1, p, w_router, router_bias, w1, w3, w2, ws1, ws3, ws2, wg, bg, wp, ln_g, ln_b):
    t, d = x1.shape
    n_sub = t // ROUTE_SUB
    wt, rk, cnt, xs = _route_dispatch(x1, w_router, router_bias)
    per_step = ROUTE_TILE // ROUTE_SUB
    cnt = jnp.swapaxes(cnt[:, :, :per_step], 1, 2).reshape(n_sub, N_EXPERTS).astype(jnp.int32)
    ov_n, ov_list = _overflow_lists(cnt)
    ys, w1b, w3b, w2b = _experts(xs, w1, w3, w2)
    return _combine(x1, p, rk, wt, ys, ov_n, ov_list, w1b, w3b, w2b, ws1, ws3, ws2, wg, bg, wp, ln_g, ln_b)


def kernel(x, p, w_in, ret_gn_gain, attn_scale, sinks, w_out, ln1_g, ln1_b, w_router, router_bias,
           w1, w3, w2, ws1, ws3, ws2, w_ple_gate, b_ple_gate, w_ple_proj, ln2_g, ln2_b):
    b, s, d = x.shape
    h = x
    for i in range(DEPTH):
        h = _mixer(h, w_in[i], w_out[i], ret_gn_gain[i], attn_scale[i], sinks[i], ln1_g[i], ln1_b[i])
        h2 = _moe_block(h.reshape(b * s, d), p[i].reshape(b * s, -1), w_router[i], router_bias[i],
                        w1[i], w3[i], w2[i], ws1[i], ws3[i], ws2[i],
                        w_ple_gate[i], b_ple_gate[i], w_ple_proj[i], ln2_g[i], ln2_b[i])
        h = h2.reshape(b, s, d)
    return h
```
